```python
import math, functools
import jax, jax.numpy as jnp
from jax import lax
import numpy as np


D_MODEL = 1024
BATCH = 8
SEQ = 4096
DEPTH = 2

CTX_LEN = 256
GRID_W = 64
HEAD_DIM = 64
MIX_W = D_MODEL
ATT_W = MIX_W // 2
RET_W = MIX_W // 4
MLSTM_W = MIX_W - ATT_W - RET_W
ATT_HEADS = ATT_W // HEAD_DIM
ATT_KV_HEADS = ATT_HEADS // 4
KV_W = ATT_KV_HEADS * HEAD_DIM
RET_HEADS = RET_W // HEAD_DIM
MLSTM_HEADS = MLSTM_W // HEAD_DIM
N_GATES = 4
SPLIT_SIZES = (ATT_W, KV_W, KV_W, RET_W, RET_W, RET_W, RET_W, MLSTM_W, MLSTM_W, MLSTM_W, MLSTM_W, N_GATES * MLSTM_HEADS)
N_IN = sum(SPLIT_SIZES)
CHUNK = 128
Q_BLOCK = 128
CONV_W = 3
ROPE_THETA = 10000.0
D_FF = 2816
N_EXPERTS = 8
TOP_K = 2
D_FF_EXPERT = 3584
MOE_BLOCK = 256
N_DENSE = (DEPTH + 1) // 2
N_MOE = DEPTH // 2
EPS = 1e-6
F32 = jnp.float32

kernel_name = 'hybrid_dit_retention_gqa_mlstm_moe'


def rms_norm(x, w):
    xf = x.astype(F32)
    y = xf * lax.rsqrt(jnp.mean(jnp.square(xf), -1, keepdims=True) + EPS)
    return (y * w.astype(F32)).astype(x.dtype)


def head_layer_norm(x, w):
    xf = x.astype(F32)
    mu = jnp.mean(xf, -1, keepdims=True)
    var = jnp.mean(jnp.square(xf - mu), -1, keepdims=True)
    g = w.astype(F32).reshape(1, x.shape[1], 1, x.shape[3])
    return (xf - mu) * lax.rsqrt(var + EPS) * g


def modulate(x, shift, scale):
    return x * (1.0 + scale) + shift


def adaln_chunks(cond, w, b):
    return jnp.split(jax.nn.silu(cond) @ w + b, 6, axis=-1)


def to_heads(x):
    b, t, w = x.shape
    return x.reshape(b, t, w // HEAD_DIM, HEAD_DIM).transpose(0, 2, 1, 3)


def from_heads(x):
    b, h, t, d = x.shape
    return x.transpose(0, 2, 1, 3).reshape(b, t, h * d)


def axial_rope_tables(n_tokens):
    rows = n_tokens // GRID_W
    row = jnp.repeat(jnp.arange(rows, dtype=F32), GRID_W)
    col = jnp.tile(jnp.arange(GRID_W, dtype=F32), rows)
    n_freq = HEAD_DIM // 4
    inv_freq = ROPE_THETA ** (-jnp.arange(n_freq, dtype=F32) / n_freq)
    ang = jnp.concatenate([row[:, None] * inv_freq, col[:, None] * inv_freq], -1)
    return jnp.cos(ang), jnp.sin(ang)


def apply_rope(x, cos, sin):
    half = x.shape[-1] // 2
    x1, x2 = x[..., :half], x[..., half:]
    return jnp.concatenate([x1 * cos - x2 * sin, x1 * sin + x2 * cos], -1)


def conv_centered(x, w):
    ch = x.shape[-1]
    return lax.conv_general_dilated(x, w[:, None, :].astype(x.dtype), window_strides=(1,), padding='SAME',
                                    dimension_numbers=('NWC', 'WIO', 'NWC'), feature_group_count=ch)


def _flip(t):
    return jnp.flip(t, axis=2)


def _chunks(x):
    return x.reshape(x.shape[:2] + (x.shape[2] // CHUNK, CHUNK) + x.shape[3:])


def group_queries(q):
    b, hq, t, d = q.shape
    return q.reshape(b, ATT_KV_HEADS, hq // ATT_KV_HEADS, t, d)


def attend(qg, keys, vals):
    s = jnp.einsum('bhgqd,bhkd->bhgqk', qg, keys).astype(F32) * (HEAD_DIM ** -0.5)
    p = jax.nn.softmax(s, axis=-1).astype(vals.dtype)
    return jnp.einsum('bhgqk,bhkd->bhgqd', p, vals)


def latent_attention(q, k, v, k_ctx, v_ctx):
    b, hq, t, d = q.shape
    keys = jnp.concatenate([k_ctx, k], axis=2)
    vals = jnp.concatenate([v_ctx, v], axis=2)
    nb = t // Q_BLOCK
    qb = jnp.moveaxis(group_queries(q).reshape(b, ATT_KV_HEADS, hq // ATT_KV_HEADS, nb, Q_BLOCK, d), 3, 0)
    ob = lax.map(lambda qi: attend(qi, keys, vals), qb)
    return jnp.moveaxis(ob, 0, 3).reshape(b, hq, t, d)


def retention_states(k, v, log_gamma, s0):
    kc, vc = _chunks(k.astype(F32)), _chunks(v.astype(F32))
    j = jnp.arange(CHUNK, dtype=F32)
    k_dec = kc * jnp.exp((CHUNK - 1.0 - j)[None, :] * log_gamma[:, None])[None, :, None, :, None]
    s_inc = jnp.einsum('bhcsk,bhcsv->cbhkv', k_dec, vc)
    chunk_decay = jnp.exp(CHUNK * log_gamma)[None, :, None, None]

    def step(s, inc):
        return chunk_decay * s + inc, s

    s_fin, s_start = lax.scan(step, s0, s_inc)
    return s_start, s_fin


def retention_out(q, k, v, log_gamma, s_start):
    qc, kc, vc = (_chunks(t.astype(F32)) for t in (q, k, v))
    i = jnp.arange(CHUNK, dtype=F32)
    lag = i[:, None] - i[None, :]
    decay = jnp.where(lag >= 0, jnp.exp(jnp.maximum(lag, 0.0)[None] * log_gamma[:, None, None]), 0.0)
    scores = jnp.einsum('bhcld,bhcsd->bhcls', qc, kc) * decay[None, :, None]
    intra = jnp.einsum('bhcls,bhcsv->bhclv', scores, vc)
    q_dec = qc * jnp.exp((i + 1.0)[None, :] * log_gamma[:, None])[None, :, None, :, None]
    cross = jnp.einsum('bhcld,cbhdv->bhclv', q_dec, s_start)
    o = intra + cross
    return o.reshape(q.shape[0], q.shape[1], -1, o.shape[-1])


def bidir_retention_states(k, v, log_gamma, init_f, init_b):
    st_f, fin_f = retention_states(k, v, log_gamma[0], init_f)
    st_b, fin_b = retention_states(_flip(k), _flip(v), log_gamma[1], init_b)
    return (st_f, st_b), (fin_f, fin_b)


def bidir_retention_out(q, k, v, log_gamma, starts):
    st_f, st_b = starts
    fwd = retention_out(q, k, v, log_gamma[0], st_f)
    bwd = _flip(retention_out(_flip(q), _flip(k), _flip(v), log_gamma[1], st_b))
    return fwd + bwd


def _mlstm_logs(ig, lf):
    return _chunks(ig), jnp.cumsum(_chunks(lf), axis=-1)


def mlstm_states(k, v, ig, lf, state0):
    kc, vc = _chunks(k.astype(F32)), _chunks(v.astype(F32))
    igc, b = _mlstm_logs(ig, lf)
    b_end = b[..., -1]
    a = b_end[..., None] - b + igc
    m_loc = jnp.max(a, axis=-1)
    kw = kc * jnp.exp(a - m_loc[..., None])[..., None]
    c_inc = jnp.einsum('bhcsk,bhcsv->cbhkv', kw, vc)
    n_inc = jnp.moveaxis(jnp.sum(kw, axis=3), 2, 0)

    def step(state, inp):
        c, n, m = state
        be, ml, ci, ni = inp
        m_new = jnp.maximum(be + m, ml)
        w_prev = jnp.exp(be + m - m_new)
        w_inc = jnp.exp(ml - m_new)
        c_new = w_prev[..., None, None] * c + w_inc[..., None, None] * ci
        n_new = w_prev[..., None] * n + w_inc[..., None] * ni
        return (c_new, n_new, m_new), (c, n, m)

    xs = (jnp.moveaxis(b_end, 2, 0), jnp.moveaxis(m_loc, 2, 0), c_inc, n_inc)
    final, starts = lax.scan(step, state0, xs)
    return starts, final


def mlstm_out(q, k, v, ig, lf, starts):
    c_st, n_st, m_st = starts
    qc, kc, vc = (_chunks(t.astype(F32)) for t in (q, k, v))
    igc, b = _mlstm_logs(ig, lf)
    tri = jnp.tril(jnp.ones((CHUNK, CHUNK), dtype=bool))
    log_d = jnp.where(tri, b[..., :, None] - b[..., None, :] + igc[..., None, :], -jnp.inf)
    g = b + jnp.moveaxis(m_st, 0, 2)[..., None]
    m_t = jnp.maximum(g, jnp.max(log_d, axis=-1))
    s = jnp.einsum('bhcld,bhcsd->bhcls', qc, kc) * jnp.exp(log_d - m_t[..., None])
    w_prev = jnp.exp(g - m_t)
    num = jnp.einsum('bhcls,bhcsv->bhclv', s, vc) + w_prev[..., None] * jnp.einsum('bhcld,cbhdv->bhclv', qc, c_st)
    den = jnp.sum(s, axis=-1) + w_prev * jnp.einsum('bhcld,cbhd->bhcl', qc, n_st)
    h = num / jnp.maximum(jnp.abs(den), jnp.exp(-m_t))[..., None]
    return h.reshape(q.shape[0], q.shape[1], -1, h.shape[-1])


def bidir_mlstm_states(k, v, ig, lf, init_f, init_b):
    st_f, fin_f = mlstm_states(k, v, ig[0], lf[0], init_f)
    st_b, fin_b = mlstm_states(_flip(k), _flip(v), _flip(ig[1]), _flip(lf[1]), init_b)
    return (st_f, st_b), (fin_f, fin_b)


def bidir_mlstm_out(q, k, v, ig, lf, starts):
    st_f, st_b = starts
    fwd = mlstm_out(q, k, v, ig[0], lf[0], st_f)
    bwd = _flip(mlstm_out(_flip(q), _flip(k), _flip(v), _flip(ig[1]), _flip(lf[1]), st_b))
    return fwd + bwd


def prepare_heads(p, rope, qn_w, kn_w, conv_w, gate_b):
    b, t, _ = p.shape
    offsets = [int(o) for o in np.cumsum(SPLIT_SIZES)[:-1]]
    aq, ak, av, rq, rk, rv, rg, mq, mk, mv, mo, mg = jnp.split(p, offsets, axis=-1)
    aq = rms_norm(to_heads(aq), qn_w)
    ak = rms_norm(to_heads(ak), kn_w)
    rq, rk = to_heads(rq), to_heads(rk) * (HEAD_DIM ** -0.5)
    if rope is not None:
        aq, ak, rq, rk = (apply_rope(z, rope[0], rope[1]) for z in (aq, ak, rq, rk))
    mqk = jax.nn.silu(conv_centered(jnp.concatenate([mq, mk], axis=-1), conv_w))
    gates = (mg.astype(F32).reshape(b, t, N_GATES, MLSTM_HEADS) + gate_b.astype(F32)).transpose(2, 0, 3, 1)
    return {'aq': aq, 'ak': ak, 'av': to_heads(av),
            'rq': rq, 'rk': rk, 'rv': to_heads(rv), 'rg': rg,
            'mq': to_heads(mqk[..., :MLSTM_W]), 'mk': to_heads(mqk[..., MLSTM_W:]) * (HEAD_DIM ** -0.5),
            'mv': to_heads(mv), 'mo': mo, 'ig': gates[:2], 'lf': jax.nn.log_sigmoid(gates[2:])}


def token_mixers(p_ctx, p_lat, rope, qn_w, kn_w, ret_decay, ret_norm_w, conv_w, gate_b, mlstm_norm_w, need_ctx_out):
    dt = p_lat.dtype
    b = p_lat.shape[0]
    cx = prepare_heads(p_ctx, None, qn_w, kn_w, conv_w, gate_b)
    lt = prepare_heads(p_lat, rope, qn_w, kn_w, conv_w, gate_b)
    log_gamma = jnp.log1p(-jnp.exp(ret_decay.astype(F32)))
    s0 = jnp.zeros((b, RET_HEADS, HEAD_DIM, HEAD_DIM), F32)
    m0 = (jnp.zeros((b, MLSTM_HEADS, HEAD_DIM, HEAD_DIM), F32), jnp.zeros((b, MLSTM_HEADS, HEAD_DIM), F32),
          jnp.zeros((b, MLSTM_HEADS), F32))
    ret_st_c, ret_fin_c = bidir_retention_states(cx['rk'], cx['rv'], log_gamma, s0, s0)
    ret_st_l, _ = bidir_retention_states(lt['rk'], lt['rv'], log_gamma, ret_fin_c[0], ret_fin_c[1])
    ml_st_c, ml_fin_c = bidir_mlstm_states(cx['mk'], cx['mv'], cx['ig'], cx['lf'], m0, m0)
    ml_st_l, _ = bidir_mlstm_states(lt['mk'], lt['mv'], lt['ig'], lt['lf'], ml_fin_c[0], ml_fin_c[1])

    def merge(d, att, ret_st, ml_st):
        ret = bidir_retention_out(d['rq'], d['rk'], d['rv'], log_gamma, ret_st)
        ret = from_heads(head_layer_norm(ret, ret_norm_w)).astype(dt) * jax.nn.silu(d['rg'])
        ml = bidir_mlstm_out(d['mq'], d['mk'], d['mv'], d['ig'], d['lf'], ml_st)
        ml = ml * jax.nn.sigmoid(to_heads(d['mo']).astype(F32))
        ml = from_heads(head_layer_norm(ml, mlstm_norm_w)).astype(dt)
        return jnp.concatenate([from_heads(att), ret, ml], axis=-1)

    att_l = latent_attention(lt['aq'], lt['ak'], lt['av'], cx['ak'], cx['av'])
    out_l = merge(lt, att_l, ret_st_l, ml_st_l)
    out_c = None
    if need_ctx_out:
        qc = cx['aq']
        att_c = attend(group_queries(qc), cx['ak'], cx['av']).reshape(qc.shape)
        out_c = merge(cx, att_c, ret_st_c, ml_st_c)
    return out_c, out_l


def swiglu(x, w_gate, w_up, w_down):
    return (jax.nn.silu(x @ w_gate) * (x @ w_up)) @ w_down


def moe_swiglu(x, router_w, router_b, w_gate, w_up, w_down):
    shp = x.shape
    xt = x.reshape(-1, shp[-1])
    n_tok = xt.shape[0]
    logits = (xt @ router_w + router_b).astype(F32)
    top_v, top_i = lax.top_k(logits, TOP_K)
    top_w = jax.nn.softmax(top_v, axis=-1)
    n_assign = n_tok * TOP_K
    flat_e = top_i.reshape(-1)
    order = jnp.argsort(flat_e)
    sorted_e = flat_e[order]
    sorted_tok = order // TOP_K
    sorted_w = top_w.reshape(-1)[order]
    counts = jnp.bincount(flat_e, length=N_EXPERTS)
    padded = (counts + MOE_BLOCK - 1) // MOE_BLOCK * MOE_BLOCK
    starts = jnp.cumsum(counts) - counts
    pad_ends = jnp.cumsum(padded)
    pad_starts = pad_ends - padded
    slot = pad_starts[sorted_e] + jnp.arange(n_assign) - starts[sorted_e]
    n_blocks = -(-n_assign // MOE_BLOCK) + N_EXPERTS
    n_slots = n_blocks * MOE_BLOCK
    slot_tok = jnp.zeros((n_slots,), jnp.int32).at[slot].set(sorted_tok.astype(jnp.int32))
    slot_w = jnp.zeros((n_slots,), F32).at[slot].set(sorted_w)
    block_e = jnp.minimum(jnp.searchsorted(pad_ends, jnp.arange(n_blocks) * MOE_BLOCK, side='right'), N_EXPERTS - 1)
    xb = xt[slot_tok].reshape(n_blocks, MOE_BLOCK, shp[-1])

    def expert_block(args):
        xi, e = args
        return swiglu(xi, w_gate[e], w_up[e], w_down[e])

    yb = lax.map(expert_block, (xb, block_e)).reshape(n_slots, shp[-1])
    y = jax.ops.segment_sum(yb * slot_w[:, None].astype(yb.dtype), slot_tok, num_segments=n_tok)
    return y.reshape(shp)


def setup_inputs(seed: int = 0) -> dict:
    key = jax.random.key(seed)
    ks = jax.random.split(key, 26)
    D = D_MODEL

    def nrm(k, shape, scale):
        return jax.random.normal(k, shape, F32) * scale

    ret_base = -(5.0 + jnp.arange(RET_HEADS, dtype=F32)) * math.log(2.0)
    fgate = jnp.linspace(3.0, 6.0, MLSTM_HEADS, dtype=F32)
    zero_h = jnp.zeros((MLSTM_HEADS,), F32)
    gate_base = jnp.stack([zero_h, zero_h, fgate, fgate])
    return {
        'x': nrm(ks[0], (BATCH, SEQ, D), 1.0),
        'c': nrm(ks[1], (BATCH, D), 1.0),
        'ctx': nrm(ks[2], (BATCH, CTX_LEN, D), 1.0),
        'c_ctx': nrm(ks[3], (D,), 1.0),
        'mod_w': nrm(ks[4], (DEPTH, D, 6 * D), 0.5 * D ** -0.5),
        'mod_b': nrm(ks[5], (DEPTH, 6 * D), 0.02),
        'norm1_w': 1.0 + nrm(ks[6], (DEPTH, D), 0.02),
        'norm2_w': 1.0 + nrm(ks[7], (DEPTH, D), 0.02),
        'w_in': nrm(ks[8], (DEPTH, D, N_IN), D ** -0.5),
        'attn_qn_w': 1.0 + nrm(ks[9], (DEPTH, HEAD_DIM), 0.02),
        'attn_kn_w': 1.0 + nrm(ks[10], (DEPTH, HEAD_DIM), 0.02),
        'ret_decay': ret_base + nrm(ks[11], (DEPTH, 2, RET_HEADS), 0.05),
        'ret_norm_w': 1.0 + nrm(ks[12], (DEPTH, RET_W), 0.02),
        'mlstm_conv_w': nrm(ks[13], (DEPTH, CONV_W, 2 * MLSTM_W), CONV_W ** -0.5),
        'mlstm_gate_b': gate_base + nrm(ks[14], (DEPTH, N_GATES, MLSTM_HEADS), 0.1),
        'mlstm_norm_w': 1.0 + nrm(ks[15], (DEPTH, MLSTM_W), 0.02),
        'w_out': nrm(ks[16], (DEPTH, MIX_W, D), MIX_W ** -0.5),
        'ffn_w_gate': nrm(ks[17], (N_DENSE, D, D_FF), D ** -0.5),
        'ffn_w_up': nrm(ks[18], (N_DENSE, D, D_FF), D ** -0.5),
        'ffn_w_down': nrm(ks[19], (N_DENSE, D_FF, D), D_FF ** -0.5),
        'router_w': nrm(ks[20], (N_MOE, D, N_EXPERTS), D ** -0.5),
        'router_b': nrm(ks[21], (N_MOE, N_EXPERTS), 0.01),
        'moe_w_gate': nrm(ks[22], (N_MOE, N_EXPERTS, D, D_FF_EXPERT), D ** -0.5),
        'moe_w_up': nrm(ks[23], (N_MOE, N_EXPERTS, D, D_FF_EXPERT), D ** -0.5),
        'moe_w_down': nrm(ks[24], (N_MOE, N_EXPERTS, D_FF_EXPERT, D), D_FF_EXPERT ** -0.5),
    }


def reference(x, c, ctx, c_ctx, mod_w, mod_b, norm1_w, norm2_w, w_in, attn_qn_w, attn_kn_w, ret_decay,
              ret_norm_w, mlstm_conv_w, mlstm_gate_b, mlstm_norm_w, w_out, ffn_w_gate, ffn_w_up, ffn_w_down,
              router_w, router_b, moe_w_gate, moe_w_up, moe_w_down):
    cos, sin = axial_rope_tables(x.shape[1])
    rope = (cos.astype(x.dtype), sin.astype(x.dtype))
    h_lat, h_ctx = x, ctx
    for layer in range(DEPTH):
        last = layer == DEPTH - 1
        mod_l = [m[:, None, :] for m in adaln_chunks(c, mod_w[layer], mod_b[layer])]
        mod_c = adaln_chunks(c_ctx, mod_w[layer], mod_b[layer])
        a_lat = modulate(rms_norm(h_lat, norm1_w[layer]), mod_l[0], mod_l[1])
        a_ctx = modulate(rms_norm(h_ctx, norm1_w[layer]), mod_c[0], mod_c[1])
        mix_c, mix_l = token_mixers(a_ctx @ w_in[layer], a_lat @ w_in[layer], rope, attn_qn_w[layer],
                                    attn_kn_w[layer], ret_decay[layer], ret_norm_w[layer], mlstm_conv_w[layer],
                                    mlstm_gate_b[layer], mlstm_norm_w[layer], not last)
        h_lat = h_lat + mod_l[2] * (mix_l @ w_out[layer])
        if not last:
            h_ctx = h_ctx + mod_c[2] * (mix_c @ w_out[layer])
        if layer % 2 == 0:
            ffn = functools.partial(swiglu, w_gate=ffn_w_gate[layer // 2], w_up=ffn_w_up[layer // 2],
                                    w_down=ffn_w_down[layer // 2])
        else:
            ffn = functools.partial(moe_swiglu, router_w=router_w[layer // 2], router_b=router_b[layer // 2],
                                    w_gate=moe_w_gate[layer // 2], w_up=moe_w_up[layer // 2],
                                    w_down=moe_w_down[layer // 2])
        f_lat = modulate(rms_norm(h_lat, norm2_w[layer]), mod_l[3], mod_l[4])
        h_lat = h_lat + mod_l[5] * ffn(f_lat)
        if not last:
            f_ctx = modulate(rms_norm(h_ctx, norm2_w[layer]), mod_c[3], mod_c[4])
            h_ctx = h_ctx + mod_c[5] * ffn(f_ctx)
    return h_lat
```

```python
import functools
import math

import jax
import jax.numpy as jnp
from jax import lax
from jax.experimental import pallas as pl
from jax.experimental.pallas import tpu as pltpu

F32 = jnp.float32
BF16 = jnp.bfloat16

HEAD_DIM = 64
GRID_W = 64
ROPE_THETA = 10000.0
EPS = 1e-6
CHUNK = 128
TOKEN_TILE = 256
KV_TILE = 512
N_GATES = 4
TOP_K = 2
MOE_TILE = 512
MOE_FF_TILE = 512
VMEM_LIMIT = 56 * 1024 * 1024
QK_SCALE = HEAD_DIM ** -0.5


def _cparams(*sem):
    return pltpu.CompilerParams(dimension_semantics=sem, vmem_limit_bytes=VMEM_LIMIT)


def _silu(x):
    return x * jax.nn.sigmoid(x)


def _split_dot(x, ones_bf16, parts=2):
    acc = None
    r = x
    for _ in range(parts):
        hi = r.astype(BF16)
        d = jnp.dot(hi, ones_bf16, preferred_element_type=F32)
        acc = d if acc is None else acc + d
        r = r - hi.astype(F32)
    return acc


def _dot_nt(a, b):
    return lax.dot_general(a, b, (((1,), (1,)), ((), ())), preferred_element_type=F32)


def _rope(x, cos_t, sin_t):
    w = x.shape[1]
    reps = w // 128
    c = jnp.concatenate([cos_t] * reps, axis=1) if reps > 1 else cos_t
    s = jnp.concatenate([sin_t] * reps, axis=1) if reps > 1 else sin_t
    lane = lax.broadcasted_iota(jnp.int32, x.shape, 1)
    first_half = (lane % HEAD_DIM) < (HEAD_DIM // 2)
    swapped = jnp.where(first_half, pltpu.roll(x, w - HEAD_DIM // 2, 1), pltpu.roll(x, HEAD_DIM // 2, 1))
    return x * c + swapped * s


def _group_rms(x, gmat, gain):
    ms = _split_dot(x * x, gmat) * (1.0 / HEAD_DIM)
    return x * lax.rsqrt(ms + EPS) * gain


def _group_layer_norm(x, gmat, gain):
    mu = _split_dot(x, gmat) * (1.0 / HEAD_DIM)
    xc = x - mu
    var = _split_dot(xc * xc, gmat) * (1.0 / HEAD_DIM)
    return xc * lax.rsqrt(var + EPS) * gain


def _expand_heads(x, n_heads, lane):
    out = x[:, n_heads - 1:n_heads]
    for h in range(n_heads - 2, -1, -1):
        out = jnp.where(lane < (h + 1) * HEAD_DIM, x[:, h:h + 1], out)
    return out


def _block_diag_mask(n):
    r = lax.broadcasted_iota(jnp.int32, (n, n), 0) // HEAD_DIM
    c = lax.broadcasted_iota(jnp.int32, (n, n), 1) // HEAD_DIM
    return r == c


def _adaln_kernel(c_ref, w_ref, b_ref, o_ref):
    a = _silu(c_ref[...]).astype(BF16)
    o_ref[...] = jnp.dot(a, w_ref[...].astype(BF16), preferred_element_type=F32) + b_ref[...]


def adaln(cond, w, b):
    r, d = cond.shape
    n = w.shape[1]
    tn = 1536 if n % 1536 == 0 else n
    return pl.pallas_call(
        _adaln_kernel,
        grid=(n // tn,),
        in_specs=[pl.BlockSpec((r, d), lambda j: (0, 0)),
                  pl.BlockSpec((d, tn), lambda j: (0, j)),
                  pl.BlockSpec((1, tn), lambda j: (0, j))],
        out_specs=pl.BlockSpec((r, tn), lambda j: (0, j)),
        out_shape=jax.ShapeDtypeStruct((r, n), F32),
        compiler_params=_cparams("arbitrary"),
        name="adaln",
    )(cond, w, b.reshape(1, n))


def _norm_mod(x, nw, shift, scale):
    ms = jnp.mean(x * x, axis=-1, keepdims=True)
    y = x * lax.rsqrt(ms + EPS) * nw
    return y * (1.0 + scale) + shift


def _proj_in_kernel(h_ref, mod_ref, nw_ref, cos_ref, sin_ref, w_ref, wgt_ref, qn_ref, kn_ref, gq_ref, gk_ref,
                    q_ref, k_ref, v_ref, r_ref, m_ref, gc_ref, gr_ref, *, dims):
    att_w, kv_w, ret_w, ml_w, n_gate = dims
    x = h_ref[0]
    a = _norm_mod(x, nw_ref[...], mod_ref[0, 0:1, :], mod_ref[0, 1:2, :]).astype(BF16)
    p = jnp.dot(a, w_ref[...], preferred_element_type=F32)
    cos_t, sin_t = cos_ref[...], sin_ref[...]
    o = 0
    aq = p[:, o:o + att_w]; o += att_w
    ak = p[:, o:o + kv_w]; o += kv_w
    av = p[:, o:o + kv_w]; o += kv_w
    aq = _rope(_group_rms(aq, gq_ref[...], qn_ref[...]), cos_t, sin_t) * QK_SCALE
    ak = _rope(_group_rms(ak, gk_ref[...], kn_ref[...]), cos_t, sin_t)
    for hh in range(att_w // HEAD_DIM):
        q_ref[0, hh] = aq[:, hh * HEAD_DIM:(hh + 1) * HEAD_DIM].astype(BF16)
    tm = x.shape[0]
    lane = lax.broadcasted_iota(jnp.int32, (tm, 2 * HEAD_DIM), 1)
    for hh in range(kv_w // HEAD_DIM):
        k_ref[0, hh] = ak[:, hh * HEAD_DIM:(hh + 1) * HEAD_DIM].astype(BF16)
        vh = av[:, hh * HEAD_DIM:(hh + 1) * HEAD_DIM]
        vpad = jnp.concatenate([vh, jnp.zeros_like(vh)], axis=1)
        v_ref[0, hh] = jnp.where(lane == HEAD_DIM, 1.0, vpad).astype(BF16)
    rq = _rope(p[:, o:o + ret_w], cos_t, sin_t)
    rk = _rope(p[:, o + ret_w:o + 2 * ret_w] * QK_SCALE, cos_t, sin_t)
    r_ref[0, :, 0:ret_w] = rq
    r_ref[0, :, ret_w:2 * ret_w] = rk
    r_ref[0, :, 2 * ret_w:4 * ret_w] = p[:, o + 2 * ret_w:o + 4 * ret_w]
    o += 4 * ret_w
    m_ref[0] = p[:, o:o + 4 * ml_w]
    o += 4 * ml_w
    gc_ref[0] = p[:, o:o + n_gate]
    gr_ref[0] = _dot_nt(wgt_ref[...], a)


def proj_in(h, mod, norm_w, cos_t, sin_t, w_all, w_gate_t, qn, kn, gq, gk, *, n_ctx, dims):
    att_w, kv_w, ret_w, ml_w, n_gate = dims
    b, s, d = h.shape
    tm = TOKEN_TILE
    n_ctx_tiles = n_ctx // tm
    n_pad = w_all.shape[1]
    ctx_row = mod.shape[0] - 1

    def mod_map(bi, i):
        return (jnp.where(i < n_ctx_tiles, ctx_row, bi), 0, 0)

    const2 = lambda bi, i: (0, 0)
    out_shape = (
        jax.ShapeDtypeStruct((b, att_w // HEAD_DIM, s, HEAD_DIM), BF16),
        jax.ShapeDtypeStruct((b, kv_w // HEAD_DIM, s, HEAD_DIM), BF16),
        jax.ShapeDtypeStruct((b, kv_w // HEAD_DIM, s, 2 * HEAD_DIM), BF16),
        jax.ShapeDtypeStruct((b, s, 4 * ret_w), F32),
        jax.ShapeDtypeStruct((b, s, 4 * ml_w), F32),
        jax.ShapeDtypeStruct((b, s, n_gate), F32),
        jax.ShapeDtypeStruct((b, n_gate, s), F32),
    )
    return pl.pallas_call(
        functools.partial(_proj_in_kernel, dims=dims),
        grid=(b, s // tm),
        in_specs=[
            pl.BlockSpec((1, tm, d), lambda bi, i: (bi, i, 0)),
            pl.BlockSpec((1, 6, d), mod_map),
            pl.BlockSpec((1, d), const2),
            pl.BlockSpec((tm, 128), lambda bi, i: (i, 0)),
            pl.BlockSpec((tm, 128), lambda bi, i: (i, 0)),
            pl.BlockSpec((d, n_pad), const2),
            pl.BlockSpec((n_gate, d), const2),
            pl.BlockSpec((1, att_w), const2),
            pl.BlockSpec((1, kv_w), const2),
            pl.BlockSpec((att_w, att_w), const2),
            pl.BlockSpec((kv_w, kv_w), const2),
        ],
        out_specs=(
            pl.BlockSpec((1, att_w // HEAD_DIM, tm, HEAD_DIM), lambda bi, i: (bi, 0, i, 0)),
            pl.BlockSpec((1, kv_w // HEAD_DIM, tm, HEAD_DIM), lambda bi, i: (bi, 0, i, 0)),
            pl.BlockSpec((1, kv_w // HEAD_DIM, tm, 2 * HEAD_DIM), lambda bi, i: (bi, 0, i, 0)),
            pl.BlockSpec((1, tm, 4 * ret_w), lambda bi, i: (bi, i, 0)),
            pl.BlockSpec((1, tm, 4 * ml_w), lambda bi, i: (bi, i, 0)),
            pl.BlockSpec((1, tm, n_gate), lambda bi, i: (bi, i, 0)),
            pl.BlockSpec((1, n_gate, tm), lambda bi, i: (bi, 0, i)),
        ),
        out_shape=out_shape,
        compiler_params=_cparams("parallel", "parallel"),
        name="proj_in",
    )(h, mod, norm_w, cos_t, sin_t, w_all, w_gate_t, qn, kn, gq, gk)


def _conv_kernel(x_ref, prev_ref, next_ref, w_ref, o_ref, *, n_ctx, seq, half):
    i = pl.program_id(1)
    x = x_ref[0]
    tm, w = x.shape
    row = lax.broadcasted_iota(jnp.int32, (tm, 1), 0)
    grow = row + i * tm
    xp = jnp.where(row == 0, prev_ref[0, 7:8, :], pltpu.roll(x, 1, 0))
    xn = jnp.where(row == tm - 1, next_ref[0, 0:1, :], pltpu.roll(x, tm - 1, 0))
    xp = jnp.where((grow == 0) | (grow == n_ctx), 0.0, xp)
    xn = jnp.where((grow == n_ctx - 1) | (grow == seq - 1), 0.0, xn)
    y = _silu(xp * w_ref[0:1, :] + x * w_ref[1:2, :] + xn * w_ref[2:3, :])
    lane = lax.broadcasted_iota(jnp.int32, (tm, w), 1)
    o_ref[0] = jnp.where(lane >= half, y * QK_SCALE, y)


def mlstm_conv(m, conv_w, *, n_ctx):
    b, s, _ = m.shape
    w = conv_w.shape[1]
    tm = TOKEN_TILE
    r8 = tm // 8
    last8 = s // 8 - 1
    return pl.pallas_call(
        functools.partial(_conv_kernel, n_ctx=n_ctx, seq=s, half=w // 2),
        grid=(b, s // tm),
        in_specs=[pl.BlockSpec((1, tm, w), lambda bi, i: (bi, i, 0)),
                  pl.BlockSpec((1, 8, w), lambda bi, i: (bi, jnp.maximum(i * r8 - 1, 0), 0)),
                  pl.BlockSpec((1, 8, w), lambda bi, i: (bi, jnp.minimum((i + 1) * r8, last8), 0)),
                  pl.BlockSpec((3, w), lambda bi, i: (0, 0))],
        out_specs=pl.BlockSpec((1, tm, w), lambda bi, i: (bi, i, 0)),
        out_shape=jax.ShapeDtypeStruct((b, s, w), F32),
        compiler_params=_cparams("parallel", "parallel"),
        name="mlstm_conv",
    )(m, m, m, conv_w)


def _attn_kernel(q_ref, k_ref, v_ref, o_ref, m_sc, acc_sc, *, n_ctx, n_lat_tiles, skip):
    i = pl.program_id(2) + skip
    g, tq, hd = q_ref.shape[1], q_ref.shape[2], q_ref.shape[3]
    q = q_ref[0].reshape(g * tq, hd)

    def step(k, v):
        s = _dot_nt(q, k)
        m_old = m_sc[...]
        m_new = jnp.maximum(m_old, jnp.max(s, axis=-1, keepdims=True))
        p = jnp.exp(s - m_new)
        acc_sc[...] = jnp.exp(m_old - m_new) * acc_sc[...] + jnp.dot(p.astype(BF16), v, preferred_element_type=F32)
        m_sc[...] = m_new

    m_sc[...] = jnp.full(m_sc.shape, -jnp.inf, F32)
    acc_sc[...] = jnp.zeros(acc_sc.shape, F32)
    step(k_ref[0, 0, 0:n_ctx, :], v_ref[0, 0, 0:n_ctx, :])

    @pl.when(i * tq >= n_ctx)
    def _():
        def body(j, carry):
            start = pl.multiple_of(n_ctx + j * KV_TILE, KV_TILE if n_ctx % KV_TILE == 0 else math.gcd(n_ctx, KV_TILE))
            step(k_ref[0, 0, pl.ds(start, KV_TILE), :], v_ref[0, 0, pl.ds(start, KV_TILE), :])
            return carry
        lax.fori_loop(0, n_lat_tiles, body, 0)

    acc = acc_sc[...]
    out = acc[:, 0:hd] / acc[:, hd:hd + 1]
    o_ref[0] = jnp.concatenate([out[h * tq:(h + 1) * tq] for h in range(g)], axis=1)


def attention(q, k, v, *, n_ctx, skip):
    b, hq, s, hd = q.shape
    hkv = k.shape[1]
    g = hq // hkv
    tq = TOKEN_TILE
    n_lat = s - n_ctx
    assert n_lat % KV_TILE == 0 and n_ctx % tq == 0
    nq = s // tq - skip
    return pl.pallas_call(
        functools.partial(_attn_kernel, n_ctx=n_ctx, n_lat_tiles=n_lat // KV_TILE, skip=skip),
        grid=(b, hkv, nq),
        in_specs=[pl.BlockSpec((1, g, tq, hd), lambda bi, gi, i: (bi, gi, i + skip, 0)),
                  pl.BlockSpec((1, 1, s, hd), lambda bi, gi, i: (bi, gi, 0, 0)),
                  pl.BlockSpec((1, 1, s, 2 * hd), lambda bi, gi, i: (bi, gi, 0, 0))],
        out_specs=pl.BlockSpec((1, tq, g * hd), lambda bi, gi, i: (bi, i, gi)),
        out_shape=jax.ShapeDtypeStruct((b, nq * tq, hq * hd), F32),
        scratch_shapes=[pltpu.VMEM((g * tq, 1), F32), pltpu.VMEM((g * tq, 2 * hd), F32)],
        compiler_params=_cparams("parallel", "parallel", "arbitrary"),
        name="attention",
    )(q, k, v)


def _rev_chunk(c, nc_ctx, nc):
    return jnp.where(c < nc_ctx, nc_ctx - 1 - c, nc - 1 + nc_ctx - c)


def _ret_state_kernel(kf_ref, vf_ref, kb_ref, vb_ref, dec_ref, sf_ref, sb_ref, s_sc):
    c = pl.program_id(1)

    @pl.when(c == 0)
    def _():
        s_sc[...] = jnp.zeros(s_sc.shape, F32)

    w = kf_ref.shape[2]
    lg = jnp.log1p(-jnp.exp(dec_ref[...]))
    pos = lax.broadcasted_iota(jnp.int32, (CHUNK, 1), 0).astype(F32)
    bd = _block_diag_mask(w)
    for d, (k_ref, v_ref, out_ref) in enumerate(((kf_ref, vf_ref, sf_ref), (kb_ref, vb_ref, sb_ref))):
        lgd = lg[d:d + 1, :]
        e = (CHUNK - 1.0 - pos) if d == 0 else pos
        kdec = k_ref[0] * jnp.exp(e * lgd)
        inc = jnp.dot(kdec.T.astype(BF16), v_ref[0].astype(BF16), preferred_element_type=F32)
        s_old = s_sc[d]
        out_ref[0, 0] = s_old
        s_sc[d] = jnp.exp(CHUNK * lgd) * s_old + jnp.where(bd, inc, 0.0)


def ret_states(r, dec, *, n_ctx):
    b, s, w4 = r.shape
    w = w4 // 4
    nc, nc_ctx = s // CHUNK, n_ctx // CHUNK
    rev = lambda c: _rev_chunk(c, nc_ctx, nc)
    st = jax.ShapeDtypeStruct((b, nc, w, w), F32)
    return pl.pallas_call(
        _ret_state_kernel,
        grid=(b, nc),
        in_specs=[pl.BlockSpec((1, CHUNK, w), lambda bi, c: (bi, c, 1)),
                  pl.BlockSpec((1, CHUNK, w), lambda bi, c: (bi, c, 2)),
                  pl.BlockSpec((1, CHUNK, w), lambda bi, c: (bi, rev(c), 1)),
                  pl.BlockSpec((1, CHUNK, w), lambda bi, c: (bi, rev(c), 2)),
                  pl.BlockSpec((2, w), lambda bi, c: (0, 0))],
        out_specs=(pl.BlockSpec((1, 1, w, w), lambda bi, c: (bi, c, 0, 0)),
                   pl.BlockSpec((1, 1, w, w), lambda bi, c: (bi, rev(c), 0, 0))),
        out_shape=(st, st),
        scratch_shapes=[pltpu.VMEM((2, w, w), F32)],
        compiler_params=_cparams("parallel", "arbitrary"),
        name="ret_states",
    )(r, r, r, r, dec)


def _ret_out_kernel(r_ref, sf_ref, sb_ref, dec_ref, nw_ref, gmat_ref, o_ref):
    w = o_ref.shape[2]
    x = r_ref[0]
    q, k, v, g = x[:, 0:w], x[:, w:2 * w], x[:, 2 * w:3 * w], x[:, 3 * w:4 * w]
    lg = jnp.log1p(-jnp.exp(dec_ref[...]))
    pos = lax.broadcasted_iota(jnp.int32, (CHUNK, 1), 0).astype(F32)
    row = lax.broadcasted_iota(jnp.int32, (CHUNK, CHUNK), 0)
    col = lax.broadcasted_iota(jnp.int32, (CHUNK, CHUNK), 1)
    lane = lax.broadcasted_iota(jnp.int32, (CHUNK, w), 1)
    kb, vb = k.astype(BF16), v.astype(BF16)
    total = jnp.zeros((CHUNK, w), F32)
    for d, s_ref in enumerate((sf_ref, sb_ref)):
        lgd = lg[d:d + 1, :]
        qe = (pos + 1.0) if d == 0 else (CHUNK - pos)
        qdec = q * jnp.exp(qe * lgd)
        acc = jnp.dot(qdec.astype(BF16), s_ref[0, 0].astype(BF16), preferred_element_type=F32)
        lag = ((row - col) if d == 0 else (col - row)).astype(F32)
        for h in range(w // HEAD_DIM):
            hm = (lane >= h * HEAD_DIM) & (lane < (h + 1) * HEAD_DIM)
            sc = _dot_nt(jnp.where(hm, q, 0.0).astype(BF16), kb)
            lgh = lgd[:, h * HEAD_DIM:h * HEAD_DIM + 1]
            decay = jnp.where(lag >= 0, jnp.exp(jnp.maximum(lag, 0.0) * lgh), 0.0)
            pv = jnp.dot((sc * decay).astype(BF16), vb, preferred_element_type=F32)
            acc = acc + jnp.where(hm, pv, 0.0)
        total = total + acc
    o_ref[0] = _group_layer_norm(total, gmat_ref[...], nw_ref[...]) * _silu(g)


def ret_out(r, sf, sb, dec, norm_w, gmat, *, skip_chunks):
    b, s, w4 = r.shape
    w = w4 // 4
    nc = s // CHUNK - skip_chunks
    return pl.pallas_call(
        _ret_out_kernel,
        grid=(b, nc),
        in_specs=[pl.BlockSpec((1, CHUNK, w4), lambda bi, c: (bi, c + skip_chunks, 0)),
                  pl.BlockSpec((1, 1, w, w), lambda bi, c: (bi, c + skip_chunks, 0, 0)),
                  pl.BlockSpec((1, 1, w, w), lambda bi, c: (bi, c + skip_chunks, 0, 0)),
                  pl.BlockSpec((2, w), lambda bi, c: (0, 0)),
                  pl.BlockSpec((1, w), lambda bi, c: (0, 0)),
                  pl.BlockSpec((w, w), lambda bi, c: (0, 0))],
        out_specs=pl.BlockSpec((1, CHUNK, w), lambda bi, c: (bi, c, 0)),
        out_shape=jax.ShapeDtypeStruct((b, nc * CHUNK, w), F32),
        compiler_params=_cparams("parallel", "parallel"),
        name="ret_out",
    )(r, sf, sb, dec, norm_w, gmat)


def _tri_masks():
    row = lax.broadcasted_iota(jnp.int32, (CHUNK, CHUNK), 0)
    col = lax.broadcasted_iota(jnp.int32, (CHUNK, CHUNK), 1)
    return col <= row, col >= row


def _ones_dot(ones_bf16, x, parts=3):
    acc = None
    r = x
    for _ in range(parts):
        hi = r.astype(BF16)
        d = jnp.dot(ones_bf16, hi, preferred_element_type=F32)
        acc = d if acc is None else acc + d
        r = r - hi.astype(F32)
    return acc


def _mlstm_state_kernel(kf_ref, vf_ref, gf_ref, kb_ref, vb_ref, gb_ref, bias_ref,
                        cf_ref, nmf_ref, cb_ref, nmb_ref, c_sc, nm_sc, *, n_heads):
    c = pl.program_id(1)

    @pl.when(c == 0)
    def _():
        c_sc[...] = jnp.zeros(c_sc.shape, F32)
        nm_sc[...] = jnp.zeros(nm_sc.shape, F32)

    w = kf_ref.shape[2]
    hh = n_heads
    tril, triu = _tri_masks()
    lane_t = lax.broadcasted_iota(jnp.int32, (CHUNK, w), 1)
    lane_1 = lax.broadcasted_iota(jnp.int32, (1, w), 1)
    bd = _block_diag_mask(w)
    dirs = ((kf_ref, vf_ref, gf_ref, cf_ref, nmf_ref, tril), (kb_ref, vb_ref, gb_ref, cb_ref, nmb_ref, triu))
    for d, (k_ref, v_ref, g_ref, c_out, nm_out, tmat) in enumerate(dirs):
        pre = g_ref[0] + bias_ref[...]
        ig = pre[:, d * hh:(d + 1) * hh]
        lf_all = jax.nn.log_sigmoid(pre)
        b_all = _ones_dot(tmat.astype(BF16), lf_all)
        lf = lf_all[:, (2 + d) * hh:(3 + d) * hh]
        bcum = b_all[:, (2 + d) * hh:(3 + d) * hh]
        b_end = jnp.sum(lf, axis=0, keepdims=True)
        a = b_end - bcum + ig
        m_loc = jnp.max(a, axis=0, keepdims=True)
        wgt = jnp.exp(a - m_loc)
        kw = k_ref[0] * _expand_heads(wgt, hh, lane_t)
        c_inc = jnp.dot(kw.T.astype(BF16), v_ref[0].astype(BF16), preferred_element_type=F32)
        n_inc = jnp.sum(kw, axis=0, keepdims=True)
        be = _expand_heads(b_end, hh, lane_1)
        ml = _expand_heads(m_loc, hh, lane_1)
        c_old = c_sc[d]
        n_old = nm_sc[d, 0:1, :]
        m_old = nm_sc[d, 1:2, :]
        c_out[0, 0] = c_old
        nm_out[0, 0] = nm_sc[d]
        m_new = jnp.maximum(be + m_old, ml)
        w_prev = jnp.exp(be + m_old - m_new)
        w_inc = jnp.exp(ml - m_new)
        c_sc[d] = w_prev * c_old + w_inc * jnp.where(bd, c_inc, 0.0)
        nm_sc[d, 0:1, :] = w_prev * n_old + w_inc * n_inc
        nm_sc[d, 1:2, :] = m_new


def mlstm_states(qk, m, gcol, bias_row, *, n_ctx, n_heads):
    b, s, w2 = qk.shape
    w = w2 // 2
    ng = gcol.shape[2]
    nc, nc_ctx = s // CHUNK, n_ctx // CHUNK
    rev = lambda c: _rev_chunk(c, nc_ctx, nc)
    cst = jax.ShapeDtypeStruct((b, nc, w, w), F32)
    nmst = jax.ShapeDtypeStruct((b, nc, 8, w), F32)
    fwd3 = lambda j: (lambda bi, c: (bi, c, j))
    bwd3 = lambda j: (lambda bi, c: (bi, rev(c), j))
    return pl.pallas_call(
        functools.partial(_mlstm_state_kernel, n_heads=n_heads),
        grid=(b, nc),
        in_specs=[pl.BlockSpec((1, CHUNK, w), fwd3(1)),
                  pl.BlockSpec((1, CHUNK, w), fwd3(2)),
                  pl.BlockSpec((1, CHUNK, ng), fwd3(0)),
                  pl.BlockSpec((1, CHUNK, w), bwd3(1)),
                  pl.BlockSpec((1, CHUNK, w), bwd3(2)),
                  pl.BlockSpec((1, CHUNK, ng), bwd3(0)),
                  pl.BlockSpec((1, ng), lambda bi, c: (0, 0))],
        out_specs=(pl.BlockSpec((1, 1, w, w), lambda bi, c: (bi, c, 0, 0)),
                   pl.BlockSpec((1, 1, 8, w), lambda bi, c: (bi, c, 0, 0)),
                   pl.BlockSpec((1, 1, w, w), lambda bi, c: (bi, rev(c), 0, 0)),
                   pl.BlockSpec((1, 1, 8, w), lambda bi, c: (bi, rev(c), 0, 0))),
        out_shape=(cst, nmst, cst, nmst),
        scratch_shapes=[pltpu.VMEM((2, w, w), F32), pltpu.VMEM((2, 8, w), F32)],
        compiler_params=_cparams("parallel", "arbitrary"),
        name="mlstm_states",
    )(qk, m, gcol, qk, m, gcol, bias_row)


def _mlstm_out_kernel(qk_ref, m_ref, gc_ref, gr_ref, bc_ref, br_ref, cf_ref, nmf_ref, cb_ref, nmb_ref,
                      nw_ref, gmat_ref, o_ref, *, n_heads):
    w = o_ref.shape[2]
    hh = n_heads
    qk = qk_ref[0]
    q, k = qk[:, 0:w], qk[:, w:2 * w]
    mm = m_ref[0]
    v, og = mm[:, 2 * w:3 * w], mm[:, 3 * w:4 * w]
    pre_c = gc_ref[0] + bc_ref[...]
    pre_r = gr_ref[0] + br_ref[...]
    lf_c = jax.nn.log_sigmoid(pre_c)
    lf_r = jax.nn.log_sigmoid(pre_r)
    tril, triu = _tri_masks()
    lane = lax.broadcasted_iota(jnp.int32, (CHUNK, w), 1)
    kb, vb = k.astype(BF16), v.astype(BF16)
    gmat = gmat_ref[...]
    total = jnp.zeros((CHUNK, w), F32)
    for d, (c_ref, nm_ref) in enumerate(((cf_ref, nmf_ref), (cb_ref, nmb_ref))):
        tmat = tril if d == 0 else triu
        tmat_t = triu if d == 0 else tril
        b_c = _ones_dot(tmat.astype(BF16), lf_c)
        b_r = _split_dot(lf_r, tmat_t.astype(BF16), parts=3)
        nm = nm_ref[0, 0]
        n_row, m_row = nm[0:1, :], nm[1:2, :]
        cross = jnp.dot(q.astype(BF16), c_ref[0, 0].astype(BF16), preferred_element_type=F32)
        qn = _split_dot(q * n_row, gmat)
        num = jnp.zeros((CHUNK, w), F32)
        den_e = jnp.zeros((CHUNK, w), F32)
        wp_e = jnp.zeros((CHUNK, w), F32)
        mt_e = jnp.zeros((CHUNK, w), F32)
        for h in range(hh):
            fc, ic = (2 + d) * hh + h, d * hh + h
            bcol = b_c[:, fc:fc + 1]
            logd = jnp.where(tmat, bcol - b_r[fc:fc + 1, :] + pre_r[ic:ic + 1, :], -jnp.inf)
            gg = bcol + m_row[:, h * HEAD_DIM:h * HEAD_DIM + 1]
            m_t = jnp.maximum(gg, jnp.max(logd, axis=-1, keepdims=True))
            hm = (lane >= h * HEAD_DIM) & (lane < (h + 1) * HEAD_DIM)
            sc = _dot_nt(jnp.where(hm, q, 0.0).astype(BF16), kb)
            p = sc * jnp.exp(logd - m_t)
            pv = jnp.dot(p.astype(BF16), vb, preferred_element_type=F32)
            num = num + jnp.where(hm, pv, 0.0)
            den_e = jnp.where(hm, jnp.sum(p, axis=-1, keepdims=True), den_e)
            wp_e = jnp.where(hm, jnp.exp(gg - m_t), wp_e)
            mt_e = jnp.where(hm, m_t, mt_e)
        num = num + wp_e * cross
        den = den_e + wp_e * qn
        total = total + num / jnp.maximum(jnp.abs(den), jnp.exp(-mt_e))
    o_ref[0] = _group_layer_norm(total * jax.nn.sigmoid(og), gmat, nw_ref[...])


def mlstm_out(qk, m, gcol, grow, bias_row, bias_col, cf, nmf, cb, nmb, norm_w, gmat, *, skip_chunks, n_heads):
    b, s, w2 = qk.shape
    w = w2 // 2
    ng = gcol.shape[2]
    nc = s // CHUNK - skip_chunks
    sk = skip_chunks
    return pl.pallas_call(
        functools.partial(_mlstm_out_kernel, n_heads=n_heads),
        grid=(b, nc),
        in_specs=[pl.BlockSpec((1, CHUNK, w2), lambda bi, c: (bi, c + sk, 0)),
                  pl.BlockSpec((1, CHUNK, 4 * w), lambda bi, c: (bi, c + sk, 0)),
                  pl.BlockSpec((1, CHUNK, ng), lambda bi, c: (bi, c + sk, 0)),
                  pl.BlockSpec((1, ng, CHUNK), lambda bi, c: (bi, 0, c + sk)),
                  pl.BlockSpec((1, ng), lambda bi, c: (0, 0)),
                  pl.BlockSpec((ng, 1), lambda bi, c: (0, 0)),
                  pl.BlockSpec((1, 1, w, w), lambda bi, c: (bi, c + sk, 0, 0)),
                  pl.BlockSpec((1, 1, 8, w), lambda bi, c: (bi, c + sk, 0, 0)),
                  pl.BlockSpec((1, 1, w, w), lambda bi, c: (bi, c + sk, 0, 0)),
                  pl.BlockSpec((1, 1, 8, w), lambda bi, c: (bi, c + sk, 0, 0)),
                  pl.BlockSpec((1, w), lambda bi, c: (0, 0)),
                  pl.BlockSpec((w, w), lambda bi, c: (0, 0))],
        out_specs=pl.BlockSpec((1, CHUNK, w), lambda bi, c: (bi, c, 0)),
        out_shape=jax.ShapeDtypeStruct((b, nc * CHUNK, w), F32),
        compiler_params=_cparams("parallel", "parallel"),
        name="mlstm_out",
    )(qk, m, gcol, grow, bias_row, bias_col, cf, nmf, cb, nmb, norm_w, gmat)


def _proj_out_kernel(att_ref, ret_ref, ml_ref, h_ref, mod_ref, w_ref, o_ref):
    aw, rw = att_ref.shape[2], ret_ref.shape[2]
    y = jnp.dot(att_ref[0].astype(BF16), w_ref[0:aw, :], preferred_element_type=F32)
    y = y + jnp.dot(ret_ref[0].astype(BF16), w_ref[aw:aw + rw, :], preferred_element_type=F32)
    y = y + jnp.dot(ml_ref[0].astype(BF16), w_ref[aw + rw:, :], preferred_element_type=F32)
    o_ref[0] = h_ref[0] + mod_ref[0, 2:3, :] * y


def _mod_map(n_ctx_tiles, ctx_row, skip):
    return lambda bi, i: (jnp.where(i + skip < n_ctx_tiles, ctx_row, bi), 0, 0)


def proj_out(att, ret, ml, h, mod, w_out, *, n_ctx, skip):
    b, s, d = h.shape
    tm = TOKEN_TILE
    nt = s // tm - skip
    tok = lambda bi, i: (bi, i, 0)
    return pl.pallas_call(
        _proj_out_kernel,
        grid=(b, nt),
        in_specs=[pl.BlockSpec((1, tm, att.shape[2]), tok),
                  pl.BlockSpec((1, tm, ret.shape[2]), tok),
                  pl.BlockSpec((1, tm, ml.shape[2]), tok),
                  pl.BlockSpec((1, tm, d), lambda bi, i: (bi, i + skip, 0)),
                  pl.BlockSpec((1, 6, d), _mod_map(n_ctx // tm, mod.shape[0] - 1, skip)),
                  pl.BlockSpec(w_out.shape, lambda bi, i: (0, 0))],
        out_specs=pl.BlockSpec((1, tm, d), tok),
        out_shape=jax.ShapeDtypeStruct((b, nt * tm, d), F32),
        compiler_params=_cparams("parallel", "parallel"),
        name="proj_out",
    )(att, ret, ml, h, mod, w_out)


def _ffn_kernel(h_ref, mod_ref, nw_ref, wg_ref, wu_ref, wd_ref, o_ref):
    x = h_ref[0]
    a = _norm_mod(x, nw_ref[...], mod_ref[0, 3:4, :], mod_ref[0, 4:5, :]).astype(BF16)
    hg = jnp.dot(a, wg_ref[...], preferred_element_type=F32)
    hu = jnp.dot(a, wu_ref[...], preferred_element_type=F32)
    act = (_silu(hg) * hu).astype(BF16)
    y = jnp.dot(act, wd_ref[...], preferred_element_type=F32)
    o_ref[0] = x + mod_ref[0, 5:6, :] * y


def ffn_dense(h, mod, norm_w, wg, wu, wd, *, n_ctx):
    b, s, d = h.shape
    tm = TOKEN_TILE
    tok = lambda bi, i: (bi, i, 0)
    const = lambda bi, i: (0, 0)
    resident = functools.partial(pl.BlockSpec, pipeline_mode=pl.Buffered(1))
    return pl.pallas_call(
        _ffn_kernel,
        grid=(b, s // tm),
        in_specs=[pl.BlockSpec((1, tm, d), tok),
                  pl.BlockSpec((1, 6, d), _mod_map(n_ctx // tm, mod.shape[0] - 1, 0)),
                  pl.BlockSpec((1, d), const),
                  resident(wg.shape, const),
                  resident(wu.shape, const),
                  resident(wd.shape, const)],
        out_specs=pl.BlockSpec((1, tm, d), tok),
        out_shape=jax.ShapeDtypeStruct((b, s, d), F32),
        compiler_params=_cparams("parallel", "parallel"),
        name="ffn_dense",
    )(h, mod, norm_w, wg, wu, wd)


def _moe_pre_kernel(h_ref, mod_ref, nw_ref, rw_ref, rb_ref, xn_ref, route_ref, *, n_experts):
    x = h_ref[0]
    a = _norm_mod(x, nw_ref[...], mod_ref[0, 3:4, :], mod_ref[0, 4:5, :])
    a_hi = a.astype(BF16)
    xn_ref[0] = a_hi
    a_lo = (a - a_hi.astype(F32)).astype(BF16)
    rw = rw_ref[...]
    w_hi = rw.astype(BF16)
    w_lo = (rw - w_hi.astype(F32)).astype(BF16)
    logits = (jnp.dot(a_hi, w_hi, preferred_element_type=F32) + jnp.dot(a_lo, w_hi, preferred_element_type=F32)
              + jnp.dot(a_hi, w_lo, preferred_element_type=F32)) + rb_ref[...]
    lane = lax.broadcasted_iota(jnp.int32, logits.shape, 1)
    l1 = jnp.where(lane < n_experts, logits, -jnp.inf)
    v1 = jnp.max(l1, axis=-1, keepdims=True)
    i1 = jnp.min(jnp.where(l1 == v1, lane, 128), axis=-1, keepdims=True)
    l2 = jnp.where(lane == i1, -jnp.inf, l1)
    v2 = jnp.max(l2, axis=-1, keepdims=True)
    i2 = jnp.min(jnp.where(l2 == v2, lane, 128), axis=-1, keepdims=True)
    e2 = jnp.exp(v2 - v1)
    w1 = 1.0 / (1.0 + e2)
    w2 = e2 * w1
    lane8 = lax.broadcasted_iota(jnp.int32, route_ref.shape[1:], 1)
    route = jnp.where(lane8 == 0, i1.astype(F32),
                      jnp.where(lane8 == 1, i2.astype(F32),
                                jnp.where(lane8 == 2, w1, jnp.where(lane8 == 3, w2, 0.0))))
    route_ref[0] = route


def moe_pre(h, mod, norm_w, router_w_pad, router_b_pad, *, n_experts):
    b, t, d = h.shape
    tm = TOKEN_TILE
    tok = lambda bi, i: (bi, i, 0)
    const = lambda bi, i: (0, 0)
    return pl.pallas_call(
        functools.partial(_moe_pre_kernel, n_experts=n_experts),
        grid=(b, t // tm),
        in_specs=[pl.BlockSpec((1, tm, d), tok),
                  pl.BlockSpec((1, 6, d), lambda bi, i: (bi, 0, 0)),
                  pl.BlockSpec((1, d), const),
                  pl.BlockSpec(router_w_pad.shape, const),
                  pl.BlockSpec((1, 128), const)],
        out_specs=(pl.BlockSpec((1, tm, d), tok), pl.BlockSpec((1, tm, 8), tok)),
        out_shape=(jax.ShapeDtypeStruct((b, t, d), BF16), jax.ShapeDtypeStruct((b, t, 8), F32)),
        compiler_params=_cparams("parallel", "parallel"),
        name="moe_pre",
    )(h, mod, norm_w, router_w_pad, router_b_pad)


def _moe_gmm_kernel(be_ref, nu_ref, x_ref, wg_ref, wu_ref, wd_ref, o_ref, acc_sc):
    i, j = pl.program_id(0), pl.program_id(1)
    nj = pl.num_programs(1)
    used = i < nu_ref[0]

    @pl.when(used)
    def _():
        x = x_ref[...]
        hg = jnp.dot(x, wg_ref[0], preferred_element_type=F32)
        hu = jnp.dot(x, wu_ref[0], preferred_element_type=F32)
        y = jnp.dot((_silu(hg) * hu).astype(BF16), wd_ref[0], preferred_element_type=F32)

        @pl.when(j == 0)
        def _():
            acc_sc[...] = y

        @pl.when(j > 0)
        def _():
            acc_sc[...] += y

    @pl.when(j == nj - 1)
    def _():
        o_ref[...] = jnp.where(used, acc_sc[...], 0.0)


def moe_gmm(block_e, n_used, xb, wg, wu, wd):
    n_slots, d = xb.shape
    tm, tf = MOE_TILE, MOE_FF_TILE
    n_blocks = n_slots // tm
    f = wg.shape[2]
    nj = f // tf
    last = nj - 1

    def jj(i, j, nu):
        return jnp.where(i < nu[0], j, last)

    grid_spec = pltpu.PrefetchScalarGridSpec(
        num_scalar_prefetch=2,
        grid=(n_blocks, nj),
        in_specs=[pl.BlockSpec((tm, d), lambda i, j, be, nu: (i, 0)),
                  pl.BlockSpec((1, d, tf), lambda i, j, be, nu: (be[i], 0, jj(i, j, nu))),
                  pl.BlockSpec((1, d, tf), lambda i, j, be, nu: (be[i], 0, jj(i, j, nu))),
                  pl.BlockSpec((1, tf, d), lambda i, j, be, nu: (be[i], jj(i, j, nu), 0))],
        out_specs=pl.BlockSpec((tm, d), lambda i, j, be, nu: (i, 0)),
        scratch_shapes=[pltpu.VMEM((tm, d), F32)],
    )
    return pl.pallas_call(
        _moe_gmm_kernel,
        grid_spec=grid_spec,
        out_shape=jax.ShapeDtypeStruct((n_slots, d), F32),
        compiler_params=_cparams("arbitrary", "arbitrary"),
        name="moe_gmm",
    )(block_e, n_used, xb, wg, wu, wd)


def _moe_combine_kernel(h_ref, mod_ref, route_ref, y1_ref, y2_ref, o_ref):
    r = route_ref[0]
    y = r[:, 2:3] * y1_ref[0] + r[:, 3:4] * y2_ref[0]
    o_ref[0] = h_ref[0] + mod_ref[0, 5:6, :] * y


def moe_combine(h, mod, route, y1, y2):
    b, t, d = h.shape
    tm = TOKEN_TILE
    tok = lambda bi, i: (bi, i, 0)
    return pl.pallas_call(
        _moe_combine_kernel,
        grid=(b, t // tm),
        in_specs=[pl.BlockSpec((1, tm, d), tok),
                  pl.BlockSpec((1, 6, d), lambda bi, i: (bi, 0, 0)),
                  pl.BlockSpec((1, tm, 8), tok),
                  pl.BlockSpec((1, tm, d), tok),
                  pl.BlockSpec((1, tm, d), tok)],
        out_specs=pl.BlockSpec((1, tm, d), tok),
        out_shape=jax.ShapeDtypeStruct((b, t, d), F32),
        compiler_params=_cparams("parallel", "parallel"),
        name="moe_combine",
    )(h, mod, route, y1, y2)


def _moe_dispatch_plan(expert_idx, n_experts):
    n = expert_idx.shape[0]
    tile = MOE_TILE
    flat_e = expert_idx.reshape(-1)
    onehot = (flat_e[:, None] == jnp.arange(n_experts, dtype=jnp.int32)[None, :]).astype(jnp.int32)
    csum = jnp.cumsum(onehot, axis=0)
    counts = csum[-1]
    padded = (counts + tile - 1) // tile * tile
    pad_ends = jnp.cumsum(padded)
    pad_starts = pad_ends - padded
    slot = jnp.sum(onehot * (pad_starts[None, :] + csum - 1), axis=1)
    n_blocks = -(-(n * TOP_K) // tile) + n_experts
    n_used = pad_ends[-1] // tile
    blk = jnp.arange(n_blocks, dtype=jnp.int32)
    block_e = jnp.minimum(jnp.searchsorted(pad_ends, blk * tile, side="right"), n_experts - 1).astype(jnp.int32)
    block_e = jnp.where(blk < n_used, block_e, block_e[jnp.maximum(n_used - 1, 0)])
    slot_tok = jnp.zeros((n_blocks * tile,), jnp.int32).at[slot].set(jnp.arange(n * TOP_K, dtype=jnp.int32) // TOP_K)
    return slot.reshape(n, TOP_K), slot_tok, block_e, n_used.astype(jnp.int32).reshape(1)


def _rope_tables(n_ctx, n_lat):
    rows = n_lat // GRID_W
    row = jnp.repeat(jnp.arange(rows, dtype=F32), GRID_W)
    col = jnp.tile(jnp.arange(GRID_W, dtype=F32), rows)
    n_freq = HEAD_DIM // 4
    inv_freq = ROPE_THETA ** (-jnp.arange(n_freq, dtype=F32) / n_freq)
    ang = jnp.concatenate([row[:, None] * inv_freq, col[:, None] * inv_freq], -1)
    cos, sin = jnp.cos(ang), jnp.sin(ang)
    cos_t = jnp.concatenate([cos, cos, cos, cos], axis=1)
    sin_t = jnp.concatenate([-sin, sin, -sin, sin], axis=1)
    cos_t = jnp.concatenate([jnp.ones((n_ctx, 128), F32), cos_t], axis=0)
    sin_t = jnp.concatenate([jnp.zeros((n_ctx, 128), F32), sin_t], axis=0)
    return cos_t, sin_t


def _block_ones(n):
    idx = jnp.arange(n) // HEAD_DIM
    return (idx[:, None] == idx[None, :]).astype(BF16)


def kernel(x, c, ctx, c_ctx, mod_w, mod_b, norm1_w, norm2_w, w_in, attn_qn_w, attn_kn_w, ret_decay, ret_norm_w,
           mlstm_conv_w, mlstm_gate_b, mlstm_norm_w, w_out, ffn_w_gate, ffn_w_up, ffn_w_down, router_w, router_b,
           moe_w_gate, moe_w_up, moe_w_down):
    b, n_lat, d = x.shape
    n_ctx = ctx.shape[1]
    depth = mod_w.shape[0]
    n_experts = router_w.shape[2]
    ret_heads = ret_decay.shape[2]
    ml_heads = mlstm_gate_b.shape[2]
    ret_w, ml_w = ret_heads * HEAD_DIM, ml_heads * HEAD_DIM
    n_gate = N_GATES * ml_heads
    kv_w = (w_in.shape[2] - 4 * ret_w - 4 * ml_w - n_gate) // 6
    att_w = 4 * kv_w
    dims = (att_w, kv_w, ret_w, ml_w, n_gate)
    assert n_ctx % TOKEN_TILE == 0 and n_lat % KV_TILE == 0 and b + 1 <= 16
    n_main = w_in.shape[2] - n_gate
    n_pad = -(-w_in.shape[2] // 128) * 128

    cos_t, sin_t = _rope_tables(n_ctx, n_lat)
    gq, gk, g_ret, g_ml = _block_ones(att_w), _block_ones(kv_w), _block_ones(ret_w), _block_ones(ml_w)
    cond = jnp.concatenate([c, c_ctx[None, :], jnp.zeros((16 - b - 1, d), F32)], axis=0)
    ctx_tiles = n_ctx // TOKEN_TILE
    ctx_chunks = n_ctx // CHUNK

    h = jnp.concatenate([ctx, x], axis=1)
    for layer in range(depth):
        last = layer == depth - 1
        skip = ctx_tiles if last else 0
        mod = adaln(cond, mod_w[layer], mod_b[layer]).reshape(16, 6, d)[:b + 1]
        w_all = jnp.pad(w_in[layer].astype(BF16), ((0, 0), (0, n_pad - w_in.shape[2])))
        w_gate_t = w_in[layer][:, n_main:].T.astype(BF16)
        qn = jnp.tile(attn_qn_w[layer], att_w // HEAD_DIM)[None, :]
        kn = jnp.tile(attn_kn_w[layer], kv_w // HEAD_DIM)[None, :]
        q, k, v, r, m, gcol, grow = proj_in(h, mod, norm1_w[layer][None, :], cos_t, sin_t, w_all, w_gate_t, qn, kn,
                                            gq, gk, n_ctx=n_ctx, dims=dims)
        att = attention(q, k, v, n_ctx=n_ctx, skip=skip)
        dec = jnp.repeat(ret_decay[layer], HEAD_DIM, axis=1)
        sf, sb = ret_states(r, dec, n_ctx=n_ctx)
        ret = ret_out(r, sf, sb, dec, ret_norm_w[layer][None, :], g_ret, skip_chunks=ctx_chunks if last else 0)
        qk = mlstm_conv(m, mlstm_conv_w[layer], n_ctx=n_ctx)
        bias = mlstm_gate_b[layer].reshape(-1)
        cf, nmf, cb, nmb = mlstm_states(qk, m, gcol, bias[None, :], n_ctx=n_ctx, n_heads=ml_heads)
        ml = mlstm_out(qk, m, gcol, grow, bias[None, :], bias[:, None], cf, nmf, cb, nmb, mlstm_norm_w[layer][None, :],
                       g_ml, skip_chunks=ctx_chunks if last else 0, n_heads=ml_heads)
        h = proj_out(att, ret, ml, h, mod, w_out[layer].astype(BF16), n_ctx=n_ctx, skip=skip)
        if layer % 2 == 0:
            li = layer // 2
            assert not last, "dense FFN on a latent-only stream is not wired up"
            h = ffn_dense(h, mod, norm2_w[layer][None, :], ffn_w_gate[li].astype(BF16), ffn_w_up[li].astype(BF16),
                          ffn_w_down[li].astype(BF16), n_ctx=n_ctx)
        else:
            li = layer // 2
            assert last, "expert FFN on the combined stream is not wired up"
            mod_lat = mod[:b]
            rw = jnp.pad(router_w[li], ((0, 0), (0, 128 - n_experts)))
            rb = jnp.pad(router_b[li], (0, 128 - n_experts))[None, :]
            xn, route = moe_pre(h, mod_lat, norm2_w[layer][None, :], rw, rb, n_experts=n_experts)
            t = h.shape[1]
            route_flat = route.reshape(b * t, 8)
            slot, slot_tok, block_e, n_used = _moe_dispatch_plan(route_flat[:, 0:TOP_K].astype(jnp.int32), n_experts)
            xb = jnp.take(xn.reshape(b * t, d), slot_tok, axis=0)
            yb = moe_gmm(block_e, n_used, xb, moe_w_gate[li].astype(BF16), moe_w_up[li].astype(BF16),
                         moe_w_down[li].astype(BF16))
            y1 = jnp.take(yb, slot[:, 0], axis=0).reshape(b, t, d)
            y2 = jnp.take(yb, slot[:, 1], axis=0).reshape(b, t, d)
            h = moe_combine(h, mod_lat, route, y1, y2)
    return h if h.shape[1] == n_lat else h[:, n_ctx:, :]
```

```python
import functools
import math

import jax
import jax.numpy as jnp
from jax import lax
from jax.experimental import pallas as pl
from jax.experimental.pallas import tpu as pltpu

F32 = jnp.float32
BF16 = jnp.bfloat16

HEAD_DIM = 64
GRID_W = 64
ROPE_THETA = 10000.0
EPS = 1e-6
CHUNK = 128
TOKEN_TILE = 256
KV_TILE = 512
N_GATES = 4
TOP_K = 2
MOE_TILE = 512
MOE_FF_TILE = 512
VMEM_LIMIT = 56 * 1024 * 1024
QK_SCALE = HEAD_DIM ** -0.5


def _cparams(*sem):
    return pltpu.CompilerParams(dimension_semantics=sem, vmem_limit_bytes=VMEM_LIMIT)


def _silu(x):
    return x * jax.nn.sigmoid(x)


def _split_dot(x, ones_bf16, parts=2):
    acc = None
    r = x
    for _ in range(parts):
        hi = r.astype(BF16)
        d = jnp.dot(hi, ones_bf16, preferred_element_type=F32)
        acc = d if acc is None else acc + d
        r = r - hi.astype(F32)
    return acc


def _dot_nt(a, b):
    return lax.dot_general(a, b, (((1,), (1,)), ((), ())), preferred_element_type=F32)


def _rope(x, cos_t, sin_t):
    w = x.shape[1]
    reps = w // 128
    c = jnp.concatenate([cos_t] * reps, axis=1) if reps > 1 else cos_t
    s = jnp.concatenate([sin_t] * reps, axis=1) if reps > 1 else sin_t
    lane = lax.broadcasted_iota(jnp.int32, x.shape, 1)
    first_half = (lane % HEAD_DIM) < (HEAD_DIM // 2)
    swapped = jnp.where(first_half, pltpu.roll(x, w - HEAD_DIM // 2, 1), pltpu.roll(x, HEAD_DIM // 2, 1))
    return x * c + swapped * s


def _group_rms(x, gmat, gain):
    ms = _split_dot(x * x, gmat) * (1.0 / HEAD_DIM)
    return x * lax.rsqrt(ms + EPS) * gain


def _group_layer_norm(x, gmat, gain):
    mu = _split_dot(x, gmat) * (1.0 / HEAD_DIM)
    xc = x - mu
    var = _split_dot(xc * xc, gmat) * (1.0 / HEAD_DIM)
    return xc * lax.rsqrt(var + EPS) * gain


def _expand_heads(x, n_heads, lane):
    out = x[:, n_heads - 1:n_heads]
    for h in range(n_heads - 2, -1, -1):
        out = jnp.where(lane < (h + 1) * HEAD_DIM, x[:, h:h + 1], out)
    return out


def _block_diag_mask(n):
    r = lax.broadcasted_iota(jnp.int32, (n, n), 0) // HEAD_DIM
    c = lax.broadcasted_iota(jnp.int32, (n, n), 1) // HEAD_DIM
    return r == c


def _adaln_kernel(c_ref, w_ref, b_ref, o_ref):
    a = _silu(c_ref[...]).astype(BF16)
    o_ref[...] = jnp.dot(a, w_ref[...].astype(BF16), preferred_element_type=F32) + b_ref[...]


def adaln(cond, w, b):
    r, d = cond.shape
    n = w.shape[1]
    tn = 1536 if n % 1536 == 0 else n
    return pl.pallas_call(
        _adaln_kernel,
        grid=(n // tn,),
        in_specs=[pl.BlockSpec((r, d), lambda j: (0, 0)),
                  pl.BlockSpec((d, tn), lambda j: (0, j)),
                  pl.BlockSpec((1, tn), lambda j: (0, j))],
        out_specs=pl.BlockSpec((r, tn), lambda j: (0, j)),
        out_shape=jax.ShapeDtypeStruct((r, n), F32),
        compiler_params=_cparams("arbitrary"),
        name="adaln",
    )(cond, w, b.reshape(1, n))


def _norm_mod(x, nw, shift, scale):
    ms = jnp.mean(x * x, axis=-1, keepdims=True)
    y = x * lax.rsqrt(ms + EPS) * nw
    return y * (1.0 + scale) + shift


def _proj_in_kernel(h_ref, mod_ref, nw_ref, cos_ref, sin_ref, w_ref, wgt_ref, qn_ref, kn_ref, gq_ref, gk_ref,
                    q_ref, k_ref, v_ref, r_ref, m_ref, gc_ref, gr_ref, *, dims):
    att_w, kv_w, ret_w, ml_w, n_gate = dims
    x = h_ref[0]
    a = _norm_mod(x, nw_ref[...], mod_ref[0, 0:1, :], mod_ref[0, 1:2, :]).astype(BF16)
    p = jnp.dot(a, w_ref[...], preferred_element_type=F32)
    cos_t, sin_t = cos_ref[...], sin_ref[...]
    o = 0
    aq = p[:, o:o + att_w]; o += att_w
    ak = p[:, o:o + kv_w]; o += kv_w
    av = p[:, o:o + kv_w]; o += kv_w
    aq = _rope(_group_rms(aq, gq_ref[...], qn_ref[...]), cos_t, sin_t) * QK_SCALE
    ak = _rope(_group_rms(ak, gk_ref[...], kn_ref[...]), cos_t, sin_t)
    for hh in range(att_w // HEAD_DIM):
        q_ref[0, hh] = aq[:, hh * HEAD_DIM:(hh + 1) * HEAD_DIM].astype(BF16)
    tm = x.shape[0]
    lane = lax.broadcasted_iota(jnp.int32, (tm, 2 * HEAD_DIM), 1)
    for hh in range(kv_w // HEAD_DIM):
        k_ref[0, hh] = ak[:, hh * HEAD_DIM:(hh + 1) * HEAD_DIM].astype(BF16)
        vh = av[:, hh * HEAD_DIM:(hh + 1) * HEAD_DIM]
        vpad = jnp.concatenate([vh, jnp.zeros_like(vh)], axis=1)
        v_ref[0, hh] = jnp.where(lane == HEAD_DIM, 1.0, vpad).astype(BF16)
    rq = _rope(p[:, o:o + ret_w], cos_t, sin_t)
    rk = _rope(p[:, o + ret_w:o + 2 * ret_w] * QK_SCALE, cos_t, sin_t)
    r_ref[0, :, 0:ret_w] = rq
    r_ref[0, :, ret_w:2 * ret_w] = rk
    r_ref[0, :, 2 * ret_w:4 * ret_w] = p[:, o + 2 * ret_w:o + 4 * ret_w]
    o += 4 * ret_w
    m_ref[0] = p[:, o:o + 4 * ml_w]
    o += 4 * ml_w
    gc_ref[0] = p[:, o:o + n_gate]
    gr_ref[0] = _dot_nt(wgt_ref[...], a)


def proj_in(h, mod, norm_w, cos_t, sin_t, w_all, w_gate_t, qn, kn, gq, gk, *, n_ctx, dims):
    att_w, kv_w, ret_w, ml_w, n_gate = dims
    b, s, d = h.shape
    tm = TOKEN_TILE
    n_ctx_tiles = n_ctx // tm
    n_pad = w_all.shape[1]
    ctx_row = mod.shape[0] - 1

    def mod_map(bi, i):
        return (jnp.where(i < n_ctx_tiles, ctx_row, bi), 0, 0)

    const2 = lambda bi, i: (0, 0)
    out_shape = (
        jax.ShapeDtypeStruct((b, att_w // HEAD_DIM, s, HEAD_DIM), BF16),
        jax.ShapeDtypeStruct((b, kv_w // HEAD_DIM, s, HEAD_DIM), BF16),
        jax.ShapeDtypeStruct((b, kv_w // HEAD_DIM, s, 2 * HEAD_DIM), BF16),
        jax.ShapeDtypeStruct((b, s, 4 * ret_w), F32),
        jax.ShapeDtypeStruct((b, s, 4 * ml_w), F32),
        jax.ShapeDtypeStruct((b, s, n_gate), F32),
        jax.ShapeDtypeStruct((b, n_gate, s), F32),
    )
    return pl.pallas_call(
        functools.partial(_proj_in_kernel, dims=dims),
        grid=(b, s // tm),
        in_specs=[
            pl.BlockSpec((1, tm, d), lambda bi, i: (bi, i, 0)),
            pl.BlockSpec((1, 6, d), mod_map),
            pl.BlockSpec((1, d), const2),
            pl.BlockSpec((tm, 128), lambda bi, i: (i, 0)),
            pl.BlockSpec((tm, 128), lambda bi, i: (i, 0)),
            pl.BlockSpec((d, n_pad), const2),
            pl.BlockSpec((n_gate, d), const2),
            pl.BlockSpec((1, att_w), const2),
            pl.BlockSpec((1, kv_w), const2),
            pl.BlockSpec((att_w, att_w), const2),
            pl.BlockSpec((kv_w, kv_w), const2),
        ],
        out_specs=(
            pl.BlockSpec((1, att_w // HEAD_DIM, tm, HEAD_DIM), lambda bi, i: (bi, 0, i, 0)),
            pl.BlockSpec((1, kv_w // HEAD_DIM, tm, HEAD_DIM), lambda bi, i: (bi, 0, i, 0)),
            pl.BlockSpec((1, kv_w // HEAD_DIM, tm, 2 * HEAD_DIM), lambda bi, i: (bi, 0, i, 0)),
            pl.BlockSpec((1, tm, 4 * ret_w), lambda bi, i: (bi, i, 0)),
            pl.BlockSpec((1, tm, 4 * ml_w), lambda bi, i: (bi, i, 0)),
            pl.BlockSpec((1, tm, n_gate), lambda bi, i: (bi, i, 0)),
            pl.BlockSpec((1, n_gate, tm), lambda bi, i: (bi, 0, i)),
        ),
        out_shape=out_shape,
        compiler_params=_cparams("parallel", "parallel"),
        name="proj_in",
    )(h, mod, norm_w, cos_t, sin_t, w_all, w_gate_t, qn, kn, gq, gk)


def _conv_kernel(x_ref, prev_ref, next_ref, w_ref, o_ref, *, n_ctx, seq, half):
    i = pl.program_id(1)
    x = x_ref[0]
    tm, w = x.shape
    row = lax.broadcasted_iota(jnp.int32, (tm, 1), 0)
    grow = row + i * tm
    xp = jnp.where(row == 0, prev_ref[0, 7:8, :], pltpu.roll(x, 1, 0))
    xn = jnp.where(row == tm - 1, next_ref[0, 0:1, :], pltpu.roll(x, tm - 1, 0))
    xp = jnp.where((grow == 0) | (grow == n_ctx), 0.0, xp)
    xn = jnp.where((grow == n_ctx - 1) | (grow == seq - 1), 0.0, xn)
    y = _silu(xp * w_ref[0:1, :] + x * w_ref[1:2, :] + xn * w_ref[2:3, :])
    lane = lax.broadcasted_iota(jnp.int32, (tm, w), 1)
    o_ref[0] = jnp.where(lane >= half, y * QK_SCALE, y)


def mlstm_conv(m, conv_w, *, n_ctx):
    b, s, _ = m.shape
    w = conv_w.shape[1]
    tm = TOKEN_TILE
    r8 = tm // 8
    last8 = s // 8 - 1
    return pl.pallas_call(
        functools.partial(_conv_kernel, n_ctx=n_ctx, seq=s, half=w // 2),
        grid=(b, s // tm),
        in_specs=[pl.BlockSpec((1, tm, w), lambda bi, i: (bi, i, 0)),
                  pl.BlockSpec((1, 8, w), lambda bi, i: (bi, jnp.maximum(i * r8 - 1, 0), 0)),
                  pl.BlockSpec((1, 8, w), lambda bi, i: (bi, jnp.minimum((i + 1) * r8, last8), 0)),
                  pl.BlockSpec((3, w), lambda bi, i: (0, 0))],
        out_specs=pl.BlockSpec((1, tm, w), lambda bi, i: (bi, i, 0)),
        out_shape=jax.ShapeDtypeStruct((b, s, w), F32),
        compiler_params=_cparams("parallel", "parallel"),
        name="mlstm_conv",
    )(m, m, m, conv_w)


def _attn_kernel(q_ref, k_ref, v_ref, o_ref, m_sc, acc_sc, *, n_ctx, n_lat_tiles, skip):
    i = pl.program_id(2) + skip
    g, tq, hd = q_ref.shape[1], q_ref.shape[2], q_ref.shape[3]

    def step(k, v):
        for h in range(g):
            rows = slice(h * tq, (h + 1) * tq)
            s = _dot_nt(q_ref[0, h], k)
            m_old = m_sc[rows]
            m_new = jnp.maximum(m_old, jnp.max(s, axis=-1, keepdims=True))
            p = jnp.exp(s - jnp.concatenate([m_new] * (s.shape[1] // 128), axis=1))
            acc_sc[rows] = jnp.exp(m_old - m_new) * acc_sc[rows] + jnp.dot(p.astype(BF16), v, preferred_element_type=F32)
            m_sc[rows] = m_new

    m_sc[...] = jnp.full(m_sc.shape, -jnp.inf, F32)
    acc_sc[...] = jnp.zeros(acc_sc.shape, F32)
    step(k_ref[0, 0, 0:n_ctx, :], v_ref[0, 0, 0:n_ctx, :])

    @pl.when(i * tq >= n_ctx)
    def _():
        def body(j, carry):
            start = pl.multiple_of(n_ctx + j * KV_TILE, KV_TILE if n_ctx % KV_TILE == 0 else math.gcd(n_ctx, KV_TILE))
            step(k_ref[0, 0, pl.ds(start, KV_TILE), :], v_ref[0, 0, pl.ds(start, KV_TILE), :])
            return carry
        lax.fori_loop(0, n_lat_tiles, body, 0)

    acc = acc_sc[...]
    out = acc[:, 0:hd] / acc[:, hd:hd + 1]
    o_ref[0] = jnp.concatenate([out[h * tq:(h + 1) * tq] for h in range(g)], axis=1)


def attention(q, k, v, *, n_ctx, skip):
    b, hq, s, hd = q.shape
    hkv = k.shape[1]
    g = hq // hkv
    tq = TOKEN_TILE
    n_lat = s - n_ctx
    assert n_lat % KV_TILE == 0 and n_ctx % tq == 0
    nq = s // tq - skip
    return pl.pallas_call(
        functools.partial(_attn_kernel, n_ctx=n_ctx, n_lat_tiles=n_lat // KV_TILE, skip=skip),
        grid=(b, hkv, nq),
        in_specs=[pl.BlockSpec((1, g, tq, hd), lambda bi, gi, i: (bi, gi, i + skip, 0)),
                  pl.BlockSpec((1, 1, s, hd), lambda bi, gi, i: (bi, gi, 0, 0)),
                  pl.BlockSpec((1, 1, s, 2 * hd), lambda bi, gi, i: (bi, gi, 0, 0))],
        out_specs=pl.BlockSpec((1, tq, g * hd), lambda bi, gi, i: (bi, i, gi)),
        out_shape=jax.ShapeDtypeStruct((b, nq * tq, hq * hd), F32),
        scratch_shapes=[pltpu.VMEM((g * tq, 128), F32), pltpu.VMEM((g * tq, 2 * hd), F32)],
        compiler_params=_cparams("parallel", "parallel", "arbitrary"),
        name="attention",
    )(q, k, v)


def _rev_chunk(c, nc_ctx, nc):
    return jnp.where(c < nc_ctx, nc_ctx - 1 - c, nc - 1 + nc_ctx - c)


def _ret_state_kernel(kf_ref, vf_ref, kb_ref, vb_ref, dec_ref, sf_ref, sb_ref, s_sc):
    c = pl.program_id(1)

    @pl.when(c == 0)
    def _():
        s_sc[...] = jnp.zeros(s_sc.shape, F32)

    w = kf_ref.shape[2]
    lg = jnp.log1p(-jnp.exp(dec_ref[...]))
    pos = lax.broadcasted_iota(jnp.int32, (CHUNK, 1), 0).astype(F32)
    bd = _block_diag_mask(w)
    for d, (k_ref, v_ref, out_ref) in enumerate(((kf_ref, vf_ref, sf_ref), (kb_ref, vb_ref, sb_ref))):
        lgd = lg[d:d + 1, :]
        e = (CHUNK - 1.0 - pos) if d == 0 else pos
        kdec = k_ref[0] * jnp.exp(e * lgd)
        inc = jnp.dot(kdec.T.astype(BF16), v_ref[0].astype(BF16), preferred_element_type=F32)
        s_old = s_sc[d]
        out_ref[0, 0] = s_old
        s_sc[d] = jnp.exp(CHUNK * lgd) * s_old + jnp.where(bd, inc, 0.0)


def ret_states(r, dec, *, n_ctx):
    b, s, w4 = r.shape
    w = w4 // 4
    nc, nc_ctx = s // CHUNK, n_ctx // CHUNK
    rev = lambda c: _rev_chunk(c, nc_ctx, nc)
    st = jax.ShapeDtypeStruct((b, nc, w, w), F32)
    return pl.pallas_call(
        _ret_state_kernel,
        grid=(b, nc),
        in_specs=[pl.BlockSpec((1, CHUNK, w), lambda bi, c: (bi, c, 1)),
                  pl.BlockSpec((1, CHUNK, w), lambda bi, c: (bi, c, 2)),
                  pl.BlockSpec((1, CHUNK, w), lambda bi, c: (bi, rev(c), 1)),
                  pl.BlockSpec((1, CHUNK, w), lambda bi, c: (bi, rev(c), 2)),
                  pl.BlockSpec((2, w), lambda bi, c: (0, 0))],
        out_specs=(pl.BlockSpec((1, 1, w, w), lambda bi, c: (bi, c, 0, 0)),
                   pl.BlockSpec((1, 1, w, w), lambda bi, c: (bi, rev(c), 0, 0))),
        out_shape=(st, st),
        scratch_shapes=[pltpu.VMEM((2, w, w), F32)],
        compiler_params=_cparams("parallel", "arbitrary"),
        name="ret_states",
    )(r, r, r, r, dec)


def _ret_out_kernel(r_ref, sf_ref, sb_ref, dec_ref, nw_ref, gmat_ref, o_ref):
    w = o_ref.shape[2]
    x = r_ref[0]
    q, k, v, g = x[:, 0:w], x[:, w:2 * w], x[:, 2 * w:3 * w], x[:, 3 * w:4 * w]
    lg = jnp.log1p(-jnp.exp(dec_ref[...]))
    pos = lax.broadcasted_iota(jnp.int32, (CHUNK, 1), 0).astype(F32)
    row = lax.broadcasted_iota(jnp.int32, (CHUNK, CHUNK), 0)
    col = lax.broadcasted_iota(jnp.int32, (CHUNK, CHUNK), 1)
    lane = lax.broadcasted_iota(jnp.int32, (CHUNK, w), 1)
    kb, vb = k.astype(BF16), v.astype(BF16)
    total = jnp.zeros((CHUNK, w), F32)
    for d, s_ref in enumerate((sf_ref, sb_ref)):
        lgd = lg[d:d + 1, :]
        qe = (pos + 1.0) if d == 0 else (CHUNK - pos)
        qdec = q * jnp.exp(qe * lgd)
        acc = jnp.dot(qdec.astype(BF16), s_ref[0, 0].astype(BF16), preferred_element_type=F32)
        lag = ((row - col) if d == 0 else (col - row)).astype(F32)
        for h in range(w // HEAD_DIM):
            hm = (lane >= h * HEAD_DIM) & (lane < (h + 1) * HEAD_DIM)
            sc = _dot_nt(jnp.where(hm, q, 0.0).astype(BF16), kb)
            lgh = lgd[:, h * HEAD_DIM:h * HEAD_DIM + 1]
            decay = jnp.where(lag >= 0, jnp.exp(jnp.maximum(lag, 0.0) * lgh), 0.0)
            pv = jnp.dot((sc * decay).astype(BF16), vb, preferred_element_type=F32)
            acc = acc + jnp.where(hm, pv, 0.0)
        total = total + acc
    o_ref[0] = _group_layer_norm(total, gmat_ref[...], nw_ref[...]) * _silu(g)


def ret_out(r, sf, sb, dec, norm_w, gmat, *, skip_chunks):
    b, s, w4 = r.shape
    w = w4 // 4
    nc = s // CHUNK - skip_chunks
    return pl.pallas_call(
        _ret_out_kernel,
        grid=(b, nc),
        in_specs=[pl.BlockSpec((1, CHUNK, w4), lambda bi, c: (bi, c + skip_chunks, 0)),
                  pl.BlockSpec((1, 1, w, w), lambda bi, c: (bi, c + skip_chunks, 0, 0)),
                  pl.BlockSpec((1, 1, w, w), lambda bi, c: (bi, c + skip_chunks, 0, 0)),
                  pl.BlockSpec((2, w), lambda bi, c: (0, 0)),
                  pl.BlockSpec((1, w), lambda bi, c: (0, 0)),
                  pl.BlockSpec((w, w), lambda bi, c: (0, 0))],
        out_specs=pl.BlockSpec((1, CHUNK, w), lambda bi, c: (bi, c, 0)),
        out_shape=jax.ShapeDtypeStruct((b, nc * CHUNK, w), F32),
        compiler_params=_cparams("parallel", "parallel"),
        name="ret_out",
    )(r, sf, sb, dec, norm_w, gmat)


def _tri_masks():
    row = lax.broadcasted_iota(jnp.int32, (CHUNK, CHUNK), 0)
    col = lax.broadcasted_iota(jnp.int32, (CHUNK, CHUNK), 1)
    return col <= row, col >= row


def _ones_dot(ones_bf16, x, parts=3):
    acc = None
    r = x
    for _ in range(parts):
        hi = r.astype(BF16)
        d = jnp.dot(ones_bf16, hi, preferred_element_type=F32)
        acc = d if acc is None else acc + d
        r = r - hi.astype(F32)
    return acc


def _mlstm_state_kernel(kf_ref, vf_ref, gf_ref, kb_ref, vb_ref, gb_ref, bias_ref,
                        cf_ref, nmf_ref, cb_ref, nmb_ref, c_sc, nm_sc, *, n_heads):
    c = pl.program_id(1)

    @pl.when(c == 0)
    def _():
        c_sc[...] = jnp.zeros(c_sc.shape, F32)
        nm_sc[...] = jnp.zeros(nm_sc.shape, F32)

    w = kf_ref.shape[2]
    hh = n_heads
    tril, triu = _tri_masks()
    lane_t = lax.broadcasted_iota(jnp.int32, (CHUNK, w), 1)
    lane_1 = lax.broadcasted_iota(jnp.int32, (1, w), 1)
    bd = _block_diag_mask(w)
    dirs = ((kf_ref, vf_ref, gf_ref, cf_ref, nmf_ref, tril), (kb_ref, vb_ref, gb_ref, cb_ref, nmb_ref, triu))
    for d, (k_ref, v_ref, g_ref, c_out, nm_out, tmat) in enumerate(dirs):
        pre = g_ref[0] + bias_ref[...]
        ig = pre[:, d * hh:(d + 1) * hh]
        lf_all = jax.nn.log_sigmoid(pre)
        b_all = _ones_dot(tmat.astype(BF16), lf_all)
        lf = lf_all[:, (2 + d) * hh:(3 + d) * hh]
        bcum = b_all[:, (2 + d) * hh:(3 + d) * hh]
        b_end = jnp.sum(lf, axis=0, keepdims=True)
        a = b_end - bcum + ig
        m_loc = jnp.max(a, axis=0, keepdims=True)
        wgt = jnp.exp(a - m_loc)
        kw = k_ref[0] * _expand_heads(wgt, hh, lane_t)
        c_inc = jnp.dot(kw.T.astype(BF16), v_ref[0].astype(BF16), preferred_element_type=F32)
        n_inc = jnp.sum(kw, axis=0, keepdims=True)
        be = _expand_heads(b_end, hh, lane_1)
        ml = _expand_heads(m_loc, hh, lane_1)
        c_old = c_sc[d]
        n_old = nm_sc[d, 0:1, :]
        m_old = nm_sc[d, 1:2, :]
        c_out[0, 0] = c_old
        nm_out[0, 0] = nm_sc[d]
        m_new = jnp.maximum(be + m_old, ml)
        w_prev = jnp.exp(be + m_old - m_new)
        w_inc = jnp.exp(ml - m_new)
        c_sc[d] = w_prev * c_old + w_inc * jnp.where(bd, c_inc, 0.0)
        nm_sc[d, 0:1, :] = w_prev * n_old + w_inc * n_inc
        nm_sc[d, 1:2, :] = m_new


def mlstm_states(qk, m, gcol, bias_row, *, n_ctx, n_heads):
    b, s, w2 = qk.shape
    w = w2 // 2
    ng = gcol.shape[2]
    nc, nc_ctx = s // CHUNK, n_ctx // CHUNK
    rev = lambda c: _rev_chunk(c, nc_ctx, nc)
    cst = jax.ShapeDtypeStruct((b, nc, w, w), F32)
    nmst = jax.ShapeDtypeStruct((b, nc, 8, w), F32)
    fwd3 = lambda j: (lambda bi, c: (bi, c, j))
    bwd3 = lambda j: (lambda bi, c: (bi, rev(c), j))
    return pl.pallas_call(
        functools.partial(_mlstm_state_kernel, n_heads=n_heads),
        grid=(b, nc),
        in_specs=[pl.BlockSpec((1, CHUNK, w), fwd3(1)),
                  pl.BlockSpec((1, CHUNK, w), fwd3(2)),
                  pl.BlockSpec((1, CHUNK, ng), fwd3(0)),
                  pl.BlockSpec((1, CHUNK, w), bwd3(1)),
                  pl.BlockSpec((1, CHUNK, w), bwd3(2)),
                  pl.BlockSpec((1, CHUNK, ng), bwd3(0)),
                  pl.BlockSpec((1, ng), lambda bi, c: (0, 0))],
        out_specs=(pl.BlockSpec((1, 1, w, w), lambda bi, c: (bi, c, 0, 0)),
                   pl.BlockSpec((1, 1, 8, w), lambda bi, c: (bi, c, 0, 0)),
                   pl.BlockSpec((1, 1, w, w), lambda bi, c: (bi, rev(c), 0, 0)),
                   pl.BlockSpec((1, 1, 8, w), lambda bi, c: (bi, rev(c), 0, 0))),
        out_shape=(cst, nmst, cst, nmst),
        scratch_shapes=[pltpu.VMEM((2, w, w), F32), pltpu.VMEM((2, 8, w), F32)],
        compiler_params=_cparams("parallel", "arbitrary"),
        name="mlstm_states",
    )(qk, m, gcol, qk, m, gcol, bias_row)


def _mlstm_out_kernel(qk_ref, m_ref, gc_ref, gr_ref, bc_ref, br_ref, cf_ref, nmf_ref, cb_ref, nmb_ref,
                      nw_ref, gmat_ref, o_ref, *, n_heads):
    w = o_ref.shape[2]
    hh = n_heads
    qk = qk_ref[0]
    q, k = qk[:, 0:w], qk[:, w:2 * w]
    mm = m_ref[0]
    v, og = mm[:, 2 * w:3 * w], mm[:, 3 * w:4 * w]
    pre_c = gc_ref[0] + bc_ref[...]
    pre_r = gr_ref[0] + br_ref[...]
    lf_c = jax.nn.log_sigmoid(pre_c)
    lf_r = jax.nn.log_sigmoid(pre_r)
    tril, triu = _tri_masks()
    lane = lax.broadcasted_iota(jnp.int32, (CHUNK, w), 1)
    kb, vb = k.astype(BF16), v.astype(BF16)
    gmat = gmat_ref[...]
    total = jnp.zeros((CHUNK, w), F32)
    for d, (c_ref, nm_ref) in enumerate(((cf_ref, nmf_ref), (cb_ref, nmb_ref))):
        tmat = tril if d == 0 else triu
        tmat_t = triu if d == 0 else tril
        b_c = _ones_dot(tmat.astype(BF16), lf_c)
        b_r = _split_dot(lf_r, tmat_t.astype(BF16), parts=3)
        nm = nm_ref[0, 0]
        n_row, m_row = nm[0:1, :], nm[1:2, :]
        cross = jnp.dot(q.astype(BF16), c_ref[0, 0].astype(BF16), preferred_element_type=F32)
        qn = _split_dot(q * n_row, gmat)
        num = jnp.zeros((CHUNK, w), F32)
        den_e = jnp.zeros((CHUNK, w), F32)
        wp_e = jnp.zeros((CHUNK, w), F32)
        mt_e = jnp.zeros((CHUNK, w), F32)
        for h in range(hh):
            fc, ic = (2 + d) * hh + h, d * hh + h
            bcol = b_c[:, fc:fc + 1]
            logd = jnp.where(tmat, bcol - b_r[fc:fc + 1, :] + pre_r[ic:ic + 1, :], -jnp.inf)
            gg = bcol + m_row[:, h * HEAD_DIM:h * HEAD_DIM + 1]
            m_t = jnp.maximum(gg, jnp.max(logd, axis=-1, keepdims=True))
            hm = (lane >= h * HEAD_DIM) & (lane < (h + 1) * HEAD_DIM)
            sc = _dot_nt(jnp.where(hm, q, 0.0).astype(BF16), kb)
            p = sc * jnp.exp(logd - m_t)
            pv = jnp.dot(p.astype(BF16), vb, preferred_element_type=F32)
            num = num + jnp.where(hm, pv, 0.0)
            den_e = jnp.where(hm, jnp.sum(p, axis=-1, keepdims=True), den_e)
            wp_e = jnp.where(hm, jnp.exp(gg - m_t), wp_e)
            mt_e = jnp.where(hm, m_t, mt_e)
        num = num + wp_e * cross
        den = den_e + wp_e * qn
        total = total + num / jnp.maximum(jnp.abs(den), jnp.exp(-mt_e))
    o_ref[0] = _group_layer_norm(total * jax.nn.sigmoid(og), gmat, nw_ref[...])


def mlstm_out(qk, m, gcol, grow, bias_row, bias_col, cf, nmf, cb, nmb, norm_w, gmat, *, skip_chunks, n_heads):
    b, s, w2 = qk.shape
    w = w2 // 2
    ng = gcol.shape[2]
    nc = s // CHUNK - skip_chunks
    sk = skip_chunks
    return pl.pallas_call(
        functools.partial(_mlstm_out_kernel, n_heads=n_heads),
        grid=(b, nc),
        in_specs=[pl.BlockSpec((1, CHUNK, w2), lambda bi, c: (bi, c + sk, 0)),
                  pl.BlockSpec((1, CHUNK, 4 * w), lambda bi, c: (bi, c + sk, 0)),
                  pl.BlockSpec((1, CHUNK, ng), lambda bi, c: (bi, c + sk, 0)),
                  pl.BlockSpec((1, ng, CHUNK), lambda bi, c: (bi, 0, c + sk)),
                  pl.BlockSpec((1, ng), lambda bi, c: (0, 0)),
                  pl.BlockSpec((ng, 1), lambda bi, c: (0, 0)),
                  pl.BlockSpec((1, 1, w, w), lambda bi, c: (bi, c + sk, 0, 0)),
                  pl.BlockSpec((1, 1, 8, w), lambda bi, c: (bi, c + sk, 0, 0)),
                  pl.BlockSpec((1, 1, w, w), lambda bi, c: (bi, c + sk, 0, 0)),
                  pl.BlockSpec((1, 1, 8, w), lambda bi, c: (bi, c + sk, 0, 0)),
                  pl.BlockSpec((1, w), lambda bi, c: (0, 0)),
                  pl.BlockSpec((w, w), lambda bi, c: (0, 0))],
        out_specs=pl.BlockSpec((1, CHUNK, w), lambda bi, c: (bi, c, 0)),
        out_shape=jax.ShapeDtypeStruct((b, nc * CHUNK, w), F32),
        compiler_params=_cparams("parallel", "parallel"),
        name="mlstm_out",
    )(qk, m, gcol, grow, bias_row, bias_col, cf, nmf, cb, nmb, norm_w, gmat)


def _proj_out_kernel(att_ref, ret_ref, ml_ref, h_ref, mod_ref, w_ref, o_ref):
    aw, rw = att_ref.shape[2], ret_ref.shape[2]
    y = jnp.dot(att_ref[0].astype(BF16), w_ref[0:aw, :], preferred_element_type=F32)
    y = y + jnp.dot(ret_ref[0].astype(BF16), w_ref[aw:aw + rw, :], preferred_element_type=F32)
    y = y + jnp.dot(ml_ref[0].astype(BF16), w_ref[aw + rw:, :], preferred_element_type=F32)
    o_ref[0] = h_ref[0] + mod_ref[0, 2:3, :] * y


def _mod_map(n_ctx_tiles, ctx_row, skip):
    return lambda bi, i: (jnp.where(i + skip < n_ctx_tiles, ctx_row, bi), 0, 0)


def proj_out(att, ret, ml, h, mod, w_out, *, n_ctx, skip):
    b, s, d = h.shape
    tm = TOKEN_TILE
    nt = s // tm - skip
    tok = lambda bi, i: (bi, i, 0)
    return pl.pallas_call(
        _proj_out_kernel,
        grid=(b, nt),
        in_specs=[pl.BlockSpec((1, tm, att.shape[2]), tok),
                  pl.BlockSpec((1, tm, ret.shape[2]), tok),
                  pl.BlockSpec((1, tm, ml.shape[2]), tok),
                  pl.BlockSpec((1, tm, d), lambda bi, i: (bi, i + skip, 0)),
                  pl.BlockSpec((1, 6, d), _mod_map(n_ctx // tm, mod.shape[0] - 1, skip)),
                  pl.BlockSpec(w_out.shape, lambda bi, i: (0, 0))],
        out_specs=pl.BlockSpec((1, tm, d), tok),
        out_shape=jax.ShapeDtypeStruct((b, nt * tm, d), F32),
        compiler_params=_cparams("parallel", "parallel"),
        name="proj_out",
    )(att, ret, ml, h, mod, w_out)


def _ffn_kernel(h_ref, mod_ref, nw_ref, wg_ref, wu_ref, wd_ref, o_ref):
    x = h_ref[0]
    a = _norm_mod(x, nw_ref[...], mod_ref[0, 3:4, :], mod_ref[0, 4:5, :]).astype(BF16)
    hg = jnp.dot(a, wg_ref[...], preferred_element_type=F32)
    hu = jnp.dot(a, wu_ref[...], preferred_element_type=F32)
    act = (_silu(hg) * hu).astype(BF16)
    y = jnp.dot(act, wd_ref[...], preferred_element_type=F32)
    o_ref[0] = x + mod_ref[0, 5:6, :] * y


def ffn_dense(h, mod, norm_w, wg, wu, wd, *, n_ctx):
    b, s, d = h.shape
    tm = TOKEN_TILE
    tok = lambda bi, i: (bi, i, 0)
    const = lambda bi, i: (0, 0)
    resident = functools.partial(pl.BlockSpec, pipeline_mode=pl.Buffered(1))
    return pl.pallas_call(
        _ffn_kernel,
        grid=(b, s // tm),
        in_specs=[pl.BlockSpec((1, tm, d), tok),
                  pl.BlockSpec((1, 6, d), _mod_map(n_ctx // tm, mod.shape[0] - 1, 0)),
                  pl.BlockSpec((1, d), const),
                  resident(wg.shape, const),
                  resident(wu.shape, const),
                  resident(wd.shape, const)],
        out_specs=pl.BlockSpec((1, tm, d), tok),
        out_shape=jax.ShapeDtypeStruct((b, s, d), F32),
        compiler_params=_cparams("parallel", "parallel"),
        name="ffn_dense",
    )(h, mod, norm_w, wg, wu, wd)


def _moe_pre_kernel(h_ref, mod_ref, nw_ref, rw_ref, rb_ref, xn_ref, route_ref, *, n_experts):
    x = h_ref[0]
    a = _norm_mod(x, nw_ref[...], mod_ref[0, 3:4, :], mod_ref[0, 4:5, :])
    a_hi = a.astype(BF16)
    xn_ref[0] = a_hi
    a_lo = (a - a_hi.astype(F32)).astype(BF16)
    rw = rw_ref[...]
    w_hi = rw.astype(BF16)
    w_lo = (rw - w_hi.astype(F32)).astype(BF16)
    logits = (jnp.dot(a_hi, w_hi, preferred_element_type=F32) + jnp.dot(a_lo, w_hi, preferred_element_type=F32)
              + jnp.dot(a_hi, w_lo, preferred_element_type=F32)) + rb_ref[...]
    lane = lax.broadcasted_iota(jnp.int32, logits.shape, 1)
    l1 = jnp.where(lane < n_experts, logits, -jnp.inf)
    v1 = jnp.max(l1, axis=-1, keepdims=True)
    i1 = jnp.min(jnp.where(l1 == v1, lane, 128), axis=-1, keepdims=True)
    l2 = jnp.where(lane == i1, -jnp.inf, l1)
    v2 = jnp.max(l2, axis=-1, keepdims=True)
    i2 = jnp.min(jnp.where(l2 == v2, lane, 128), axis=-1, keepdims=True)
    e2 = jnp.exp(v2 - v1)
    w1 = 1.0 / (1.0 + e2)
    w2 = e2 * w1
    lane8 = lax.broadcasted_iota(jnp.int32, route_ref.shape[1:], 1)
    route = jnp.where(lane8 == 0, i1.astype(F32),
                      jnp.where(lane8 == 1, i2.astype(F32),
                                jnp.where(lane8 == 2, w1, jnp.where(lane8 == 3, w2, 0.0))))
    route_ref[0] = route


def moe_pre(h, mod, norm_w, router_w_pad, router_b_pad, *, n_experts):
    b, t, d = h.shape
    tm = TOKEN_TILE
    tok = lambda bi, i: (bi, i, 0)
    const = lambda bi, i: (0, 0)
    return pl.pallas_call(
        functools.partial(_moe_pre_kernel, n_experts=n_experts),
        grid=(b, t // tm),
        in_specs=[pl.BlockSpec((1, tm, d), tok),
                  pl.BlockSpec((1, 6, d), lambda bi, i: (bi, 0, 0)),
                  pl.BlockSpec((1, d), const),
                  pl.BlockSpec(router_w_pad.shape, const),
                  pl.BlockSpec((1, 128), const)],
        out_specs=(pl.BlockSpec((1, tm, d), tok), pl.BlockSpec((1, tm, 8), tok)),
        out_shape=(jax.ShapeDtypeStruct((b, t, d), BF16), jax.ShapeDtypeStruct((b, t, 8), F32)),
        compiler_params=_cparams("parallel", "parallel"),
        name="moe_pre",
    )(h, mod, norm_w, router_w_pad, router_b_pad)


def _moe_gmm_kernel(be_ref, nu_ref, x_ref, wg_ref, wu_ref, wd_ref, o_ref, acc_sc):
    i, j = pl.program_id(0), pl.program_id(1)
    nj = pl.num_programs(1)
    used = i < nu_ref[0]

    @pl.when(used)
    def _():
        x = x_ref[...]
        hg = jnp.dot(x, wg_ref[0], preferred_element_type=F32)
        hu = jnp.dot(x, wu_ref[0], preferred_element_type=F32)
        y = jnp.dot((_silu(hg) * hu).astype(BF16), wd_ref[0], preferred_element_type=F32)

        @pl.when(j == 0)
        def _():
            acc_sc[...] = y

        @pl.when(j > 0)
        def _():
            acc_sc[...] += y

    @pl.when(j == nj - 1)
    def _():
        o_ref[...] = jnp.where(used, acc_sc[...], 0.0)


def moe_gmm(block_e, n_used, xb, wg, wu, wd):
    n_slots, d = xb.shape
    tm, tf = MOE_TILE, MOE_FF_TILE
    n_blocks = n_slots // tm
    f = wg.shape[2]
    nj = f // tf
    last = nj - 1

    def jj(i, j, nu):
        return jnp.where(i < nu[0], j, last)

    grid_spec = pltpu.PrefetchScalarGridSpec(
        num_scalar_prefetch=2,
        grid=(n_blocks, nj),
        in_specs=[pl.BlockSpec((tm, d), lambda i, j, be, nu: (i, 0)),
                  pl.BlockSpec((1, d, tf), lambda i, j, be, nu: (be[i], 0, jj(i, j, nu))),
                  pl.BlockSpec((1, d, tf), lambda i, j, be, nu: (be[i], 0, jj(i, j, nu))),
                  pl.BlockSpec((1, tf, d), lambda i, j, be, nu: (be[i], jj(i, j, nu), 0))],
        out_specs=pl.BlockSpec((tm, d), lambda i, j, be, nu: (i, 0)),
        scratch_shapes=[pltpu.VMEM((tm, d), F32)],
    )
    return pl.pallas_call(
        _moe_gmm_kernel,
        grid_spec=grid_spec,
        out_shape=jax.ShapeDtypeStruct((n_slots, d), F32),
        compiler_params=_cparams("arbitrary", "arbitrary"),
        name="moe_gmm",
    )(block_e, n_used, xb, wg, wu, wd)


def _moe_combine_kernel(h_ref, mod_ref, route_ref, y1_ref, y2_ref, o_ref):
    r = route_ref[0]
    y = r[:, 2:3] * y1_ref[0] + r[:, 3:4] * y2_ref[0]
    o_ref[0] = h_ref[0] + mod_ref[0, 5:6, :] * y


def moe_combine(h, mod, route, y1, y2):
    b, t, d = h.shape
    tm = TOKEN_TILE
    tok = lambda bi, i: (bi, i, 0)
    return pl.pallas_call(
        _moe_combine_kernel,
        grid=(b, t // tm),
        in_specs=[pl.BlockSpec((1, tm, d), tok),
                  pl.BlockSpec((1, 6, d), lambda bi, i: (bi, 0, 0)),
                  pl.BlockSpec((1, tm, 8), tok),
                  pl.BlockSpec((1, tm, d), tok),
                  pl.BlockSpec((1, tm, d), tok)],
        out_specs=pl.BlockSpec((1, tm, d), tok),
        out_shape=jax.ShapeDtypeStruct((b, t, d), F32),
        compiler_params=_cparams("parallel", "parallel"),
        name="moe_combine",
    )(h, mod, route, y1, y2)


def _moe_dispatch_plan(expert_idx, n_experts):
    n = expert_idx.shape[0]
    tile = MOE_TILE
    flat_e = expert_idx.reshape(-1)
    onehot = (flat_e[:, None] == jnp.arange(n_experts, dtype=jnp.int32)[None, :]).astype(jnp.int32)
    csum = jnp.cumsum(onehot, axis=0)
    counts = csum[-1]
    padded = (counts + tile - 1) // tile * tile
    pad_ends = jnp.cumsum(padded)
    pad_starts = pad_ends - padded
    slot = jnp.sum(onehot * (pad_starts[None, :] + csum - 1), axis=1)
    n_blocks = -(-(n * TOP_K) // tile) + n_experts
    n_used = pad_ends[-1] // tile
    blk = jnp.arange(n_blocks, dtype=jnp.int32)
    block_e = jnp.minimum(jnp.searchsorted(pad_ends, blk * tile, side="right"), n_experts - 1).astype(jnp.int32)
    block_e = jnp.where(blk < n_used, block_e, block_e[jnp.maximum(n_used - 1, 0)])
    slot_tok = jnp.zeros((n_blocks * tile,), jnp.int32).at[slot].set(jnp.arange(n * TOP_K, dtype=jnp.int32) // TOP_K)
    return slot.reshape(n, TOP_K), slot_tok, block_e, n_used.astype(jnp.int32).reshape(1)


def _rope_tables(n_ctx, n_lat):
    rows = n_lat // GRID_W
    row = jnp.repeat(jnp.arange(rows, dtype=F32), GRID_W)
    col = jnp.tile(jnp.arange(GRID_W, dtype=F32), rows)
    n_freq = HEAD_DIM // 4
    inv_freq = ROPE_THETA ** (-jnp.arange(n_freq, dtype=F32) / n_freq)
    ang = jnp.concatenate([row[:, None] * inv_freq, col[:, None] * inv_freq], -1)
    cos, sin = jnp.cos(ang), jnp.sin(ang)
    cos_t = jnp.concatenate([cos, cos, cos, cos], axis=1)
    sin_t = jnp.concatenate([-sin, sin, -sin, sin], axis=1)
    cos_t = jnp.concatenate([jnp.ones((n_ctx, 128), F32), cos_t], axis=0)
    sin_t = jnp.concatenate([jnp.zeros((n_ctx, 128), F32), sin_t], axis=0)
    return cos_t, sin_t


def _block_ones(n):
    idx = jnp.arange(n) // HEAD_DIM
    return (idx[:, None] == idx[None, :]).astype(BF16)


def kernel(x, c, ctx, c_ctx, mod_w, mod_b, norm1_w, norm2_w, w_in, attn_qn_w, attn_kn_w, ret_decay, ret_norm_w,
           mlstm_conv_w, mlstm_gate_b, mlstm_norm_w, w_out, ffn_w_gate, ffn_w_up, ffn_w_down, router_w, router_b,
           moe_w_gate, moe_w_up, moe_w_down):
    b, n_lat, d = x.shape
    n_ctx = ctx.shape[1]
    depth = mod_w.shape[0]
    n_experts = router_w.shape[2]
    ret_heads = ret_decay.shape[2]
    ml_heads = mlstm_gate_b.shape[2]
    ret_w, ml_w = ret_heads * HEAD_DIM, ml_heads * HEAD_DIM
    n_gate = N_GATES * ml_heads
    kv_w = (w_in.shape[2] - 4 * ret_w - 4 * ml_w - n_gate) // 6
    att_w = 4 * kv_w
    dims = (att_w, kv_w, ret_w, ml_w, n_gate)
    assert n_ctx % TOKEN_TILE == 0 and n_lat % KV_TILE == 0 and b + 1 <= 16
    n_main = w_in.shape[2] - n_gate
    n_pad = -(-w_in.shape[2] // 128) * 128

    cos_t, sin_t = _rope_tables(n_ctx, n_lat)
    gq, gk, g_ret, g_ml = _block_ones(att_w), _block_ones(kv_w), _block_ones(ret_w), _block_ones(ml_w)
    cond = jnp.concatenate([c, c_ctx[None, :], jnp.zeros((16 - b - 1, d), F32)], axis=0)
    ctx_tiles = n_ctx // TOKEN_TILE
    ctx_chunks = n_ctx // CHUNK

    h = jnp.concatenate([ctx, x], axis=1)
    for layer in range(depth):
        last = layer == depth - 1
        skip = ctx_tiles if last else 0
        mod = adaln(cond, mod_w[layer], mod_b[layer]).reshape(16, 6, d)[:b + 1]
        w_all = jnp.pad(w_in[layer].astype(BF16), ((0, 0), (0, n_pad - w_in.shape[2])))
        w_gate_t = w_in[layer][:, n_main:].T.astype(BF16)
        qn = jnp.tile(attn_qn_w[layer], att_w // HEAD_DIM)[None, :]
        kn = jnp.tile(attn_kn_w[layer], kv_w // HEAD_DIM)[None, :]
        q, k, v, r, m, gcol, grow = proj_in(h, mod, norm1_w[layer][None, :], cos_t, sin_t, w_all, w_gate_t, qn, kn,
                                            gq, gk, n_ctx=n_ctx, dims=dims)
        att = attention(q, k, v, n_ctx=n_ctx, skip=skip)
        dec = jnp.repeat(ret_decay[layer], HEAD_DIM, axis=1)
        sf, sb = ret_states(r, dec, n_ctx=n_ctx)
        ret = ret_out(r, sf, sb, dec, ret_norm_w[layer][None, :], g_ret, skip_chunks=ctx_chunks if last else 0)
        qk = mlstm_conv(m, mlstm_conv_w[layer], n_ctx=n_ctx)
        bias = mlstm_gate_b[layer].reshape(-1)
        cf, nmf, cb, nmb = mlstm_states(qk, m, gcol, bias[None, :], n_ctx=n_ctx, n_heads=ml_heads)
        ml = mlstm_out(qk, m, gcol, grow, bias[None, :], bias[:, None], cf, nmf, cb, nmb, mlstm_norm_w[layer][None, :],
                       g_ml, skip_chunks=ctx_chunks if last else 0, n_heads=ml_heads)
        h = proj_out(att, ret, ml, h, mod, w_out[layer].astype(BF16), n_ctx=n_ctx, skip=skip)
        if layer % 2 == 0:
            li = layer // 2
            assert not last, "dense FFN on a latent-only stream is not wired up"
            h = ffn_dense(h, mod, norm2_w[layer][None, :], ffn_w_gate[li].astype(BF16), ffn_w_up[li].astype(BF16),
                          ffn_w_down[li].astype(BF16), n_ctx=n_ctx)
        else:
            li = layer // 2
            assert last, "expert FFN on the combined stream is not wired up"
            mod_lat = mod[:b]
            rw = jnp.pad(router_w[li], ((0, 0), (0, 128 - n_experts)))
            rb = jnp.pad(router_b[li], (0, 128 - n_experts))[None, :]
            xn, route = moe_pre(h, mod_lat, norm2_w[layer][None, :], rw, rb, n_experts=n_experts)
            t = h.shape[1]
            route_flat = route.reshape(b * t, 8)
            slot, slot_tok, block_e, n_used = _moe_dispatch_plan(route_flat[:, 0:TOP_K].astype(jnp.int32), n_experts)
            xb = jnp.take(xn.reshape(b * t, d), slot_tok, axis=0)
            yb = moe_gmm(block_e, n_used, xb, moe_w_gate[li].astype(BF16), moe_w_up[li].astype(BF16),
                         moe_w_down[li].astype(BF16))
            y1 = jnp.take(yb, slot[:, 0], axis=0).reshape(b, t, d)
            y2 = jnp.take(yb, slot[:, 1], axis=0).reshape(b, t, d)
            h = moe_combine(h, mod_lat, route, y1, y2)
    return h if h.shape[1] == n_lat else h[:, n_ctx:, :]
```

```python
import functools
import math

import jax
import jax.numpy as jnp
from jax import lax
from jax.experimental import pallas as pl
from jax.experimental.pallas import tpu as pltpu

F32 = jnp.float32
BF16 = jnp.bfloat16

HEAD_DIM = 64
GRID_W = 64
ROPE_THETA = 10000.0
EPS = 1e-6
CHUNK = 128
TOKEN_TILE = 256
KV_TILE = 512
N_GATES = 4
TOP_K = 2
MOE_TILE = 512
MOE_FF_TILE = 512
VMEM_LIMIT = 56 * 1024 * 1024
QK_SCALE = HEAD_DIM ** -0.5
LOG2E = math.log2(math.e)
MAX_SCORE_SPAN_LOG2 = 100.0


def _cparams(*sem):
    return pltpu.CompilerParams(dimension_semantics=sem, vmem_limit_bytes=VMEM_LIMIT)


def _silu(x):
    return x * jax.nn.sigmoid(x)


def _split_dot(x, ones_bf16, parts=2):
    acc = None
    r = x
    for _ in range(parts):
        hi = r.astype(BF16)
        d = jnp.dot(hi, ones_bf16, preferred_element_type=F32)
        acc = d if acc is None else acc + d
        r = r - hi.astype(F32)
    return acc


def _dot_nt(a, b):
    return lax.dot_general(a, b, (((1,), (1,)), ((), ())), preferred_element_type=F32)


def _rope(x, cos_t, sin_t):
    w = x.shape[1]
    reps = w // 128
    c = jnp.concatenate([cos_t] * reps, axis=1) if reps > 1 else cos_t
    s = jnp.concatenate([sin_t] * reps, axis=1) if reps > 1 else sin_t
    lane = lax.broadcasted_iota(jnp.int32, x.shape, 1)
    first_half = (lane % HEAD_DIM) < (HEAD_DIM // 2)
    swapped = jnp.where(first_half, pltpu.roll(x, w - HEAD_DIM // 2, 1), pltpu.roll(x, HEAD_DIM // 2, 1))
    return x * c + swapped * s


def _group_rms(x, gmat, gain):
    ms = _split_dot(x * x, gmat) * (1.0 / HEAD_DIM)
    return x * lax.rsqrt(ms + EPS) * gain


def _group_layer_norm(x, gmat, gain):
    mu = _split_dot(x, gmat) * (1.0 / HEAD_DIM)
    xc = x - mu
    var = _split_dot(xc * xc, gmat) * (1.0 / HEAD_DIM)
    return xc * lax.rsqrt(var + EPS) * gain


def _expand_heads(x, n_heads, lane):
    out = x[:, n_heads - 1:n_heads]
    for h in range(n_heads - 2, -1, -1):
        out = jnp.where(lane < (h + 1) * HEAD_DIM, x[:, h:h + 1], out)
    return out


def _block_diag_mask(n):
    r = lax.broadcasted_iota(jnp.int32, (n, n), 0) // HEAD_DIM
    c = lax.broadcasted_iota(jnp.int32, (n, n), 1) // HEAD_DIM
    return r == c


def _adaln_kernel(c_ref, w_ref, b_ref, o_ref):
    a = _silu(c_ref[...]).astype(BF16)
    o_ref[...] = jnp.dot(a, w_ref[...].astype(BF16), preferred_element_type=F32) + b_ref[...]


def adaln(cond, w, b):
    r, d = cond.shape
    n = w.shape[1]
    tn = 1536 if n % 1536 == 0 else n
    return pl.pallas_call(
        _adaln_kernel,
        grid=(n // tn,),
        in_specs=[pl.BlockSpec((r, d), lambda j: (0, 0)),
                  pl.BlockSpec((d, tn), lambda j: (0, j)),
                  pl.BlockSpec((1, tn), lambda j: (0, j))],
        out_specs=pl.BlockSpec((r, tn), lambda j: (0, j)),
        out_shape=jax.ShapeDtypeStruct((r, n), F32),
        compiler_params=_cparams("arbitrary"),
        name="adaln",
    )(cond, w, b.reshape(1, n))


def _norm_mod(x, nw, shift, scale):
    ms = jnp.mean(x * x, axis=-1, keepdims=True)
    y = x * lax.rsqrt(ms + EPS) * nw
    return y * (1.0 + scale) + shift


def _proj_in_kernel(h_ref, mod_ref, nw_ref, cos_ref, sin_ref, w_ref, wgt_ref, qn_ref, kn_ref, kb_ref, gq_ref, gk_ref,
                    q_ref, k_ref, v_ref, r_ref, m_ref, gc_ref, gr_ref, *, dims):
    att_w, kv_w, ret_w, ml_w, n_gate = dims
    x = h_ref[0]
    a = _norm_mod(x, nw_ref[...], mod_ref[0, 0:1, :], mod_ref[0, 1:2, :]).astype(BF16)
    p = jnp.dot(a, w_ref[...], preferred_element_type=F32)
    cos_t, sin_t = cos_ref[...], sin_ref[...]
    o = 0
    aq = p[:, o:o + att_w]; o += att_w
    ak = p[:, o:o + kv_w]; o += kv_w
    av = p[:, o:o + kv_w]; o += kv_w
    aq = _rope(_group_rms(aq, gq_ref[...], qn_ref[...]), cos_t, sin_t) * (QK_SCALE * LOG2E)
    ak = _rope(_group_rms(ak, gk_ref[...], kn_ref[...]), cos_t, sin_t)
    neg_bound = -jnp.sqrt(_split_dot(aq * aq, gq_ref[...])) * kb_ref[...]
    tm = x.shape[0]
    lane = lax.broadcasted_iota(jnp.int32, (tm, 2 * HEAD_DIM), 1)

    def widen(a, extra):
        return jnp.where(lane == HEAD_DIM, extra, jnp.concatenate([a, jnp.zeros_like(a)], axis=1)).astype(BF16)

    for hh in range(att_w // HEAD_DIM):
        cols = slice(hh * HEAD_DIM, (hh + 1) * HEAD_DIM)
        q_ref[0, hh] = widen(aq[:, cols], neg_bound[:, hh * HEAD_DIM:hh * HEAD_DIM + 1])
    for hh in range(kv_w // HEAD_DIM):
        cols = slice(hh * HEAD_DIM, (hh + 1) * HEAD_DIM)
        k_ref[0, hh] = widen(ak[:, cols], 1.0)
        v_ref[0, hh] = widen(av[:, cols], 1.0)
    rq = _rope(p[:, o:o + ret_w], cos_t, sin_t)
    rk = _rope(p[:, o + ret_w:o + 2 * ret_w] * QK_SCALE, cos_t, sin_t)
    r_ref[0, :, 0:ret_w] = rq
    r_ref[0, :, ret_w:2 * ret_w] = rk
    r_ref[0, :, 2 * ret_w:4 * ret_w] = p[:, o + 2 * ret_w:o + 4 * ret_w]
    o += 4 * ret_w
    m_ref[0] = p[:, o:o + 4 * ml_w]
    o += 4 * ml_w
    gc_ref[0] = p[:, o:o + n_gate]
    gr_ref[0] = _dot_nt(wgt_ref[...], a)


def proj_in(h, mod, norm_w, cos_t, sin_t, w_all, w_gate_t, qn, kn, kbound, gq, gk, *, n_ctx, dims):
    att_w, kv_w, ret_w, ml_w, n_gate = dims
    b, s, d = h.shape
    tm = TOKEN_TILE
    n_ctx_tiles = n_ctx // tm
    n_pad = w_all.shape[1]
    ctx_row = mod.shape[0] - 1

    def mod_map(bi, i):
        return (jnp.where(i < n_ctx_tiles, ctx_row, bi), 0, 0)

    const2 = lambda bi, i: (0, 0)
    out_shape = (
        jax.ShapeDtypeStruct((b, att_w // HEAD_DIM, s, 2 * HEAD_DIM), BF16),
        jax.ShapeDtypeStruct((b, kv_w // HEAD_DIM, s, 2 * HEAD_DIM), BF16),
        jax.ShapeDtypeStruct((b, kv_w // HEAD_DIM, s, 2 * HEAD_DIM), BF16),
        jax.ShapeDtypeStruct((b, s, 4 * ret_w), F32),
        jax.ShapeDtypeStruct((b, s, 4 * ml_w), F32),
        jax.ShapeDtypeStruct((b, s, n_gate), F32),
        jax.ShapeDtypeStruct((b, n_gate, s), F32),
    )
    return pl.pallas_call(
        functools.partial(_proj_in_kernel, dims=dims),
        grid=(b, s // tm),
        in_specs=[
            pl.BlockSpec((1, tm, d), lambda bi, i: (bi, i, 0)),
            pl.BlockSpec((1, 6, d), mod_map),
            pl.BlockSpec((1, d), const2),
            pl.BlockSpec((tm, 128), lambda bi, i: (i, 0)),
            pl.BlockSpec((tm, 128), lambda bi, i: (i, 0)),
            pl.BlockSpec((d, n_pad), const2),
            pl.BlockSpec((n_gate, d), const2),
            pl.BlockSpec((1, att_w), const2),
            pl.BlockSpec((1, kv_w), const2),
            pl.BlockSpec((1, att_w), const2),
            pl.BlockSpec((att_w, att_w), const2),
            pl.BlockSpec((kv_w, kv_w), const2),
        ],
        out_specs=(
            pl.BlockSpec((1, att_w // HEAD_DIM, tm, 2 * HEAD_DIM), lambda bi, i: (bi, 0, i, 0)),
            pl.BlockSpec((1, kv_w // HEAD_DIM, tm, 2 * HEAD_DIM), lambda bi, i: (bi, 0, i, 0)),
            pl.BlockSpec((1, kv_w // HEAD_DIM, tm, 2 * HEAD_DIM), lambda bi, i: (bi, 0, i, 0)),
            pl.BlockSpec((1, tm, 4 * ret_w), lambda bi, i: (bi, i, 0)),
            pl.BlockSpec((1, tm, 4 * ml_w), lambda bi, i: (bi, i, 0)),
            pl.BlockSpec((1, tm, n_gate), lambda bi, i: (bi, i, 0)),
            pl.BlockSpec((1, n_gate, tm), lambda bi, i: (bi, 0, i)),
        ),
        out_shape=out_shape,
        compiler_params=_cparams("parallel", "parallel"),
        name="proj_in",
    )(h, mod, norm_w, cos_t, sin_t, w_all, w_gate_t, qn, kn, kbound, gq, gk)


def _conv_kernel(x_ref, prev_ref, next_ref, w_ref, o_ref, *, n_ctx, seq, half):
    i = pl.program_id(1)
    x = x_ref[0]
    tm, w = x.shape
    row = lax.broadcasted_iota(jnp.int32, (tm, 1), 0)
    grow = row + i * tm
    xp = jnp.where(row == 0, prev_ref[0, 7:8, :], pltpu.roll(x, 1, 0))
    xn = jnp.where(row == tm - 1, next_ref[0, 0:1, :], pltpu.roll(x, tm - 1, 0))
    xp = jnp.where((grow == 0) | (grow == n_ctx), 0.0, xp)
    xn = jnp.where((grow == n_ctx - 1) | (grow == seq - 1), 0.0, xn)
    y = _silu(xp * w_ref[0:1, :] + x * w_ref[1:2, :] + xn * w_ref[2:3, :])
    lane = lax.broadcasted_iota(jnp.int32, (tm, w), 1)
    o_ref[0] = jnp.where(lane >= half, y * QK_SCALE, y)


def mlstm_conv(m, conv_w, *, n_ctx):
    b, s, _ = m.shape
    w = conv_w.shape[1]
    tm = TOKEN_TILE
    r8 = tm // 8
    last8 = s // 8 - 1
    return pl.pallas_call(
        functools.partial(_conv_kernel, n_ctx=n_ctx, seq=s, half=w // 2),
        grid=(b, s // tm),
        in_specs=[pl.BlockSpec((1, tm, w), lambda bi, i: (bi, i, 0)),
                  pl.BlockSpec((1, 8, w), lambda bi, i: (bi, jnp.maximum(i * r8 - 1, 0), 0)),
                  pl.BlockSpec((1, 8, w), lambda bi, i: (bi, jnp.minimum((i + 1) * r8, last8), 0)),
                  pl.BlockSpec((3, w), lambda bi, i: (0, 0))],
        out_specs=pl.BlockSpec((1, tm, w), lambda bi, i: (bi, i, 0)),
        out_shape=jax.ShapeDtypeStruct((b, s, w), F32),
        compiler_params=_cparams("parallel", "parallel"),
        name="mlstm_conv",
    )(m, m, m, conv_w)


def _attn_kernel(fast_ref, q_ref, k_ref, v_ref, o_ref, m_sc, acc_sc, *, n_ctx, n_lat_tiles, skip):
    i = pl.program_id(2) + skip
    g, tq = q_ref.shape[1], q_ref.shape[2]
    hd = HEAD_DIM

    def fast_step(k, v):
        for h in range(g):
            rows = slice(h * tq, (h + 1) * tq)
            p = jnp.exp2(_dot_nt(q_ref[0, h], k))
            acc_sc[rows] += jnp.dot(p.astype(BF16), v, preferred_element_type=F32)

    def safe_step(k, v):
        for h in range(g):
            rows = slice(h * tq, (h + 1) * tq)
            s = _dot_nt(q_ref[0, h], k)
            m_old = m_sc[rows]
            m_new = jnp.maximum(m_old, jnp.max(s, axis=-1, keepdims=True))
            p = jnp.exp2(s - jnp.concatenate([m_new] * (s.shape[1] // 128), axis=1))
            acc_sc[rows] = jnp.exp2(m_old - m_new) * acc_sc[rows] + jnp.dot(p.astype(BF16), v, preferred_element_type=F32)
            m_sc[rows] = m_new

    def sweep(step, unroll):
        step(k_ref[0, 0, 0:n_ctx, :], v_ref[0, 0, 0:n_ctx, :])

        @pl.when(i * tq >= n_ctx)
        def _():
            def body(j, carry):
                start = pl.multiple_of(n_ctx + j * KV_TILE, math.gcd(n_ctx, KV_TILE))
                step(k_ref[0, 0, pl.ds(start, KV_TILE), :], v_ref[0, 0, pl.ds(start, KV_TILE), :])
                return carry
            lax.fori_loop(0, n_lat_tiles, body, 0, unroll=unroll)

    acc_sc[...] = jnp.zeros(acc_sc.shape, F32)

    @pl.when(fast_ref[0] == 1)
    def _():
        sweep(fast_step, 2)

    @pl.when(fast_ref[0] != 1)
    def _():
        m_sc[...] = jnp.full(m_sc.shape, -jnp.inf, F32)
        sweep(safe_step, 1)

    acc = acc_sc[...]
    out = acc[:, 0:hd] / acc[:, hd:hd + 1]
    o_ref[0] = jnp.concatenate([out[h * tq:(h + 1) * tq] for h in range(g)], axis=1)


def attention(fast, q, k, v, *, n_ctx, skip):
    b, hq, s, lanes = q.shape
    hkv = k.shape[1]
    g = hq // hkv
    tq = TOKEN_TILE
    n_lat = s - n_ctx
    assert n_lat % KV_TILE == 0 and n_ctx % tq == 0 and lanes == 2 * HEAD_DIM
    nq = s // tq - skip
    grid_spec = pltpu.PrefetchScalarGridSpec(
        num_scalar_prefetch=1,
        grid=(b, hkv, nq),
        in_specs=[pl.BlockSpec((1, g, tq, lanes), lambda bi, gi, i, f: (bi, gi, i + skip, 0)),
                  pl.BlockSpec((1, 1, s, lanes), lambda bi, gi, i, f: (bi, gi, 0, 0)),
                  pl.BlockSpec((1, 1, s, lanes), lambda bi, gi, i, f: (bi, gi, 0, 0))],
        out_specs=pl.BlockSpec((1, tq, g * HEAD_DIM), lambda bi, gi, i, f: (bi, i, gi)),
        scratch_shapes=[pltpu.VMEM((g * tq, 128), F32), pltpu.VMEM((g * tq, lanes), F32)],
    )
    return pl.pallas_call(
        functools.partial(_attn_kernel, n_ctx=n_ctx, n_lat_tiles=n_lat // KV_TILE, skip=skip),
        grid_spec=grid_spec,
        out_shape=jax.ShapeDtypeStruct((b, nq * tq, hq * HEAD_DIM), F32),
        compiler_params=_cparams("parallel", "parallel", "arbitrary"),
        name="attention",
    )(fast, q, k, v)


def _rev_chunk(c, nc_ctx, nc):
    return jnp.where(c < nc_ctx, nc_ctx - 1 - c, nc - 1 + nc_ctx - c)


def _ret_state_kernel(kf_ref, vf_ref, kb_ref, vb_ref, dec_ref, sf_ref, sb_ref, s_sc):
    c = pl.program_id(1)

    @pl.when(c == 0)
    def _():
        s_sc[...] = jnp.zeros(s_sc.shape, F32)

    w = kf_ref.shape[2]
    lg = jnp.log1p(-jnp.exp(dec_ref[...]))
    pos = lax.broadcasted_iota(jnp.int32, (CHUNK, 1), 0).astype(F32)
    bd = _block_diag_mask(w)
    for d, (k_ref, v_ref, out_ref) in enumerate(((kf_ref, vf_ref, sf_ref), (kb_ref, vb_ref, sb_ref))):
        lgd = lg[d:d + 1, :]
        e = (CHUNK - 1.0 - pos) if d == 0 else pos
        kdec = k_ref[0] * jnp.exp(e * lgd)
        inc = jnp.dot(kdec.T.astype(BF16), v_ref[0].astype(BF16), preferred_element_type=F32)
        s_old = s_sc[d]
        out_ref[0, 0] = s_old.astype(BF16)
        s_sc[d] = jnp.exp(CHUNK * lgd) * s_old + jnp.where(bd, inc, 0.0)


def ret_states(r, dec, *, n_ctx):
    b, s, w4 = r.shape
    w = w4 // 4
    nc, nc_ctx = s // CHUNK, n_ctx // CHUNK
    rev = lambda c: _rev_chunk(c, nc_ctx, nc)
    st = jax.ShapeDtypeStruct((b, nc, w, w), BF16)
    return pl.pallas_call(
        _ret_state_kernel,
        grid=(b, nc),
        in_specs=[pl.BlockSpec((1, CHUNK, w), lambda bi, c: (bi, c, 1)),
                  pl.BlockSpec((1, CHUNK, w), lambda bi, c: (bi, c, 2)),
                  pl.BlockSpec((1, CHUNK, w), lambda bi, c: (bi, rev(c), 1)),
                  pl.BlockSpec((1, CHUNK, w), lambda bi, c: (bi, rev(c), 2)),
                  pl.BlockSpec((2, w), lambda bi, c: (0, 0))],
        out_specs=(pl.BlockSpec((1, 1, w, w), lambda bi, c: (bi, c, 0, 0)),
                   pl.BlockSpec((1, 1, w, w), lambda bi, c: (bi, rev(c), 0, 0))),
        out_shape=(st, st),
        scratch_shapes=[pltpu.VMEM((2, w, w), F32)],
        compiler_params=_cparams("parallel", "arbitrary"),
        name="ret_states",
    )(r, r, r, r, dec)


def _ret_out_kernel(r_ref, sf_ref, sb_ref, dec_ref, nw_ref, gmat_ref, o_ref):
    w = o_ref.shape[2]
    x = r_ref[0]
    q, k, v, g = x[:, 0:w], x[:, w:2 * w], x[:, 2 * w:3 * w], x[:, 3 * w:4 * w]
    lg = jnp.log1p(-jnp.exp(dec_ref[...]))
    pos = lax.broadcasted_iota(jnp.int32, (CHUNK, 1), 0).astype(F32)
    row = lax.broadcasted_iota(jnp.int32, (CHUNK, CHUNK), 0)
    col = lax.broadcasted_iota(jnp.int32, (CHUNK, CHUNK), 1)
    lane = lax.broadcasted_iota(jnp.int32, (CHUNK, w), 1)
    kb, vb = k.astype(BF16), v.astype(BF16)
    total = jnp.zeros((CHUNK, w), F32)
    for d, s_ref in enumerate((sf_ref, sb_ref)):
        lgd = lg[d:d + 1, :]
        qe = (pos + 1.0) if d == 0 else (CHUNK - pos)
        qdec = q * jnp.exp(qe * lgd)
        acc = jnp.dot(qdec.astype(BF16), s_ref[0, 0], preferred_element_type=F32)
        lag = ((row - col) if d == 0 else (col - row)).astype(F32)
        for h in range(w // HEAD_DIM):
            hm = (lane >= h * HEAD_DIM) & (lane < (h + 1) * HEAD_DIM)
            sc = _dot_nt(jnp.where(hm, q, 0.0).astype(BF16), kb)
            lgh = lgd[:, h * HEAD_DIM:h * HEAD_DIM + 1]
            decay = jnp.where(lag >= 0, jnp.exp(jnp.maximum(lag, 0.0) * lgh), 0.0)
            pv = jnp.dot((sc * decay).astype(BF16), vb, preferred_element_type=F32)
            acc = acc + jnp.where(hm, pv, 0.0)
        total = total + acc
    o_ref[0] = _group_layer_norm(total, gmat_ref[...], nw_ref[...]) * _silu(g)


def ret_out(r, sf, sb, dec, norm_w, gmat, *, skip_chunks):
    b, s, w4 = r.shape
    w = w4 // 4
    nc = s // CHUNK - skip_chunks
    return pl.pallas_call(
        _ret_out_kernel,
        grid=(b, nc),
        in_specs=[pl.BlockSpec((1, CHUNK, w4), lambda bi, c: (bi, c + skip_chunks, 0)),
                  pl.BlockSpec((1, 1, w, w), lambda bi, c: (bi, c + skip_chunks, 0, 0)),
                  pl.BlockSpec((1, 1, w, w), lambda bi, c: (bi, c + skip_chunks, 0, 0)),
                  pl.BlockSpec((2, w), lambda bi, c: (0, 0)),
                  pl.BlockSpec((1, w), lambda bi, c: (0, 0)),
                  pl.BlockSpec((w, w), lambda bi, c: (0, 0))],
        out_specs=pl.BlockSpec((1, CHUNK, w), lambda bi, c: (bi, c, 0)),
        out_shape=jax.ShapeDtypeStruct((b, nc * CHUNK, w), F32),
        compiler_params=_cparams("parallel", "parallel"),
        name="ret_out",
    )(r, sf, sb, dec, norm_w, gmat)


def _tri_masks():
    row = lax.broadcasted_iota(jnp.int32, (CHUNK, CHUNK), 0)
    col = lax.broadcasted_iota(jnp.int32, (CHUNK, CHUNK), 1)
    return col <= row, col >= row


def _ones_dot(ones_bf16, x, parts=3):
    acc = None
    r = x
    for _ in range(parts):
        hi = r.astype(BF16)
        d = jnp.dot(ones_bf16, hi, preferred_element_type=F32)
        acc = d if acc is None else acc + d
        r = r - hi.astype(F32)
    return acc


def _mlstm_state_kernel(kf_ref, vf_ref, gf_ref, kb_ref, vb_ref, gb_ref, bias_ref,
                        cf_ref, nmf_ref, cb_ref, nmb_ref, c_sc, nm_sc, *, n_heads):
    c = pl.program_id(1)

    @pl.when(c == 0)
    def _():
        c_sc[...] = jnp.zeros(c_sc.shape, F32)
        nm_sc[...] = jnp.zeros(nm_sc.shape, F32)

    w = kf_ref.shape[2]
    hh = n_heads
    tril, triu = _tri_masks()
    lane_t = lax.broadcasted_iota(jnp.int32, (CHUNK, w), 1)
    lane_1 = lax.broadcasted_iota(jnp.int32, (1, w), 1)
    bd = _block_diag_mask(w)
    dirs = ((kf_ref, vf_ref, gf_ref, cf_ref, nmf_ref, tril), (kb_ref, vb_ref, gb_ref, cb_ref, nmb_ref, triu))
    for d, (k_ref, v_ref, g_ref, c_out, nm_out, tmat) in enumerate(dirs):
        pre = g_ref[0] + bias_ref[...]
        ig = pre[:, d * hh:(d + 1) * hh]
        lf_all = jax.nn.log_sigmoid(pre)
        b_all = _ones_dot(tmat.astype(BF16), lf_all)
        lf = lf_all[:, (2 + d) * hh:(3 + d) * hh]
        bcum = b_all[:, (2 + d) * hh:(3 + d) * hh]
        b_end = jnp.sum(lf, axis=0, keepdims=True)
        a = b_end - bcum + ig
        m_loc = jnp.max(a, axis=0, keepdims=True)
        wgt = jnp.exp(a - m_loc)
        kw = k_ref[0] * _expand_heads(wgt, hh, lane_t)
        c_inc = jnp.dot(kw.T.astype(BF16), v_ref[0].astype(BF16), preferred_element_type=F32)
        n_inc = jnp.sum(kw, axis=0, keepdims=True)
        be = _expand_heads(b_end, hh, lane_1)
        ml = _expand_heads(m_loc, hh, lane_1)
        c_old = c_sc[d]
        n_old = nm_sc[d, 0:1, :]
        m_old = nm_sc[d, 1:2, :]
        c_out[0, 0] = c_old.astype(BF16)
        nm_out[0, 0] = nm_sc[d]
        m_new = jnp.maximum(be + m_old, ml)
        w_prev = jnp.exp(be + m_old - m_new)
        w_inc = jnp.exp(ml - m_new)
        c_sc[d] = w_prev * c_old + w_inc * jnp.where(bd, c_inc, 0.0)
        nm_sc[d, 0:1, :] = w_prev * n_old + w_inc * n_inc
        nm_sc[d, 1:2, :] = m_new


def mlstm_states(qk, m, gcol, bias_row, *, n_ctx, n_heads):
    b, s, w2 = qk.shape
    w = w2 // 2
    ng = gcol.shape[2]
    nc, nc_ctx = s // CHUNK, n_ctx // CHUNK
    rev = lambda c: _rev_chunk(c, nc_ctx, nc)
    cst = jax.ShapeDtypeStruct((b, nc, w, w), BF16)
    nmst = jax.ShapeDtypeStruct((b, nc, 8, w), F32)
    fwd3 = lambda j: (lambda bi, c: (bi, c, j))
    bwd3 = lambda j: (lambda bi, c: (bi, rev(c), j))
    return pl.pallas_call(
        functools.partial(_mlstm_state_kernel, n_heads=n_heads),
        grid=(b, nc),
        in_specs=[pl.BlockSpec((1, CHUNK, w), fwd3(1)),
                  pl.BlockSpec((1, CHUNK, w), fwd3(2)),
                  pl.BlockSpec((1, CHUNK, ng), fwd3(0)),
                  pl.BlockSpec((1, CHUNK, w), bwd3(1)),
                  pl.BlockSpec((1, CHUNK, w), bwd3(2)),
                  pl.BlockSpec((1, CHUNK, ng), bwd3(0)),
                  pl.BlockSpec((1, ng), lambda bi, c: (0, 0))],
        out_specs=(pl.BlockSpec((1, 1, w, w), lambda bi, c: (bi, c, 0, 0)),
                   pl.BlockSpec((1, 1, 8, w), lambda bi, c: (bi, c, 0, 0)),
                   pl.BlockSpec((1, 1, w, w), lambda bi, c: (bi, rev(c), 0, 0)),
                   pl.BlockSpec((1, 1, 8, w), lambda bi, c: (bi, rev(c), 0, 0))),
        out_shape=(cst, nmst, cst, nmst),
        scratch_shapes=[pltpu.VMEM((2, w, w), F32), pltpu.VMEM((2, 8, w), F32)],
        compiler_params=_cparams("parallel", "arbitrary"),
        name="mlstm_states",
    )(qk, m, gcol, qk, m, gcol, bias_row)


def _mlstm_out_kernel(qk_ref, m_ref, gc_ref, gr_ref, bc_ref, br_ref, cf_ref, nmf_ref, cb_ref, nmb_ref,
                      nw_ref, gmat_ref, o_ref, *, n_heads):
    w = o_ref.shape[2]
    hh = n_heads
    qk = qk_ref[0]
    q, k = qk[:, 0:w], qk[:, w:2 * w]
    mm = m_ref[0]
    v, og = mm[:, 2 * w:3 * w], mm[:, 3 * w:4 * w]
    pre_c = gc_ref[0] + bc_ref[...]
    pre_r = gr_ref[0] + br_ref[...]
    lf_c = jax.nn.log_sigmoid(pre_c)
    lf_r = jax.nn.log_sigmoid(pre_r)
    tril, triu = _tri_masks()
    lane = lax.broadcasted_iota(jnp.int32, (CHUNK, w), 1)
    kb, vb = k.astype(BF16), v.astype(BF16)
    gmat = gmat_ref[...]
    total = jnp.zeros((CHUNK, w), F32)
    for d, (c_ref, nm_ref) in enumerate(((cf_ref, nmf_ref), (cb_ref, nmb_ref))):
        tmat = tril if d == 0 else triu
        tmat_t = triu if d == 0 else tril
        b_c = _ones_dot(tmat.astype(BF16), lf_c)
        b_r = _split_dot(lf_r, tmat_t.astype(BF16), parts=3)
        nm = nm_ref[0, 0]
        n_row, m_row = nm[0:1, :], nm[1:2, :]
        cross = jnp.dot(q.astype(BF16), c_ref[0, 0], preferred_element_type=F32)
        qn = _split_dot(q * n_row, gmat)
        num = jnp.zeros((CHUNK, w), F32)
        den_e = jnp.zeros((CHUNK, w), F32)
        wp_e = jnp.zeros((CHUNK, w), F32)
        mt_e = jnp.zeros((CHUNK, w), F32)
        for h in range(hh):
            fc, ic = (2 + d) * hh + h, d * hh + h
            bcol = b_c[:, fc:fc + 1]
            logd = jnp.where(tmat, bcol - b_r[fc:fc + 1, :] + pre_r[ic:ic + 1, :], -jnp.inf)
            gg = bcol + m_row[:, h * HEAD_DIM:h * HEAD_DIM + 1]
            m_t = jnp.maximum(gg, jnp.max(logd, axis=-1, keepdims=True))
            hm = (lane >= h * HEAD_DIM) & (lane < (h + 1) * HEAD_DIM)
            sc = _dot_nt(jnp.where(hm, q, 0.0).astype(BF16), kb)
            p = sc * jnp.exp(logd - m_t)
            pv = jnp.dot(p.astype(BF16), vb, preferred_element_type=F32)
            num = num + jnp.where(hm, pv, 0.0)
            den_e = jnp.where(hm, jnp.sum(p, axis=-1, keepdims=True), den_e)
            wp_e = jnp.where(hm, jnp.exp(gg - m_t), wp_e)
            mt_e = jnp.where(hm, m_t, mt_e)
        num = num + wp_e * cross
        den = den_e + wp_e * qn
        total = total + num / jnp.maximum(jnp.abs(den), jnp.exp(-mt_e))
    o_ref[0] = _group_layer_norm(total * jax.nn.sigmoid(og), gmat, nw_ref[...])


def mlstm_out(qk, m, gcol, grow, bias_row, bias_col, cf, nmf, cb, nmb, norm_w, gmat, *, skip_chunks, n_heads):
    b, s, w2 = qk.shape
    w = w2 // 2
    ng = gcol.shape[2]
    nc = s // CHUNK - skip_chunks
    sk = skip_chunks
    return pl.pallas_call(
        functools.partial(_mlstm_out_kernel, n_heads=n_heads),
        grid=(b, nc),
        in_specs=[pl.BlockSpec((1, CHUNK, w2), lambda bi, c: (bi, c + sk, 0)),
                  pl.BlockSpec((1, CHUNK, 4 * w), lambda bi, c: (bi, c + sk, 0)),
                  pl.BlockSpec((1, CHUNK, ng), lambda bi, c: (bi, c + sk, 0)),
                  pl.BlockSpec((1, ng, CHUNK), lambda bi, c: (bi, 0, c + sk)),
                  pl.BlockSpec((1, ng), lambda bi, c: (0, 0)),
                  pl.BlockSpec((ng, 1), lambda bi, c: (0, 0)),
                  pl.BlockSpec((1, 1, w, w), lambda bi, c: (bi, c + sk, 0, 0)),
                  pl.BlockSpec((1, 1, 8, w), lambda bi, c: (bi, c + sk, 0, 0)),
                  pl.BlockSpec((1, 1, w, w), lambda bi, c: (bi, c + sk, 0, 0)),
                  pl.BlockSpec((1, 1, 8, w), lambda bi, c: (bi, c + sk, 0, 0)),
                  pl.BlockSpec((1, w), lambda bi, c: (0, 0)),
                  pl.BlockSpec((w, w), lambda bi, c: (0, 0))],
        out_specs=pl.BlockSpec((1, CHUNK, w), lambda bi, c: (bi, c, 0)),
        out_shape=jax.ShapeDtypeStruct((b, nc * CHUNK, w), F32),
        compiler_params=_cparams("parallel", "parallel"),
        name="mlstm_out",
    )(qk, m, gcol, grow, bias_row, bias_col, cf, nmf, cb, nmb, norm_w, gmat)


def _proj_out_kernel(att_ref, ret_ref, ml_ref, h_ref, mod_ref, w_ref, o_ref):
    aw, rw = att_ref.shape[2], ret_ref.shape[2]
    y = jnp.dot(att_ref[0].astype(BF16), w_ref[0:aw, :], preferred_element_type=F32)
    y = y + jnp.dot(ret_ref[0].astype(BF16), w_ref[aw:aw + rw, :], preferred_element_type=F32)
    y = y + jnp.dot(ml_ref[0].astype(BF16), w_ref[aw + rw:, :], preferred_element_type=F32)
    o_ref[0] = h_ref[0] + mod_ref[0, 2:3, :] * y


def _mod_map(n_ctx_tiles, ctx_row, skip):
    return lambda bi, i: (jnp.where(i + skip < n_ctx_tiles, ctx_row, bi), 0, 0)


def proj_out(att, ret, ml, h, mod, w_out, *, n_ctx, skip):
    b, s, d = h.shape
    tm = TOKEN_TILE
    nt = s // tm - skip
    tok = lambda bi, i: (bi, i, 0)
    return pl.pallas_call(
        _proj_out_kernel,
        grid=(b, nt),
        in_specs=[pl.BlockSpec((1, tm, att.shape[2]), tok),
                  pl.BlockSpec((1, tm, ret.shape[2]), tok),
                  pl.BlockSpec((1, tm, ml.shape[2]), tok),
                  pl.BlockSpec((1, tm, d), lambda bi, i: (bi, i + skip, 0)),
                  pl.BlockSpec((1, 6, d), _mod_map(n_ctx // tm, mod.shape[0] - 1, skip)),
                  pl.BlockSpec(w_out.shape, lambda bi, i: (0, 0))],
        out_specs=pl.BlockSpec((1, tm, d), tok),
        out_shape=jax.ShapeDtypeStruct((b, nt * tm, d), F32),
        compiler_params=_cparams("parallel", "parallel"),
        name="proj_out",
    )(att, ret, ml, h, mod, w_out)


def _ffn_kernel(h_ref, mod_ref, nw_ref, wg_ref, wu_ref, wd_ref, o_ref):
    x = h_ref[0]
    a = _norm_mod(x, nw_ref[...], mod_ref[0, 3:4, :], mod_ref[0, 4:5, :]).astype(BF16)
    hg = jnp.dot(a, wg_ref[...], preferred_element_type=F32)
    hu = jnp.dot(a, wu_ref[...], preferred_element_type=F32)
    act = (_silu(hg) * hu).astype(BF16)
    y = jnp.dot(act, wd_ref[...], preferred_element_type=F32)
    o_ref[0] = x + mod_ref[0, 5:6, :] * y


def ffn_dense(h, mod, norm_w, wg, wu, wd, *, n_ctx):
    b, s, d = h.shape
    tm = TOKEN_TILE
    tok = lambda bi, i: (bi, i, 0)
    const = lambda bi, i: (0, 0)
    resident = functools.partial(pl.BlockSpec, pipeline_mode=pl.Buffered(1))
    return pl.pallas_call(
        _ffn_kernel,
        grid=(b, s // tm),
        in_specs=[pl.BlockSpec((1, tm, d), tok),
                  pl.BlockSpec((1, 6, d), _mod_map(n_ctx // tm, mod.shape[0] - 1, 0)),
                  pl.BlockSpec((1, d), const),
                  resident(wg.shape, const),
                  resident(wu.shape, const),
                  resident(wd.shape, const)],
        out_specs=pl.BlockSpec((1, tm, d), tok),
        out_shape=jax.ShapeDtypeStruct((b, s, d), F32),
        compiler_params=_cparams("parallel", "parallel"),
        name="ffn_dense",
    )(h, mod, norm_w, wg, wu, wd)


def _moe_pre_kernel(h_ref, mod_ref, nw_ref, rw_ref, rb_ref, xn_ref, route_ref, *, n_experts):
    x = h_ref[0]
    a = _norm_mod(x, nw_ref[...], mod_ref[0, 3:4, :], mod_ref[0, 4:5, :])
    a_hi = a.astype(BF16)
    xn_ref[0] = a_hi
    a_lo = (a - a_hi.astype(F32)).astype(BF16)
    rw = rw_ref[...]
    w_hi = rw.astype(BF16)
    w_lo = (rw - w_hi.astype(F32)).astype(BF16)
    logits = (jnp.dot(a_hi, w_hi, preferred_element_type=F32) + jnp.dot(a_lo, w_hi, preferred_element_type=F32)
              + jnp.dot(a_hi, w_lo, preferred_element_type=F32)) + rb_ref[...]
    lane = lax.broadcasted_iota(jnp.int32, logits.shape, 1)
    l1 = jnp.where(lane < n_experts, logits, -jnp.inf)
    v1 = jnp.max(l1, axis=-1, keepdims=True)
    i1 = jnp.min(jnp.where(l1 == v1, lane, 128), axis=-1, keepdims=True)
    l2 = jnp.where(lane == i1, -jnp.inf, l1)
    v2 = jnp.max(l2, axis=-1, keepdims=True)
    i2 = jnp.min(jnp.where(l2 == v2, lane, 128), axis=-1, keepdims=True)
    e2 = jnp.exp(v2 - v1)
    w1 = 1.0 / (1.0 + e2)
    w2 = e2 * w1
    lane8 = lax.broadcasted_iota(jnp.int32, route_ref.shape[1:], 1)
    route = jnp.where(lane8 == 0, i1.astype(F32),
                      jnp.where(lane8 == 1, i2.astype(F32),
                                jnp.where(lane8 == 2, w1, jnp.where(lane8 == 3, w2, 0.0))))
    route_ref[0] = route


def moe_pre(h, mod, norm_w, router_w_pad, router_b_pad, *, n_experts):
    b, t, d = h.shape
    tm = TOKEN_TILE
    tok = lambda bi, i: (bi, i, 0)
    const = lambda bi, i: (0, 0)
    return pl.pallas_call(
        functools.partial(_moe_pre_kernel, n_experts=n_experts),
        grid=(b, t // tm),
        in_specs=[pl.BlockSpec((1, tm, d), tok),
                  pl.BlockSpec((1, 6, d), lambda bi, i: (bi, 0, 0)),
                  pl.BlockSpec((1, d), const),
                  pl.BlockSpec(router_w_pad.shape, const),
                  pl.BlockSpec((1, 128), const)],
        out_specs=(pl.BlockSpec((1, tm, d), tok), pl.BlockSpec((1, tm, 8), tok)),
        out_shape=(jax.ShapeDtypeStruct((b, t, d), BF16), jax.ShapeDtypeStruct((b, t, 8), F32)),
        compiler_params=_cparams("parallel", "parallel"),
        name="moe_pre",
    )(h, mod, norm_w, router_w_pad, router_b_pad)


def _moe_gmm_kernel(be_ref, nu_ref, x_ref, wg_ref, wu_ref, wd_ref, o_ref, acc_sc):
    i, j = pl.program_id(0), pl.program_id(1)
    nj = pl.num_programs(1)
    used = i < nu_ref[0]

    @pl.when(used)
    def _():
        x = x_ref[...]
        hg = jnp.dot(x, wg_ref[0], preferred_element_type=F32)
        hu = jnp.dot(x, wu_ref[0], preferred_element_type=F32)
        y = jnp.dot((_silu(hg) * hu).astype(BF16), wd_ref[0], preferred_element_type=F32)

        @pl.when(j == 0)
        def _():
            acc_sc[...] = y

        @pl.when(j > 0)
        def _():
            acc_sc[...] += y

    @pl.when(j == nj - 1)
    def _():
        o_ref[...] = jnp.where(used, acc_sc[...], 0.0)


def moe_gmm(block_e, n_used, xb, wg, wu, wd):
    n_slots, d = xb.shape
    tm, tf = MOE_TILE, MOE_FF_TILE
    n_blocks = n_slots // tm
    f = wg.shape[2]
    nj = f // tf
    last = nj - 1

    def jj(i, j, nu):
        return jnp.where(i < nu[0], j, last)

    grid_spec = pltpu.PrefetchScalarGridSpec(
        num_scalar_prefetch=2,
        grid=(n_blocks, nj),
        in_specs=[pl.BlockSpec((tm, d), lambda i, j, be, nu: (i, 0)),
                  pl.BlockSpec((1, d, tf), lambda i, j, be, nu: (be[i], 0, jj(i, j, nu))),
                  pl.BlockSpec((1, d, tf), lambda i, j, be, nu: (be[i], 0, jj(i, j, nu))),
                  pl.BlockSpec((1, tf, d), lambda i, j, be, nu: (be[i], jj(i, j, nu), 0))],
        out_specs=pl.BlockSpec((tm, d), lambda i, j, be, nu: (i, 0)),
        scratch_shapes=[pltpu.VMEM((tm, d), F32)],
    )
    return pl.pallas_call(
        _moe_gmm_kernel,
        grid_spec=grid_spec,
        out_shape=jax.ShapeDtypeStruct((n_slots, d), F32),
        compiler_params=_cparams("arbitrary", "arbitrary"),
        name="moe_gmm",
    )(block_e, n_used, xb, wg, wu, wd)


def _moe_combine_kernel(h_ref, mod_ref, route_ref, y1_ref, y2_ref, o_ref):
    r = route_ref[0]
    y = r[:, 2:3] * y1_ref[0] + r[:, 3:4] * y2_ref[0]
    o_ref[0] = h_ref[0] + mod_ref[0, 5:6, :] * y


def moe_combine(h, mod, route, y1, y2):
    b, t, d = h.shape
    tm = TOKEN_TILE
    tok = lambda bi, i: (bi, i, 0)
    return pl.pallas_call(
        _moe_combine_kernel,
        grid=(b, t // tm),
        in_specs=[pl.BlockSpec((1, tm, d), tok),
                  pl.BlockSpec((1, 6, d), lambda bi, i: (bi, 0, 0)),
                  pl.BlockSpec((1, tm, 8), tok),
                  pl.BlockSpec((1, tm, d), tok),
                  pl.BlockSpec((1, tm, d), tok)],
        out_specs=pl.BlockSpec((1, tm, d), tok),
        out_shape=jax.ShapeDtypeStruct((b, t, d), F32),
        compiler_params=_cparams("parallel", "parallel"),
        name="moe_combine",
    )(h, mod, route, y1, y2)


def _moe_dispatch_plan(expert_idx, n_experts):
    n = expert_idx.shape[0]
    tile = MOE_TILE
    flat_e = expert_idx.reshape(-1)
    onehot = (flat_e[None, :] == jnp.arange(n_experts, dtype=jnp.int32)[:, None]).astype(jnp.int32)
    csum = jnp.cumsum(onehot, axis=1)
    counts = csum[:, -1]
    padded = (counts + tile - 1) // tile * tile
    pad_ends = jnp.cumsum(padded)
    pad_starts = pad_ends - padded
    slot = jnp.sum(onehot * (pad_starts[:, None] + csum - 1), axis=0)
    n_blocks = -(-(n * TOP_K) // tile) + n_experts
    n_used = pad_ends[-1] // tile
    blk = jnp.arange(n_blocks, dtype=jnp.int32)
    block_e = jnp.minimum(jnp.searchsorted(pad_ends, blk * tile, side="right"), n_experts - 1).astype(jnp.int32)
    block_e = jnp.where(blk < n_used, block_e, block_e[jnp.maximum(n_used - 1, 0)])
    slot_tok = jnp.zeros((n_blocks * tile,), jnp.int32).at[slot].set(jnp.arange(n * TOP_K, dtype=jnp.int32) // TOP_K)
    return slot.reshape(n, TOP_K), slot_tok, block_e, n_used.astype(jnp.int32).reshape(1)


def _rope_tables(n_ctx, n_lat):
    rows = n_lat // GRID_W
    row = jnp.repeat(jnp.arange(rows, dtype=F32), GRID_W)
    col = jnp.tile(jnp.arange(GRID_W, dtype=F32), rows)
    n_freq = HEAD_DIM // 4
    inv_freq = ROPE_THETA ** (-jnp.arange(n_freq, dtype=F32) / n_freq)
    ang = jnp.concatenate([row[:, None] * inv_freq, col[:, None] * inv_freq], -1)
    cos, sin = jnp.cos(ang), jnp.sin(ang)
    cos_t = jnp.concatenate([cos, cos, cos, cos], axis=1)
    sin_t = jnp.concatenate([-sin, sin, -sin, sin], axis=1)
    cos_t = jnp.concatenate([jnp.ones((n_ctx, 128), F32), cos_t], axis=0)
    sin_t = jnp.concatenate([jnp.zeros((n_ctx, 128), F32), sin_t], axis=0)
    return cos_t, sin_t


def _block_ones(n):
    idx = jnp.arange(n) // HEAD_DIM
    return (idx[:, None] == idx[None, :]).astype(BF16)


def kernel(x, c, ctx, c_ctx, mod_w, mod_b, norm1_w, norm2_w, w_in, attn_qn_w, attn_kn_w, ret_decay, ret_norm_w,
           mlstm_conv_w, mlstm_gate_b, mlstm_norm_w, w_out, ffn_w_gate, ffn_w_up, ffn_w_down, router_w, router_b,
           moe_w_gate, moe_w_up, moe_w_down):
    b, n_lat, d = x.shape
    n_ctx = ctx.shape[1]
    depth = mod_w.shape[0]
    n_experts = router_w.shape[2]
    ret_heads = ret_decay.shape[2]
    ml_heads = mlstm_gate_b.shape[2]
    ret_w, ml_w = ret_heads * HEAD_DIM, ml_heads * HEAD_DIM
    n_gate = N_GATES * ml_heads
    kv_w = (w_in.shape[2] - 4 * ret_w - 4 * ml_w - n_gate) // 6
    att_w = 4 * kv_w
    dims = (att_w, kv_w, ret_w, ml_w, n_gate)
    assert n_ctx % TOKEN_TILE == 0 and n_lat % KV_TILE == 0 and b + 1 <= 16
    n_main = w_in.shape[2] - n_gate
    n_pad = -(-w_in.shape[2] // 128) * 128

    cos_t, sin_t = _rope_tables(n_ctx, n_lat)
    gq, gk, g_ret, g_ml = _block_ones(att_w), _block_ones(kv_w), _block_ones(ret_w), _block_ones(ml_w)
    cond = jnp.concatenate([c, c_ctx[None, :], jnp.zeros((16 - b - 1, d), F32)], axis=0)
    ctx_tiles = n_ctx // TOKEN_TILE
    ctx_chunks = n_ctx // CHUNK

    h = jnp.concatenate([ctx, x], axis=1)
    for layer in range(depth):
        last = layer == depth - 1
        skip = ctx_tiles if last else 0
        mod = adaln(cond, mod_w[layer], mod_b[layer]).reshape(16, 6, d)[:b + 1]
        w_all = jnp.pad(w_in[layer].astype(BF16), ((0, 0), (0, n_pad - w_in.shape[2])))
        w_gate_t = w_in[layer][:, n_main:].T.astype(BF16)
        qn = jnp.tile(attn_qn_w[layer], att_w // HEAD_DIM)[None, :]
        kn = jnp.tile(attn_kn_w[layer], kv_w // HEAD_DIM)[None, :]
        k_norm_max = math.sqrt(HEAD_DIM) * jnp.max(jnp.abs(attn_kn_w[layer])) * (1.0 + 2.0 ** -6)
        q_norm_max = math.sqrt(HEAD_DIM) * jnp.max(jnp.abs(attn_qn_w[layer])) * (QK_SCALE * LOG2E)
        fast = (2.0 * q_norm_max * k_norm_max <= MAX_SCORE_SPAN_LOG2).astype(jnp.int32).reshape(1)
        kbound = jnp.full((1, att_w), k_norm_max, F32)
        q, k, v, r, m, gcol, grow = proj_in(h, mod, norm1_w[layer][None, :], cos_t, sin_t, w_all, w_gate_t, qn, kn,
                                            kbound, gq, gk, n_ctx=n_ctx, dims=dims)
        att = attention(fast, q, k, v, n_ctx=n_ctx, skip=skip)
        dec = jnp.repeat(ret_decay[layer], HEAD_DIM, axis=1)
        sf, sb = ret_states(r, dec, n_ctx=n_ctx)
        ret = ret_out(r, sf, sb, dec, ret_norm_w[layer][None, :], g_ret, skip_chunks=ctx_chunks if last else 0)
        qk = mlstm_conv(m, mlstm_conv_w[layer], n_ctx=n_ctx)
        bias = mlstm_gate_b[layer].reshape(-1)
        cf, nmf, cb, nmb = mlstm_states(qk, m, gcol, bias[None, :], n_ctx=n_ctx, n_heads=ml_heads)
        ml = mlstm_out(qk, m, gcol, grow, bias[None, :], bias[:, None], cf, nmf, cb, nmb, mlstm_norm_w[layer][None, :],
                       g_ml, skip_chunks=ctx_chunks if last else 0, n_heads=ml_heads)
        h = proj_out(att, ret, ml, h, mod, w_out[layer].astype(BF16), n_ctx=n_ctx, skip=skip)
        if layer % 2 == 0:
            li = layer // 2
            assert not last, "dense FFN on a latent-only stream is not wired up"
            h = ffn_dense(h, mod, norm2_w[layer][None, :], ffn_w_gate[li].astype(BF16), ffn_w_up[li].astype(BF16),
                          ffn_w_down[li].astype(BF16), n_ctx=n_ctx)
        else:
            li = layer // 2
            assert last, "expert FFN on the combined stream is not wired up"
            mod_lat = mod[:b]
            rw = jnp.pad(router_w[li], ((0, 0), (0, 128 - n_experts)))
            rb = jnp.pad(router_b[li], (0, 128 - n_experts))[None, :]
            xn, route = moe_pre(h, mod_lat, norm2_w[layer][None, :], rw, rb, n_experts=n_experts)
            t = h.shape[1]
            route_flat = route.reshape(b * t, 8)
            slot, slot_tok, block_e, n_used = _moe_dispatch_plan(route_flat[:, 0:TOP_K].astype(jnp.int32), n_experts)
            xb = jnp.take(xn.reshape(b * t, d), slot_tok, axis=0)
            yb = moe_gmm(block_e, n_used, xb, moe_w_gate[li].astype(BF16), moe_w_up[li].astype(BF16),
                         moe_w_down[li].astype(BF16))
            y1 = jnp.take(yb, slot[:, 0], axis=0).reshape(b, t, d)
            y2 = jnp.take(yb, slot[:, 1], axis=0).reshape(b, t, d)
            h = moe_combine(h, mod_lat, route, y1, y2)
    return h if h.shape[1] == n_lat else h[:, n_ctx:, :]
```

```python
import functools
import math

import jax
import jax.numpy as jnp
from jax import lax
from jax.experimental import pallas as pl
from jax.experimental.pallas import tpu as pltpu

F32 = jnp.float32
BF16 = jnp.bfloat16

HEAD_DIM = 64
GRID_W = 64
ROPE_THETA = 10000.0
EPS = 1e-6
CHUNK = 128
TOKEN_TILE = 256
KV_TILE = 1024
N_GATES = 4
TOP_K = 2
MOE_TILE = 512
MOE_FF_TILE = 1792
VMEM_LIMIT = 56 * 1024 * 1024
QK_SCALE = HEAD_DIM ** -0.5
LOG2E = math.log2(math.e)
MAX_SCORE_SPAN_LOG2 = 100.0


def _cparams(*sem):
    return pltpu.CompilerParams(dimension_semantics=sem, vmem_limit_bytes=VMEM_LIMIT)


def _silu(x):
    return x * jax.nn.sigmoid(x)


def _split_dot(x, ones_bf16, parts=2):
    acc = None
    r = x
    for _ in range(parts):
        hi = r.astype(BF16)
        d = jnp.dot(hi, ones_bf16, preferred_element_type=F32)
        acc = d if acc is None else acc + d
        r = r - hi.astype(F32)
    return acc


def _dot_nt(a, b):
    return lax.dot_general(a, b, (((1,), (1,)), ((), ())), preferred_element_type=F32)


def _rope(x, cos_t, sin_t):
    w = x.shape[1]
    reps = w // 128
    c = jnp.concatenate([cos_t] * reps, axis=1) if reps > 1 else cos_t
    s = jnp.concatenate([sin_t] * reps, axis=1) if reps > 1 else sin_t
    lane = lax.broadcasted_iota(jnp.int32, x.shape, 1)
    first_half = (lane % HEAD_DIM) < (HEAD_DIM // 2)
    swapped = jnp.where(first_half, pltpu.roll(x, w - HEAD_DIM // 2, 1), pltpu.roll(x, HEAD_DIM // 2, 1))
    return x * c + swapped * s


def _group_rms(x, gmat, gain):
    ms = _split_dot(x * x, gmat) * (1.0 / HEAD_DIM)
    return x * lax.rsqrt(ms + EPS) * gain


def _group_layer_norm(x, gmat, gain):
    mu = _split_dot(x, gmat) * (1.0 / HEAD_DIM)
    xc = x - mu
    var = _split_dot(xc * xc, gmat) * (1.0 / HEAD_DIM)
    return xc * lax.rsqrt(var + EPS) * gain


def _expand_heads(x, n_heads, lane):
    out = x[:, n_heads - 1:n_heads]
    for h in range(n_heads - 2, -1, -1):
        out = jnp.where(lane < (h + 1) * HEAD_DIM, x[:, h:h + 1], out)
    return out


def _block_diag_mask(n):
    r = lax.broadcasted_iota(jnp.int32, (n, n), 0) // HEAD_DIM
    c = lax.broadcasted_iota(jnp.int32, (n, n), 1) // HEAD_DIM
    return r == c


def _adaln_kernel(c_ref, w_ref, b_ref, o_ref):
    a = _silu(c_ref[...]).astype(BF16)
    o_ref[...] = jnp.dot(a, w_ref[...].astype(BF16), preferred_element_type=F32) + b_ref[...]


def adaln(cond, w, b):
    r, d = cond.shape
    n = w.shape[1]
    tn = 1536 if n % 1536 == 0 else n
    return pl.pallas_call(
        _adaln_kernel,
        grid=(n // tn,),
        in_specs=[pl.BlockSpec((r, d), lambda j: (0, 0)),
                  pl.BlockSpec((d, tn), lambda j: (0, j)),
                  pl.BlockSpec((1, tn), lambda j: (0, j))],
        out_specs=pl.BlockSpec((r, tn), lambda j: (0, j)),
        out_shape=jax.ShapeDtypeStruct((r, n), F32),
        compiler_params=_cparams("arbitrary"),
        name="adaln",
    )(cond, w, b.reshape(1, n))


def _norm_mod(x, nw, shift, scale):
    ms = jnp.mean(x * x, axis=-1, keepdims=True)
    y = x * lax.rsqrt(ms + EPS) * nw
    return y * (1.0 + scale) + shift


def _proj_in_kernel(h_ref, mod_ref, nw_ref, cos_ref, sin_ref, w_ref, wgt_ref, qn_ref, kn_ref, kb_ref, gq_ref, gk_ref,
                    q_ref, k_ref, v_ref, r_ref, m_ref, gc_ref, gr_ref, *, dims):
    att_w, kv_w, ret_w, ml_w, n_gate = dims
    x = h_ref[0]
    a = _norm_mod(x, nw_ref[...], mod_ref[0, 0:1, :], mod_ref[0, 1:2, :]).astype(BF16)
    p = jnp.dot(a, w_ref[...], preferred_element_type=F32)
    cos_t, sin_t = cos_ref[...], sin_ref[...]
    o = 0
    aq = p[:, o:o + att_w]; o += att_w
    ak = p[:, o:o + kv_w]; o += kv_w
    av = p[:, o:o + kv_w]; o += kv_w
    aq = _rope(_group_rms(aq, gq_ref[...], qn_ref[...]), cos_t, sin_t) * (QK_SCALE * LOG2E)
    ak = _rope(_group_rms(ak, gk_ref[...], kn_ref[...]), cos_t, sin_t)
    neg_bound = -jnp.sqrt(_split_dot(aq * aq, gq_ref[...])) * kb_ref[...]
    tm = x.shape[0]
    lane = lax.broadcasted_iota(jnp.int32, (tm, 2 * HEAD_DIM), 1)

    def widen(a, extra):
        return jnp.where(lane == HEAD_DIM, extra, jnp.concatenate([a, jnp.zeros_like(a)], axis=1)).astype(BF16)

    for hh in range(att_w // HEAD_DIM):
        cols = slice(hh * HEAD_DIM, (hh + 1) * HEAD_DIM)
        q_ref[0, hh] = widen(aq[:, cols], neg_bound[:, hh * HEAD_DIM:hh * HEAD_DIM + 1])
    for hh in range(kv_w // HEAD_DIM):
        cols = slice(hh * HEAD_DIM, (hh + 1) * HEAD_DIM)
        k_ref[0, hh] = widen(ak[:, cols], 1.0)
        v_ref[0, hh] = widen(av[:, cols], 1.0)
    rq = _rope(p[:, o:o + ret_w], cos_t, sin_t)
    rk = _rope(p[:, o + ret_w:o + 2 * ret_w] * QK_SCALE, cos_t, sin_t)
    r_ref[0, :, 0:ret_w] = rq
    r_ref[0, :, ret_w:2 * ret_w] = rk
    r_ref[0, :, 2 * ret_w:4 * ret_w] = p[:, o + 2 * ret_w:o + 4 * ret_w]
    o += 4 * ret_w
    m_ref[0] = p[:, o:o + 4 * ml_w]
    o += 4 * ml_w
    gc_ref[0] = p[:, o:o + n_gate]
    gr_ref[0] = _dot_nt(wgt_ref[...], a)


def proj_in(h, mod, norm_w, cos_t, sin_t, w_all, w_gate_t, qn, kn, kbound, gq, gk, *, n_ctx, dims):
    att_w, kv_w, ret_w, ml_w, n_gate = dims
    b, s, d = h.shape
    tm = TOKEN_TILE
    n_ctx_tiles = n_ctx // tm
    n_pad = w_all.shape[1]
    ctx_row = mod.shape[0] - 1

    def mod_map(bi, i):
        return (jnp.where(i < n_ctx_tiles, ctx_row, bi), 0, 0)

    const2 = lambda bi, i: (0, 0)
    out_shape = (
        jax.ShapeDtypeStruct((b, att_w // HEAD_DIM, s, 2 * HEAD_DIM), BF16),
        jax.ShapeDtypeStruct((b, kv_w // HEAD_DIM, s, 2 * HEAD_DIM), BF16),
        jax.ShapeDtypeStruct((b, kv_w // HEAD_DIM, s, 2 * HEAD_DIM), BF16),
        jax.ShapeDtypeStruct((b, s, 4 * ret_w), F32),
        jax.ShapeDtypeStruct((b, s, 4 * ml_w), F32),
        jax.ShapeDtypeStruct((b, s, n_gate), F32),
        jax.ShapeDtypeStruct((b, n_gate, s), F32),
    )
    return pl.pallas_call(
        functools.partial(_proj_in_kernel, dims=dims),
        grid=(b, s // tm),
        in_specs=[
            pl.BlockSpec((1, tm, d), lambda bi, i: (bi, i, 0)),
            pl.BlockSpec((1, 6, d), mod_map),
            pl.BlockSpec((1, d), const2),
            pl.BlockSpec((tm, 128), lambda bi, i: (i, 0)),
            pl.BlockSpec((tm, 128), lambda bi, i: (i, 0)),
            pl.BlockSpec((d, n_pad), const2),
            pl.BlockSpec((n_gate, d), const2),
            pl.BlockSpec((1, att_w), const2),
            pl.BlockSpec((1, kv_w), const2),
            pl.BlockSpec((1, att_w), const2),
            pl.BlockSpec((att_w, att_w), const2),
            pl.BlockSpec((kv_w, kv_w), const2),
        ],
        out_specs=(
            pl.BlockSpec((1, att_w // HEAD_DIM, tm, 2 * HEAD_DIM), lambda bi, i: (bi, 0, i, 0)),
            pl.BlockSpec((1, kv_w // HEAD_DIM, tm, 2 * HEAD_DIM), lambda bi, i: (bi, 0, i, 0)),
            pl.BlockSpec((1, kv_w // HEAD_DIM, tm, 2 * HEAD_DIM), lambda bi, i: (bi, 0, i, 0)),
            pl.BlockSpec((1, tm, 4 * ret_w), lambda bi, i: (bi, i, 0)),
            pl.BlockSpec((1, tm, 4 * ml_w), lambda bi, i: (bi, i, 0)),
            pl.BlockSpec((1, tm, n_gate), lambda bi, i: (bi, i, 0)),
            pl.BlockSpec((1, n_gate, tm), lambda bi, i: (bi, 0, i)),
        ),
        out_shape=out_shape,
        compiler_params=_cparams("parallel", "parallel"),
        name="proj_in",
    )(h, mod, norm_w, cos_t, sin_t, w_all, w_gate_t, qn, kn, kbound, gq, gk)


def _conv_kernel(x_ref, prev_ref, next_ref, w_ref, o_ref, *, n_ctx, seq, half):
    i = pl.program_id(1)
    x = x_ref[0]
    tm, w = x.shape
    row = lax.broadcasted_iota(jnp.int32, (tm, 1), 0)
    grow = row + i * tm
    xp = jnp.where(row == 0, prev_ref[0, 7:8, :], pltpu.roll(x, 1, 0))
    xn = jnp.where(row == tm - 1, next_ref[0, 0:1, :], pltpu.roll(x, tm - 1, 0))
    xp = jnp.where((grow == 0) | (grow == n_ctx), 0.0, xp)
    xn = jnp.where((grow == n_ctx - 1) | (grow == seq - 1), 0.0, xn)
    y = _silu(xp * w_ref[0:1, :] + x * w_ref[1:2, :] + xn * w_ref[2:3, :])
    lane = lax.broadcasted_iota(jnp.int32, (tm, w), 1)
    o_ref[0] = jnp.where(lane >= half, y * QK_SCALE, y)


def mlstm_conv(m, conv_w, *, n_ctx):
    b, s, _ = m.shape
    w = conv_w.shape[1]
    tm = TOKEN_TILE
    r8 = tm // 8
    last8 = s // 8 - 1
    return pl.pallas_call(
        functools.partial(_conv_kernel, n_ctx=n_ctx, seq=s, half=w // 2),
        grid=(b, s // tm),
        in_specs=[pl.BlockSpec((1, tm, w), lambda bi, i: (bi, i, 0)),
                  pl.BlockSpec((1, 8, w), lambda bi, i: (bi, jnp.maximum(i * r8 - 1, 0), 0)),
                  pl.BlockSpec((1, 8, w), lambda bi, i: (bi, jnp.minimum((i + 1) * r8, last8), 0)),
                  pl.BlockSpec((3, w), lambda bi, i: (0, 0))],
        out_specs=pl.BlockSpec((1, tm, w), lambda bi, i: (bi, i, 0)),
        out_shape=jax.ShapeDtypeStruct((b, s, w), F32),
        compiler_params=_cparams("parallel", "parallel"),
        name="mlstm_conv",
    )(m, m, m, conv_w)


def _attn_kernel(fast_ref, q_ref, k_ref, v_ref, o_ref, m_sc, acc_sc, *, n_ctx, n_lat_tiles, skip):
    i = pl.program_id(2) + skip
    g, tq = q_ref.shape[1], q_ref.shape[2]
    hd = HEAD_DIM

    def fast_step(k, v):
        for h in range(g):
            rows = slice(h * tq, (h + 1) * tq)
            p = jnp.exp2(_dot_nt(q_ref[0, h], k))
            acc_sc[rows] += jnp.dot(p.astype(BF16), v, preferred_element_type=F32)

    def safe_step(k, v):
        for h in range(g):
            rows = slice(h * tq, (h + 1) * tq)
            s = _dot_nt(q_ref[0, h], k)
            m_old = m_sc[rows]
            m_new = jnp.maximum(m_old, jnp.max(s, axis=-1, keepdims=True))
            p = jnp.exp2(s - jnp.concatenate([m_new] * (s.shape[1] // 128), axis=1))
            acc_sc[rows] = jnp.exp2(m_old - m_new) * acc_sc[rows] + jnp.dot(p.astype(BF16), v, preferred_element_type=F32)
            m_sc[rows] = m_new

    def sweep(step, unroll):
        step(k_ref[0, 0, 0:n_ctx, :], v_ref[0, 0, 0:n_ctx, :])

        @pl.when(i * tq >= n_ctx)
        def _():
            def body(j, carry):
                start = pl.multiple_of(n_ctx + j * KV_TILE, math.gcd(n_ctx, KV_TILE))
                step(k_ref[0, 0, pl.ds(start, KV_TILE), :], v_ref[0, 0, pl.ds(start, KV_TILE), :])
                return carry
            lax.fori_loop(0, n_lat_tiles, body, 0, unroll=unroll)

    acc_sc[...] = jnp.zeros(acc_sc.shape, F32)

    @pl.when(fast_ref[0] == 1)
    def _():
        sweep(fast_step, 2)

    @pl.when(fast_ref[0] != 1)
    def _():
        m_sc[...] = jnp.full(m_sc.shape, -jnp.inf, F32)
        sweep(safe_step, 1)

    acc = acc_sc[...]
    out = acc[:, 0:hd] / acc[:, hd:hd + 1]
    o_ref[0] = jnp.concatenate([out[h * tq:(h + 1) * tq] for h in range(g)], axis=1)


def attention(fast, q, k, v, *, n_ctx, skip):
    b, hq, s, lanes = q.shape
    hkv = k.shape[1]
    g = hq // hkv
    tq = TOKEN_TILE
    n_lat = s - n_ctx
    assert n_lat % KV_TILE == 0 and n_ctx % tq == 0 and lanes == 2 * HEAD_DIM
    nq = s // tq - skip
    grid_spec = pltpu.PrefetchScalarGridSpec(
        num_scalar_prefetch=1,
        grid=(b, hkv, nq),
        in_specs=[pl.BlockSpec((1, g, tq, lanes), lambda bi, gi, i, f: (bi, gi, i + skip, 0)),
                  pl.BlockSpec((1, 1, s, lanes), lambda bi, gi, i, f: (bi, gi, 0, 0)),
                  pl.BlockSpec((1, 1, s, lanes), lambda bi, gi, i, f: (bi, gi, 0, 0))],
        out_specs=pl.BlockSpec((1, tq, g * HEAD_DIM), lambda bi, gi, i, f: (bi, i, gi)),
        scratch_shapes=[pltpu.VMEM((g * tq, 128), F32), pltpu.VMEM((g * tq, lanes), F32)],
    )
    return pl.pallas_call(
        functools.partial(_attn_kernel, n_ctx=n_ctx, n_lat_tiles=n_lat // KV_TILE, skip=skip),
        grid_spec=grid_spec,
        out_shape=jax.ShapeDtypeStruct((b, nq * tq, hq * HEAD_DIM), F32),
        compiler_params=_cparams("parallel", "parallel", "arbitrary"),
        name="attention",
    )(fast, q, k, v)


def _rev_chunk(c, nc_ctx, nc):
    return jnp.where(c < nc_ctx, nc_ctx - 1 - c, nc - 1 + nc_ctx - c)


def _ret_state_kernel(kf_ref, vf_ref, kb_ref, vb_ref, dec_ref, sf_ref, sb_ref, s_sc):
    c = pl.program_id(1)

    @pl.when(c == 0)
    def _():
        s_sc[...] = jnp.zeros(s_sc.shape, F32)

    w = kf_ref.shape[2]
    lg = jnp.log1p(-jnp.exp(dec_ref[...]))
    pos = lax.broadcasted_iota(jnp.int32, (CHUNK, 1), 0).astype(F32)
    bd = _block_diag_mask(w)
    for d, (k_ref, v_ref, out_ref) in enumerate(((kf_ref, vf_ref, sf_ref), (kb_ref, vb_ref, sb_ref))):
        lgd = lg[d:d + 1, :]
        e = (CHUNK - 1.0 - pos) if d == 0 else pos
        kdec = k_ref[0] * jnp.exp(e * lgd)
        inc = jnp.dot(kdec.T.astype(BF16), v_ref[0].astype(BF16), preferred_element_type=F32)
        s_old = s_sc[d]
        out_ref[0, 0] = s_old.astype(BF16)
        s_sc[d] = jnp.exp(CHUNK * lgd) * s_old + jnp.where(bd, inc, 0.0)


def ret_states(r, dec, *, n_ctx):
    b, s, w4 = r.shape
    w = w4 // 4
    nc, nc_ctx = s // CHUNK, n_ctx // CHUNK
    rev = lambda c: _rev_chunk(c, nc_ctx, nc)
    st = jax.ShapeDtypeStruct((b, nc, w, w), BF16)
    return pl.pallas_call(
        _ret_state_kernel,
        grid=(b, nc),
        in_specs=[pl.BlockSpec((1, CHUNK, w), lambda bi, c: (bi, c, 1)),
                  pl.BlockSpec((1, CHUNK, w), lambda bi, c: (bi, c, 2)),
                  pl.BlockSpec((1, CHUNK, w), lambda bi, c: (bi, rev(c), 1)),
                  pl.BlockSpec((1, CHUNK, w), lambda bi, c: (bi, rev(c), 2)),
                  pl.BlockSpec((2, w), lambda bi, c: (0, 0))],
        out_specs=(pl.BlockSpec((1, 1, w, w), lambda bi, c: (bi, c, 0, 0)),
                   pl.BlockSpec((1, 1, w, w), lambda bi, c: (bi, rev(c), 0, 0))),
        out_shape=(st, st),
        scratch_shapes=[pltpu.VMEM((2, w, w), F32)],
        compiler_params=_cparams("parallel", "arbitrary"),
        name="ret_states",
    )(r, r, r, r, dec)


def _ret_out_kernel(r_ref, sf_ref, sb_ref, dec_ref, nw_ref, gmat_ref, o_ref):
    w = o_ref.shape[2]
    x = r_ref[0]
    q, k, v, g = x[:, 0:w], x[:, w:2 * w], x[:, 2 * w:3 * w], x[:, 3 * w:4 * w]
    lg = jnp.log1p(-jnp.exp(dec_ref[...]))
    pos = lax.broadcasted_iota(jnp.int32, (CHUNK, 1), 0).astype(F32)
    row = lax.broadcasted_iota(jnp.int32, (CHUNK, CHUNK), 0)
    col = lax.broadcasted_iota(jnp.int32, (CHUNK, CHUNK), 1)
    lane = lax.broadcasted_iota(jnp.int32, (CHUNK, w), 1)
    kb, vb = k.astype(BF16), v.astype(BF16)
    total = jnp.zeros((CHUNK, w), F32)
    for d, s_ref in enumerate((sf_ref, sb_ref)):
        lgd = lg[d:d + 1, :]
        qe = (pos + 1.0) if d == 0 else (CHUNK - pos)
        qdec = q * jnp.exp(qe * lgd)
        acc = jnp.dot(qdec.astype(BF16), s_ref[0, 0], preferred_element_type=F32)
        lag = ((row - col) if d == 0 else (col - row)).astype(F32)
        for h in range(w // HEAD_DIM):
            hm = (lane >= h * HEAD_DIM) & (lane < (h + 1) * HEAD_DIM)
            sc = _dot_nt(jnp.where(hm, q, 0.0).astype(BF16), kb)
            lgh = lgd[:, h * HEAD_DIM:h * HEAD_DIM + 1]
            decay = jnp.where(lag >= 0, jnp.exp(jnp.maximum(lag, 0.0) * lgh), 0.0)
            pv = jnp.dot((sc * decay).astype(BF16), vb, preferred_element_type=F32)
            acc = acc + jnp.where(hm, pv, 0.0)
        total = total + acc
    o_ref[0] = _group_layer_norm(total, gmat_ref[...], nw_ref[...]) * _silu(g)


def ret_out(r, sf, sb, dec, norm_w, gmat, *, skip_chunks):
    b, s, w4 = r.shape
    w = w4 // 4
    nc = s // CHUNK - skip_chunks
    return pl.pallas_call(
        _ret_out_kernel,
        grid=(b, nc),
        in_specs=[pl.BlockSpec((1, CHUNK, w4), lambda bi, c: (bi, c + skip_chunks, 0)),
                  pl.BlockSpec((1, 1, w, w), lambda bi, c: (bi, c + skip_chunks, 0, 0)),
                  pl.BlockSpec((1, 1, w, w), lambda bi, c: (bi, c + skip_chunks, 0, 0)),
                  pl.BlockSpec((2, w), lambda bi, c: (0, 0)),
                  pl.BlockSpec((1, w), lambda bi, c: (0, 0)),
                  pl.BlockSpec((w, w), lambda bi, c: (0, 0))],
        out_specs=pl.BlockSpec((1, CHUNK, w), lambda bi, c: (bi, c, 0)),
        out_shape=jax.ShapeDtypeStruct((b, nc * CHUNK, w), F32),
        compiler_params=_cparams("parallel", "parallel"),
        name="ret_out",
    )(r, sf, sb, dec, norm_w, gmat)


def _tri_masks():
    row = lax.broadcasted_iota(jnp.int32, (CHUNK, CHUNK), 0)
    col = lax.broadcasted_iota(jnp.int32, (CHUNK, CHUNK), 1)
    return col <= row, col >= row


def _ones_dot(ones_bf16, x, parts=3):
    acc = None
    r = x
    for _ in range(parts):
        hi = r.astype(BF16)
        d = jnp.dot(ones_bf16, hi, preferred_element_type=F32)
        acc = d if acc is None else acc + d
        r = r - hi.astype(F32)
    return acc


def _mlstm_state_kernel(kf_ref, vf_ref, gf_ref, kb_ref, vb_ref, gb_ref, bias_ref,
                        cf_ref, nmf_ref, cb_ref, nmb_ref, c_sc, nm_sc, *, n_heads):
    c = pl.program_id(1)

    @pl.when(c == 0)
    def _():
        c_sc[...] = jnp.zeros(c_sc.shape, F32)
        nm_sc[...] = jnp.zeros(nm_sc.shape, F32)

    w = kf_ref.shape[2]
    hh = n_heads
    tril, triu = _tri_masks()
    lane_t = lax.broadcasted_iota(jnp.int32, (CHUNK, w), 1)
    lane_1 = lax.broadcasted_iota(jnp.int32, (1, w), 1)
    bd = _block_diag_mask(w)
    dirs = ((kf_ref, vf_ref, gf_ref, cf_ref, nmf_ref, tril), (kb_ref, vb_ref, gb_ref, cb_ref, nmb_ref, triu))
    for d, (k_ref, v_ref, g_ref, c_out, nm_out, tmat) in enumerate(dirs):
        pre = g_ref[0] + bias_ref[...]
        ig = pre[:, d * hh:(d + 1) * hh]
        lf_all = jax.nn.log_sigmoid(pre)
        b_all = _ones_dot(tmat.astype(BF16), lf_all)
        lf = lf_all[:, (2 + d) * hh:(3 + d) * hh]
        bcum = b_all[:, (2 + d) * hh:(3 + d) * hh]
        b_end = jnp.sum(lf, axis=0, keepdims=True)
        a = b_end - bcum + ig
        m_loc = jnp.max(a, axis=0, keepdims=True)
        wgt = jnp.exp(a - m_loc)
        kw = k_ref[0] * _expand_heads(wgt, hh, lane_t)
        c_inc = jnp.dot(kw.T.astype(BF16), v_ref[0].astype(BF16), preferred_element_type=F32)
        n_inc = jnp.sum(kw, axis=0, keepdims=True)
        be = _expand_heads(b_end, hh, lane_1)
        ml = _expand_heads(m_loc, hh, lane_1)
        c_old = c_sc[d]
        n_old = nm_sc[d, 0:1, :]
        m_old = nm_sc[d, 1:2, :]
        c_out[0, 0] = c_old.astype(BF16)
        nm_out[0, 0] = nm_sc[d]
        m_new = jnp.maximum(be + m_old, ml)
        w_prev = jnp.exp(be + m_old - m_new)
        w_inc = jnp.exp(ml - m_new)
        c_sc[d] = w_prev * c_old + w_inc * jnp.where(bd, c_inc, 0.0)
        nm_sc[d, 0:1, :] = w_prev * n_old + w_inc * n_inc
        nm_sc[d, 1:2, :] = m_new


def mlstm_states(qk, m, gcol, bias_row, *, n_ctx, n_heads):
    b, s, w2 = qk.shape
    w = w2 // 2
    ng = gcol.shape[2]
    nc, nc_ctx = s // CHUNK, n_ctx // CHUNK
    rev = lambda c: _rev_chunk(c, nc_ctx, nc)
    cst = jax.ShapeDtypeStruct((b, nc, w, w), BF16)
    nmst = jax.ShapeDtypeStruct((b, nc, 8, w), F32)
    fwd3 = lambda j: (lambda bi, c: (bi, c, j))
    bwd3 = lambda j: (lambda bi, c: (bi, rev(c), j))
    return pl.pallas_call(
        functools.partial(_mlstm_state_kernel, n_heads=n_heads),
        grid=(b, nc),
        in_specs=[pl.BlockSpec((1, CHUNK, w), fwd3(1)),
                  pl.BlockSpec((1, CHUNK, w), fwd3(2)),
                  pl.BlockSpec((1, CHUNK, ng), fwd3(0)),
                  pl.BlockSpec((1, CHUNK, w), bwd3(1)),
                  pl.BlockSpec((1, CHUNK, w), bwd3(2)),
                  pl.BlockSpec((1, CHUNK, ng), bwd3(0)),
                  pl.BlockSpec((1, ng), lambda bi, c: (0, 0))],
        out_specs=(pl.BlockSpec((1, 1, w, w), lambda bi, c: (bi, c, 0, 0)),
                   pl.BlockSpec((1, 1, 8, w), lambda bi, c: (bi, c, 0, 0)),
                   pl.BlockSpec((1, 1, w, w), lambda bi, c: (bi, rev(c), 0, 0)),
                   pl.BlockSpec((1, 1, 8, w), lambda bi, c: (bi, rev(c), 0, 0))),
        out_shape=(cst, nmst, cst, nmst),
        scratch_shapes=[pltpu.VMEM((2, w, w), F32), pltpu.VMEM((2, 8, w), F32)],
        compiler_params=_cparams("parallel", "arbitrary"),
        name="mlstm_states",
    )(qk, m, gcol, qk, m, gcol, bias_row)


def _mlstm_out_kernel(qk_ref, m_ref, gc_ref, gr_ref, bc_ref, br_ref, cf_ref, nmf_ref, cb_ref, nmb_ref,
                      nw_ref, gmat_ref, o_ref, *, n_heads):
    w = o_ref.shape[2]
    hh = n_heads
    qk = qk_ref[0]
    q, k = qk[:, 0:w], qk[:, w:2 * w]
    mm = m_ref[0]
    v, og = mm[:, 2 * w:3 * w], mm[:, 3 * w:4 * w]
    pre_c = gc_ref[0] + bc_ref[...]
    pre_r = gr_ref[0] + br_ref[...]
    lf_c = jax.nn.log_sigmoid(pre_c)
    lf_r = jax.nn.log_sigmoid(pre_r)
    tril, triu = _tri_masks()
    lane = lax.broadcasted_iota(jnp.int32, (CHUNK, w), 1)
    kb, vb = k.astype(BF16), v.astype(BF16)
    gmat = gmat_ref[...]
    total = jnp.zeros((CHUNK, w), F32)
    for d, (c_ref, nm_ref) in enumerate(((cf_ref, nmf_ref), (cb_ref, nmb_ref))):
        tmat = tril if d == 0 else triu
        tmat_t = triu if d == 0 else tril
        b_c = _ones_dot(tmat.astype(BF16), lf_c)
        b_r = _split_dot(lf_r, tmat_t.astype(BF16), parts=3)
        nm = nm_ref[0, 0]
        n_row, m_row = nm[0:1, :], nm[1:2, :]
        cross = jnp.dot(q.astype(BF16), c_ref[0, 0], preferred_element_type=F32)
        qn = _split_dot(q * n_row, gmat)
        num = jnp.zeros((CHUNK, w), F32)
        den_e = jnp.zeros((CHUNK, w), F32)
        wp_e = jnp.zeros((CHUNK, w), F32)
        mt_e = jnp.zeros((CHUNK, w), F32)
        for h in range(hh):
            fc, ic = (2 + d) * hh + h, d * hh + h
            bcol = b_c[:, fc:fc + 1]
            logd = jnp.where(tmat, bcol - b_r[fc:fc + 1, :] + pre_r[ic:ic + 1, :], -jnp.inf)
            gg = bcol + m_row[:, h * HEAD_DIM:h * HEAD_DIM + 1]
            m_t = jnp.maximum(gg, jnp.max(logd, axis=-1, keepdims=True))
            hm = (lane >= h * HEAD_DIM) & (lane < (h + 1) * HEAD_DIM)
            sc = _dot_nt(jnp.where(hm, q, 0.0).astype(BF16), kb)
            p = sc * jnp.exp(logd - m_t)
            pv = jnp.dot(p.astype(BF16), vb, preferred_element_type=F32)
            num = num + jnp.where(hm, pv, 0.0)
            den_e = jnp.where(hm, jnp.sum(p, axis=-1, keepdims=True), den_e)
            wp_e = jnp.where(hm, jnp.exp(gg - m_t), wp_e)
            mt_e = jnp.where(hm, m_t, mt_e)
        num = num + wp_e * cross
        den = den_e + wp_e * qn
        total = total + num / jnp.maximum(jnp.abs(den), jnp.exp(-mt_e))
    o_ref[0] = _group_layer_norm(total * jax.nn.sigmoid(og), gmat, nw_ref[...])


def mlstm_out(qk, m, gcol, grow, bias_row, bias_col, cf, nmf, cb, nmb, norm_w, gmat, *, skip_chunks, n_heads):
    b, s, w2 = qk.shape
    w = w2 // 2
    ng = gcol.shape[2]
    nc = s // CHUNK - skip_chunks
    sk = skip_chunks
    return pl.pallas_call(
        functools.partial(_mlstm_out_kernel, n_heads=n_heads),
        grid=(b, nc),
        in_specs=[pl.BlockSpec((1, CHUNK, w2), lambda bi, c: (bi, c + sk, 0)),
                  pl.BlockSpec((1, CHUNK, 4 * w), lambda bi, c: (bi, c + sk, 0)),
                  pl.BlockSpec((1, CHUNK, ng), lambda bi, c: (bi, c + sk, 0)),
                  pl.BlockSpec((1, ng, CHUNK), lambda bi, c: (bi, 0, c + sk)),
                  pl.BlockSpec((1, ng), lambda bi, c: (0, 0)),
                  pl.BlockSpec((ng, 1), lambda bi, c: (0, 0)),
                  pl.BlockSpec((1, 1, w, w), lambda bi, c: (bi, c + sk, 0, 0)),
                  pl.BlockSpec((1, 1, 8, w), lambda bi, c: (bi, c + sk, 0, 0)),
                  pl.BlockSpec((1, 1, w, w), lambda bi, c: (bi, c + sk, 0, 0)),
                  pl.BlockSpec((1, 1, 8, w), lambda bi, c: (bi, c + sk, 0, 0)),
                  pl.BlockSpec((1, w), lambda bi, c: (0, 0)),
                  pl.BlockSpec((w, w), lambda bi, c: (0, 0))],
        out_specs=pl.BlockSpec((1, CHUNK, w), lambda bi, c: (bi, c, 0)),
        out_shape=jax.ShapeDtypeStruct((b, nc * CHUNK, w), F32),
        compiler_params=_cparams("parallel", "parallel"),
        name="mlstm_out",
    )(qk, m, gcol, grow, bias_row, bias_col, cf, nmf, cb, nmb, norm_w, gmat)


def _proj_out_kernel(att_ref, ret_ref, ml_ref, h_ref, mod_ref, w_ref, o_ref):
    aw, rw = att_ref.shape[2], ret_ref.shape[2]
    y = jnp.dot(att_ref[0].astype(BF16), w_ref[0:aw, :], preferred_element_type=F32)
    y = y + jnp.dot(ret_ref[0].astype(BF16), w_ref[aw:aw + rw, :], preferred_element_type=F32)
    y = y + jnp.dot(ml_ref[0].astype(BF16), w_ref[aw + rw:, :], preferred_element_type=F32)
    o_ref[0] = h_ref[0] + mod_ref[0, 2:3, :] * y


def _mod_map(n_ctx_tiles, ctx_row, skip):
    return lambda bi, i: (jnp.where(i + skip < n_ctx_tiles, ctx_row, bi), 0, 0)


def proj_out(att, ret, ml, h, mod, w_out, *, n_ctx, skip):
    b, s, d = h.shape
    tm = TOKEN_TILE
    nt = s // tm - skip
    tok = lambda bi, i: (bi, i, 0)
    return pl.pallas_call(
        _proj_out_kernel,
        grid=(b, nt),
        in_specs=[pl.BlockSpec((1, tm, att.shape[2]), tok),
                  pl.BlockSpec((1, tm, ret.shape[2]), tok),
                  pl.BlockSpec((1, tm, ml.shape[2]), tok),
                  pl.BlockSpec((1, tm, d), lambda bi, i: (bi, i + skip, 0)),
                  pl.BlockSpec((1, 6, d), _mod_map(n_ctx // tm, mod.shape[0] - 1, skip)),
                  pl.BlockSpec(w_out.shape, lambda bi, i: (0, 0))],
        out_specs=pl.BlockSpec((1, tm, d), tok),
        out_shape=jax.ShapeDtypeStruct((b, nt * tm, d), F32),
        compiler_params=_cparams("parallel", "parallel"),
        name="proj_out",
    )(att, ret, ml, h, mod, w_out)


def _ffn_kernel(h_ref, mod_ref, nw_ref, wg_ref, wu_ref, wd_ref, o_ref):
    x = h_ref[0]
    a = _norm_mod(x, nw_ref[...], mod_ref[0, 3:4, :], mod_ref[0, 4:5, :]).astype(BF16)
    hg = jnp.dot(a, wg_ref[...], preferred_element_type=F32)
    hu = jnp.dot(a, wu_ref[...], preferred_element_type=F32)
    act = (_silu(hg) * hu).astype(BF16)
    y = jnp.dot(act, wd_ref[...], preferred_element_type=F32)
    o_ref[0] = x + mod_ref[0, 5:6, :] * y


def ffn_dense(h, mod, norm_w, wg, wu, wd, *, n_ctx):
    b, s, d = h.shape
    tm = TOKEN_TILE
    tok = lambda bi, i: (bi, i, 0)
    const = lambda bi, i: (0, 0)
    resident = functools.partial(pl.BlockSpec, pipeline_mode=pl.Buffered(1))
    return pl.pallas_call(
        _ffn_kernel,
        grid=(b, s // tm),
        in_specs=[pl.BlockSpec((1, tm, d), tok),
                  pl.BlockSpec((1, 6, d), _mod_map(n_ctx // tm, mod.shape[0] - 1, 0)),
                  pl.BlockSpec((1, d), const),
                  resident(wg.shape, const),
                  resident(wu.shape, const),
                  resident(wd.shape, const)],
        out_specs=pl.BlockSpec((1, tm, d), tok),
        out_shape=jax.ShapeDtypeStruct((b, s, d), F32),
        compiler_params=_cparams("parallel", "parallel"),
        name="ffn_dense",
    )(h, mod, norm_w, wg, wu, wd)


def _moe_pre_kernel(h_ref, mod_ref, nw_ref, rw_ref, rb_ref, xn_ref, route_ref, *, n_experts):
    x = h_ref[0]
    a = _norm_mod(x, nw_ref[...], mod_ref[0, 3:4, :], mod_ref[0, 4:5, :])
    a_hi = a.astype(BF16)
    xn_ref[0] = a_hi
    a_lo = (a - a_hi.astype(F32)).astype(BF16)
    rw = rw_ref[...]
    w_hi = rw.astype(BF16)
    w_lo = (rw - w_hi.astype(F32)).astype(BF16)
    logits = (jnp.dot(a_hi, w_hi, preferred_element_type=F32) + jnp.dot(a_lo, w_hi, preferred_element_type=F32)
              + jnp.dot(a_hi, w_lo, preferred_element_type=F32)) + rb_ref[...]
    lane = lax.broadcasted_iota(jnp.int32, logits.shape, 1)
    l1 = jnp.where(lane < n_experts, logits, -jnp.inf)
    v1 = jnp.max(l1, axis=-1, keepdims=True)
    i1 = jnp.min(jnp.where(l1 == v1, lane, 128), axis=-1, keepdims=True)
    l2 = jnp.where(lane == i1, -jnp.inf, l1)
    v2 = jnp.max(l2, axis=-1, keepdims=True)
    i2 = jnp.min(jnp.where(l2 == v2, lane, 128), axis=-1, keepdims=True)
    e2 = jnp.exp(v2 - v1)
    w1 = 1.0 / (1.0 + e2)
    w2 = e2 * w1
    lane8 = lax.broadcasted_iota(jnp.int32, route_ref.shape[1:], 1)
    route = jnp.where(lane8 == 0, i1.astype(F32),
                      jnp.where(lane8 == 1, i2.astype(F32),
                                jnp.where(lane8 == 2, w1, jnp.where(lane8 == 3, w2, 0.0))))
    route_ref[0] = route


def moe_pre(h, mod, norm_w, router_w_pad, router_b_pad, *, n_experts):
    b, t, d = h.shape
    tm = TOKEN_TILE
    tok = lambda bi, i: (bi, i, 0)
    const = lambda bi, i: (0, 0)
    return pl.pallas_call(
        functools.partial(_moe_pre_kernel, n_experts=n_experts),
        grid=(b, t // tm),
        in_specs=[pl.BlockSpec((1, tm, d), tok),
                  pl.BlockSpec((1, 6, d), lambda bi, i: (bi, 0, 0)),
                  pl.BlockSpec((1, d), const),
                  pl.BlockSpec(router_w_pad.shape, const),
                  pl.BlockSpec((1, 128), const)],
        out_specs=(pl.BlockSpec((1, tm, d), tok), pl.BlockSpec((1, tm, 8), tok)),
        out_shape=(jax.ShapeDtypeStruct((b, t, d), BF16), jax.ShapeDtypeStruct((b, t, 8), F32)),
        compiler_params=_cparams("parallel", "parallel"),
        name="moe_pre",
    )(h, mod, norm_w, router_w_pad, router_b_pad)


def _moe_gmm_kernel(be_ref, nu_ref, x_ref, wg_ref, wu_ref, wd_ref, o_ref, acc_sc):
    i, j = pl.program_id(0), pl.program_id(1)
    nj = pl.num_programs(1)
    used = i < nu_ref[0]

    @pl.when(used)
    def _():
        x = x_ref[...]
        hg = jnp.dot(x, wg_ref[0], preferred_element_type=F32)
        hu = jnp.dot(x, wu_ref[0], preferred_element_type=F32)
        y = jnp.dot((_silu(hg) * hu).astype(BF16), wd_ref[0], preferred_element_type=F32)

        @pl.when(j == 0)
        def _():
            acc_sc[...] = y

        @pl.when(j > 0)
        def _():
            acc_sc[...] += y

    @pl.when(j == nj - 1)
    def _():
        o_ref[...] = jnp.where(used, acc_sc[...], 0.0)


def moe_gmm(block_e, n_used, xb, wg, wu, wd):
    n_slots, d = xb.shape
    tm, tf = MOE_TILE, MOE_FF_TILE
    n_blocks = n_slots // tm
    f = wg.shape[2]
    nj = f // tf
    last = nj - 1

    def jj(i, j, nu):
        return jnp.where(i < nu[0], j, last)

    grid_spec = pltpu.PrefetchScalarGridSpec(
        num_scalar_prefetch=2,
        grid=(n_blocks, nj),
        in_specs=[pl.BlockSpec((tm, d), lambda i, j, be, nu: (i, 0)),
                  pl.BlockSpec((1, d, tf), lambda i, j, be, nu: (be[i], 0, jj(i, j, nu))),
                  pl.BlockSpec((1, d, tf), lambda i, j, be, nu: (be[i], 0, jj(i, j, nu))),
                  pl.BlockSpec((1, tf, d), lambda i, j, be, nu: (be[i], jj(i, j, nu), 0))],
        out_specs=pl.BlockSpec((tm, d), lambda i, j, be, nu: (i, 0)),
        scratch_shapes=[pltpu.VMEM((tm, d), F32)],
    )
    return pl.pallas_call(
        _moe_gmm_kernel,
        grid_spec=grid_spec,
        out_shape=jax.ShapeDtypeStruct((n_slots, d), F32),
        compiler_params=_cparams("arbitrary", "arbitrary"),
        name="moe_gmm",
    )(block_e, n_used, xb, wg, wu, wd)


def _moe_combine_kernel(h_ref, mod_ref, route_ref, y1_ref, y2_ref, o_ref):
    r = route_ref[0]
    y = r[:, 2:3] * y1_ref[0] + r[:, 3:4] * y2_ref[0]
    o_ref[0] = h_ref[0] + mod_ref[0, 5:6, :] * y


def moe_combine(h, mod, route, y1, y2):
    b, t, d = h.shape
    tm = TOKEN_TILE
    tok = lambda bi, i: (bi, i, 0)
    return pl.pallas_call(
        _moe_combine_kernel,
        grid=(b, t // tm),
        in_specs=[pl.BlockSpec((1, tm, d), tok),
                  pl.BlockSpec((1, 6, d), lambda bi, i: (bi, 0, 0)),
                  pl.BlockSpec((1, tm, 8), tok),
                  pl.BlockSpec((1, tm, d), tok),
                  pl.BlockSpec((1, tm, d), tok)],
        out_specs=pl.BlockSpec((1, tm, d), tok),
        out_shape=jax.ShapeDtypeStruct((b, t, d), F32),
        compiler_params=_cparams("parallel", "parallel"),
        name="moe_combine",
    )(h, mod, route, y1, y2)


def _moe_dispatch_plan(expert_idx, n_experts):
    n = expert_idx.shape[0]
    tile = MOE_TILE
    flat_e = expert_idx.reshape(-1)
    onehot = (flat_e[None, :] == jnp.arange(n_experts, dtype=jnp.int32)[:, None]).astype(jnp.int32)
    csum = jnp.cumsum(onehot, axis=1)
    counts = csum[:, -1]
    padded = (counts + tile - 1) // tile * tile
    pad_ends = jnp.cumsum(padded)
    pad_starts = pad_ends - padded
    slot = jnp.sum(onehot * (pad_starts[:, None] + csum - 1), axis=0)
    n_blocks = -(-(n * TOP_K) // tile) + n_experts
    n_used = pad_ends[-1] // tile
    blk = jnp.arange(n_blocks, dtype=jnp.int32)
    block_e = jnp.minimum(jnp.searchsorted(pad_ends, blk * tile, side="right"), n_experts - 1).astype(jnp.int32)
    block_e = jnp.where(blk < n_used, block_e, block_e[jnp.maximum(n_used - 1, 0)])
    slot_tok = jnp.zeros((n_blocks * tile,), jnp.int32).at[slot].set(jnp.arange(n * TOP_K, dtype=jnp.int32) // TOP_K)
    return slot.reshape(n, TOP_K), slot_tok, block_e, n_used.astype(jnp.int32).reshape(1)


def _rope_tables(n_ctx, n_lat):
    rows = n_lat // GRID_W
    row = jnp.repeat(jnp.arange(rows, dtype=F32), GRID_W)
    col = jnp.tile(jnp.arange(GRID_W, dtype=F32), rows)
    n_freq = HEAD_DIM // 4
    inv_freq = ROPE_THETA ** (-jnp.arange(n_freq, dtype=F32) / n_freq)
    ang = jnp.concatenate([row[:, None] * inv_freq, col[:, None] * inv_freq], -1)
    cos, sin = jnp.cos(ang), jnp.sin(ang)
    cos_t = jnp.concatenate([cos, cos, cos, cos], axis=1)
    sin_t = jnp.concatenate([-sin, sin, -sin, sin], axis=1)
    cos_t = jnp.concatenate([jnp.ones((n_ctx, 128), F32), cos_t], axis=0)
    sin_t = jnp.concatenate([jnp.zeros((n_ctx, 128), F32), sin_t], axis=0)
    return cos_t, sin_t


def _block_ones(n):
    idx = jnp.arange(n) // HEAD_DIM
    return (idx[:, None] == idx[None, :]).astype(BF16)


def kernel(x, c, ctx, c_ctx, mod_w, mod_b, norm1_w, norm2_w, w_in, attn_qn_w, attn_kn_w, ret_decay, ret_norm_w,
           mlstm_conv_w, mlstm_gate_b, mlstm_norm_w, w_out, ffn_w_gate, ffn_w_up, ffn_w_down, router_w, router_b,
           moe_w_gate, moe_w_up, moe_w_down):
    b, n_lat, d = x.shape
    n_ctx = ctx.shape[1]
    depth = mod_w.shape[0]
    n_experts = router_w.shape[2]
    ret_heads = ret_decay.shape[2]
    ml_heads = mlstm_gate_b.shape[2]
    ret_w, ml_w = ret_heads * HEAD_DIM, ml_heads * HEAD_DIM
    n_gate = N_GATES * ml_heads
    kv_w = (w_in.shape[2] - 4 * ret_w - 4 * ml_w - n_gate) // 6
    att_w = 4 * kv_w
    dims = (att_w, kv_w, ret_w, ml_w, n_gate)
    assert n_ctx % TOKEN_TILE == 0 and n_lat % KV_TILE == 0 and b + 1 <= 16
    n_main = w_in.shape[2] - n_gate
    n_pad = -(-w_in.shape[2] // 128) * 128

    cos_t, sin_t = _rope_tables(n_ctx, n_lat)
    gq, gk, g_ret, g_ml = _block_ones(att_w), _block_ones(kv_w), _block_ones(ret_w), _block_ones(ml_w)
    cond = jnp.concatenate([c, c_ctx[None, :], jnp.zeros((16 - b - 1, d), F32)], axis=0)
    ctx_tiles = n_ctx // TOKEN_TILE
    ctx_chunks = n_ctx // CHUNK

    h = jnp.concatenate([ctx, x], axis=1)
    for layer in range(depth):
        last = layer == depth - 1
        skip = ctx_tiles if last else 0
        mod = adaln(cond, mod_w[layer], mod_b[layer]).reshape(16, 6, d)[:b + 1]
        w_all = jnp.pad(w_in[layer].astype(BF16), ((0, 0), (0, n_pad - w_in.shape[2])))
        w_gate_t = w_in[layer][:, n_main:].T.astype(BF16)
        qn = jnp.tile(attn_qn_w[layer], att_w // HEAD_DIM)[None, :]
        kn = jnp.tile(attn_kn_w[layer], kv_w // HEAD_DIM)[None, :]
        k_norm_max = math.sqrt(HEAD_DIM) * jnp.max(jnp.abs(attn_kn_w[layer])) * (1.0 + 2.0 ** -6)
        q_norm_max = math.sqrt(HEAD_DIM) * jnp.max(jnp.abs(attn_qn_w[layer])) * (QK_SCALE * LOG2E)
        fast = (2.0 * q_norm_max * k_norm_max <= MAX_SCORE_SPAN_LOG2).astype(jnp.int32).reshape(1)
        kbound = jnp.full((1, att_w), k_norm_max, F32)
        q, k, v, r, m, gcol, grow = proj_in(h, mod, norm1_w[layer][None, :], cos_t, sin_t, w_all, w_gate_t, qn, kn,
                                            kbound, gq, gk, n_ctx=n_ctx, dims=dims)
        att = attention(fast, q, k, v, n_ctx=n_ctx, skip=skip)
        dec = jnp.repeat(ret_decay[layer], HEAD_DIM, axis=1)
        sf, sb = ret_states(r, dec, n_ctx=n_ctx)
        ret = ret_out(r, sf, sb, dec, ret_norm_w[layer][None, :], g_ret, skip_chunks=ctx_chunks if last else 0)
        qk = mlstm_conv(m, mlstm_conv_w[layer], n_ctx=n_ctx)
        bias = mlstm_gate_b[layer].reshape(-1)
        cf, nmf, cb, nmb = mlstm_states(qk, m, gcol, bias[None, :], n_ctx=n_ctx, n_heads=ml_heads)
        ml = mlstm_out(qk, m, gcol, grow, bias[None, :], bias[:, None], cf, nmf, cb, nmb, mlstm_norm_w[layer][None, :],
                       g_ml, skip_chunks=ctx_chunks if last else 0, n_heads=ml_heads)
        h = proj_out(att, ret, ml, h, mod, w_out[layer].astype(BF16), n_ctx=n_ctx, skip=skip)
        if layer % 2 == 0:
            li = layer // 2
            assert not last, "dense FFN on a latent-only stream is not wired up"
            h = ffn_dense(h, mod, norm2_w[layer][None, :], ffn_w_gate[li].astype(BF16), ffn_w_up[li].astype(BF16),
                          ffn_w_down[li].astype(BF16), n_ctx=n_ctx)
        else:
            li = layer // 2
            assert last, "expert FFN on the combined stream is not wired up"
            mod_lat = mod[:b]
            rw = jnp.pad(router_w[li], ((0, 0), (0, 128 - n_experts)))
            rb = jnp.pad(router_b[li], (0, 128 - n_experts))[None, :]
            xn, route = moe_pre(h, mod_lat, norm2_w[layer][None, :], rw, rb, n_experts=n_experts)
            t = h.shape[1]
            route_flat = route.reshape(b * t, 8)
            slot, slot_tok, block_e, n_used = _moe_dispatch_plan(route_flat[:, 0:TOP_K].astype(jnp.int32), n_experts)
            xb = jnp.take(xn.reshape(b * t, d), slot_tok, axis=0)
            yb = moe_gmm(block_e, n_used, xb, moe_w_gate[li].astype(BF16), moe_w_up[li].astype(BF16),
                         moe_w_down[li].astype(BF16))
            y1 = jnp.take(yb, slot[:, 0], axis=0).reshape(b, t, d)
            y2 = jnp.take(yb, slot[:, 1], axis=0).reshape(b, t, d)
            h = moe_combine(h, mod_lat, route, y1, y2)
    return h if h.shape[1] == n_lat else h[:, n_ctx:, :]
```

```python
import functools
import math

import jax
import jax.numpy as jnp
from jax import lax
from jax.experimental import pallas as pl
from jax.experimental.pallas import tpu as pltpu

F32 = jnp.float32
BF16 = jnp.bfloat16

HEAD_DIM = 64
GRID_W = 64
ROPE_THETA = 10000.0
EPS = 1e-6
CHUNK = 128
SUB_CHUNKS = 2
TOKEN_TILE = 256
KV_TILE = 1024
N_GATES = 4
TOP_K = 2
MOE_TILE = 512
MOE_FF_TILE = 1792
VMEM_LIMIT = 56 * 1024 * 1024
QK_SCALE = HEAD_DIM ** -0.5
LOG2E = math.log2(math.e)
MAX_SCORE_SPAN_LOG2 = 100.0


def _cparams(*sem):
    return pltpu.CompilerParams(dimension_semantics=sem, vmem_limit_bytes=VMEM_LIMIT)


def _silu(x):
    return x * jax.nn.sigmoid(x)


def _split_dot(x, ones_bf16, parts=2):
    acc = None
    r = x
    for _ in range(parts):
        hi = r.astype(BF16)
        d = jnp.dot(hi, ones_bf16, preferred_element_type=F32)
        acc = d if acc is None else acc + d
        r = r - hi.astype(F32)
    return acc


def _dot_nt(a, b):
    return lax.dot_general(a, b, (((1,), (1,)), ((), ())), preferred_element_type=F32)


def _rope(x, cos_t, sin_t):
    w = x.shape[1]
    reps = w // 128
    c = jnp.concatenate([cos_t] * reps, axis=1) if reps > 1 else cos_t
    s = jnp.concatenate([sin_t] * reps, axis=1) if reps > 1 else sin_t
    lane = lax.broadcasted_iota(jnp.int32, x.shape, 1)
    first_half = (lane % HEAD_DIM) < (HEAD_DIM // 2)
    swapped = jnp.where(first_half, pltpu.roll(x, w - HEAD_DIM // 2, 1), pltpu.roll(x, HEAD_DIM // 2, 1))
    return x * c + swapped * s


def _group_rms(x, gmat, gain):
    ms = _split_dot(x * x, gmat) * (1.0 / HEAD_DIM)
    return x * lax.rsqrt(ms + EPS) * gain


def _group_layer_norm(x, gmat, gain):
    mu = _split_dot(x, gmat) * (1.0 / HEAD_DIM)
    xc = x - mu
    var = _split_dot(xc * xc, gmat) * (1.0 / HEAD_DIM)
    return xc * lax.rsqrt(var + EPS) * gain


def _expand_heads(x, n_heads, lane):
    out = x[:, n_heads - 1:n_heads]
    for h in range(n_heads - 2, -1, -1):
        out = jnp.where(lane < (h + 1) * HEAD_DIM, x[:, h:h + 1], out)
    return out


def _block_diag_mask(n):
    r = lax.broadcasted_iota(jnp.int32, (n, n), 0) // HEAD_DIM
    c = lax.broadcasted_iota(jnp.int32, (n, n), 1) // HEAD_DIM
    return r == c


def _adaln_kernel(c_ref, w_ref, b_ref, o_ref):
    a = _silu(c_ref[...]).astype(BF16)
    o_ref[...] = jnp.dot(a, w_ref[...].astype(BF16), preferred_element_type=F32) + b_ref[...]


def adaln(cond, w, b):
    r, d = cond.shape
    n = w.shape[1]
    tn = 1536 if n % 1536 == 0 else n
    return pl.pallas_call(
        _adaln_kernel,
        grid=(n // tn,),
        in_specs=[pl.BlockSpec((r, d), lambda j: (0, 0)),
                  pl.BlockSpec((d, tn), lambda j: (0, j)),
                  pl.BlockSpec((1, tn), lambda j: (0, j))],
        out_specs=pl.BlockSpec((r, tn), lambda j: (0, j)),
        out_shape=jax.ShapeDtypeStruct((r, n), F32),
        compiler_params=_cparams("arbitrary"),
        name="adaln",
    )(cond, w, b.reshape(1, n))


def _norm_mod(x, nw, shift, scale):
    ms = jnp.mean(x * x, axis=-1, keepdims=True)
    y = x * lax.rsqrt(ms + EPS) * nw
    return y * (1.0 + scale) + shift


def _proj_in_kernel(h_ref, mod_ref, nw_ref, cos_ref, sin_ref, w_ref, wgt_ref, qn_ref, kn_ref, kb_ref, gq_ref, gk_ref,
                    q_ref, k_ref, v_ref, r_ref, m_ref, gc_ref, gr_ref, *, dims):
    att_w, kv_w, ret_w, ml_w, n_gate = dims
    x = h_ref[0]
    a = _norm_mod(x, nw_ref[...], mod_ref[0, 0:1, :], mod_ref[0, 1:2, :]).astype(BF16)
    p = jnp.dot(a, w_ref[...], preferred_element_type=F32)
    cos_t, sin_t = cos_ref[...], sin_ref[...]
    o = 0
    aq = p[:, o:o + att_w]; o += att_w
    ak = p[:, o:o + kv_w]; o += kv_w
    av = p[:, o:o + kv_w]; o += kv_w
    aq = _rope(_group_rms(aq, gq_ref[...], qn_ref[...]), cos_t, sin_t) * (QK_SCALE * LOG2E)
    ak = _rope(_group_rms(ak, gk_ref[...], kn_ref[...]), cos_t, sin_t)
    neg_bound = -jnp.sqrt(_split_dot(aq * aq, gq_ref[...])) * kb_ref[...]
    tm = x.shape[0]
    lane = lax.broadcasted_iota(jnp.int32, (tm, 2 * HEAD_DIM), 1)

    def widen(a, extra):
        return jnp.where(lane == HEAD_DIM, extra, jnp.concatenate([a, jnp.zeros_like(a)], axis=1)).astype(BF16)

    for hh in range(att_w // HEAD_DIM):
        cols = slice(hh * HEAD_DIM, (hh + 1) * HEAD_DIM)
        q_ref[0, hh] = widen(aq[:, cols], neg_bound[:, hh * HEAD_DIM:hh * HEAD_DIM + 1])
    for hh in range(kv_w // HEAD_DIM):
        cols = slice(hh * HEAD_DIM, (hh + 1) * HEAD_DIM)
        k_ref[0, hh] = widen(ak[:, cols], 1.0)
        v_ref[0, hh] = widen(av[:, cols], 1.0)
    rq = _rope(p[:, o:o + ret_w], cos_t, sin_t)
    rk = _rope(p[:, o + ret_w:o + 2 * ret_w] * QK_SCALE, cos_t, sin_t)
    r_ref[0, :, 0:ret_w] = rq
    r_ref[0, :, ret_w:2 * ret_w] = rk
    r_ref[0, :, 2 * ret_w:4 * ret_w] = p[:, o + 2 * ret_w:o + 4 * ret_w]
    o += 4 * ret_w
    m_ref[0] = p[:, o:o + 4 * ml_w]
    o += 4 * ml_w
    gc_ref[0] = p[:, o:o + n_gate]
    gr_ref[0] = _dot_nt(wgt_ref[...], a)


def proj_in(h, mod, norm_w, cos_t, sin_t, w_all, w_gate_t, qn, kn, kbound, gq, gk, *, n_ctx, dims):
    att_w, kv_w, ret_w, ml_w, n_gate = dims
    b, s, d = h.shape
    tm = TOKEN_TILE
    n_ctx_tiles = n_ctx // tm
    n_pad = w_all.shape[1]
    ctx_row = mod.shape[0] - 1

    def mod_map(bi, i):
        return (jnp.where(i < n_ctx_tiles, ctx_row, bi), 0, 0)

    const2 = lambda bi, i: (0, 0)
    out_shape = (
        jax.ShapeDtypeStruct((b, att_w // HEAD_DIM, s, 2 * HEAD_DIM), BF16),
        jax.ShapeDtypeStruct((b, kv_w // HEAD_DIM, s, 2 * HEAD_DIM), BF16),
        jax.ShapeDtypeStruct((b, kv_w // HEAD_DIM, s, 2 * HEAD_DIM), BF16),
        jax.ShapeDtypeStruct((b, s, 4 * ret_w), F32),
        jax.ShapeDtypeStruct((b, s, 4 * ml_w), F32),
        jax.ShapeDtypeStruct((b, s, n_gate), F32),
        jax.ShapeDtypeStruct((b, n_gate, s), F32),
    )
    return pl.pallas_call(
        functools.partial(_proj_in_kernel, dims=dims),
        grid=(b, s // tm),
        in_specs=[
            pl.BlockSpec((1, tm, d), lambda bi, i: (bi, i, 0)),
            pl.BlockSpec((1, 6, d), mod_map),
            pl.BlockSpec((1, d), const2),
            pl.BlockSpec((tm, 128), lambda bi, i: (i, 0)),
            pl.BlockSpec((tm, 128), lambda bi, i: (i, 0)),
            pl.BlockSpec((d, n_pad), const2),
            pl.BlockSpec((n_gate, d), const2),
            pl.BlockSpec((1, att_w), const2),
            pl.BlockSpec((1, kv_w), const2),
            pl.BlockSpec((1, att_w), const2),
            pl.BlockSpec((att_w, att_w), const2),
            pl.BlockSpec((kv_w, kv_w), const2),
        ],
        out_specs=(
            pl.BlockSpec((1, att_w // HEAD_DIM, tm, 2 * HEAD_DIM), lambda bi, i: (bi, 0, i, 0)),
            pl.BlockSpec((1, kv_w // HEAD_DIM, tm, 2 * HEAD_DIM), lambda bi, i: (bi, 0, i, 0)),
            pl.BlockSpec((1, kv_w // HEAD_DIM, tm, 2 * HEAD_DIM), lambda bi, i: (bi, 0, i, 0)),
            pl.BlockSpec((1, tm, 4 * ret_w), lambda bi, i: (bi, i, 0)),
            pl.BlockSpec((1, tm, 4 * ml_w), lambda bi, i: (bi, i, 0)),
            pl.BlockSpec((1, tm, n_gate), lambda bi, i: (bi, i, 0)),
            pl.BlockSpec((1, n_gate, tm), lambda bi, i: (bi, 0, i)),
        ),
        out_shape=out_shape,
        compiler_params=_cparams("parallel", "parallel"),
        name="proj_in",
    )(h, mod, norm_w, cos_t, sin_t, w_all, w_gate_t, qn, kn, kbound, gq, gk)


def _conv_kernel(x_ref, prev_ref, next_ref, w_ref, o_ref, *, n_ctx, seq, half):
    i = pl.program_id(1)
    x = x_ref[0]
    tm, w = x.shape
    row = lax.broadcasted_iota(jnp.int32, (tm, 1), 0)
    grow = row + i * tm
    xp = jnp.where(row == 0, prev_ref[0, 7:8, :], pltpu.roll(x, 1, 0))
    xn = jnp.where(row == tm - 1, next_ref[0, 0:1, :], pltpu.roll(x, tm - 1, 0))
    xp = jnp.where((grow == 0) | (grow == n_ctx), 0.0, xp)
    xn = jnp.where((grow == n_ctx - 1) | (grow == seq - 1), 0.0, xn)
    y = _silu(xp * w_ref[0:1, :] + x * w_ref[1:2, :] + xn * w_ref[2:3, :])
    lane = lax.broadcasted_iota(jnp.int32, (tm, w), 1)
    o_ref[0] = jnp.where(lane >= half, y * QK_SCALE, y)


def mlstm_conv(m, conv_w, *, n_ctx):
    b, s, _ = m.shape
    w = conv_w.shape[1]
    tm = TOKEN_TILE
    r8 = tm // 8
    last8 = s // 8 - 1
    return pl.pallas_call(
        functools.partial(_conv_kernel, n_ctx=n_ctx, seq=s, half=w // 2),
        grid=(b, s // tm),
        in_specs=[pl.BlockSpec((1, tm, w), lambda bi, i: (bi, i, 0)),
                  pl.BlockSpec((1, 8, w), lambda bi, i: (bi, jnp.maximum(i * r8 - 1, 0), 0)),
                  pl.BlockSpec((1, 8, w), lambda bi, i: (bi, jnp.minimum((i + 1) * r8, last8), 0)),
                  pl.BlockSpec((3, w), lambda bi, i: (0, 0))],
        out_specs=pl.BlockSpec((1, tm, w), lambda bi, i: (bi, i, 0)),
        out_shape=jax.ShapeDtypeStruct((b, s, w), F32),
        compiler_params=_cparams("parallel", "parallel"),
        name="mlstm_conv",
    )(m, m, m, conv_w)


def _attn_kernel(fast_ref, q_ref, k_ref, v_ref, o_ref, m_sc, acc_sc, *, n_ctx, n_lat_tiles, skip):
    i = pl.program_id(2) + skip
    g, tq = q_ref.shape[1], q_ref.shape[2]
    hd = HEAD_DIM

    def fast_step(k, v):
        for h in range(g):
            rows = slice(h * tq, (h + 1) * tq)
            p = jnp.exp2(_dot_nt(q_ref[0, h], k))
            acc_sc[rows] += jnp.dot(p.astype(BF16), v, preferred_element_type=F32)

    def safe_step(k, v):
        for h in range(g):
            rows = slice(h * tq, (h + 1) * tq)
            s = _dot_nt(q_ref[0, h], k)
            m_old = m_sc[rows]
            m_new = jnp.maximum(m_old, jnp.max(s, axis=-1, keepdims=True))
            p = jnp.exp2(s - jnp.concatenate([m_new] * (s.shape[1] // 128), axis=1))
            acc_sc[rows] = jnp.exp2(m_old - m_new) * acc_sc[rows] + jnp.dot(p.astype(BF16), v, preferred_element_type=F32)
            m_sc[rows] = m_new

    def sweep(step, unroll):
        step(k_ref[0, 0, 0:n_ctx, :], v_ref[0, 0, 0:n_ctx, :])

        @pl.when(i * tq >= n_ctx)
        def _():
            def body(j, carry):
                start = pl.multiple_of(n_ctx + j * KV_TILE, math.gcd(n_ctx, KV_TILE))
                step(k_ref[0, 0, pl.ds(start, KV_TILE), :], v_ref[0, 0, pl.ds(start, KV_TILE), :])
                return carry
            lax.fori_loop(0, n_lat_tiles, body, 0, unroll=unroll)

    acc_sc[...] = jnp.zeros(acc_sc.shape, F32)

    @pl.when(fast_ref[0] == 1)
    def _():
        sweep(fast_step, 2)

    @pl.when(fast_ref[0] != 1)
    def _():
        m_sc[...] = jnp.full(m_sc.shape, -jnp.inf, F32)
        sweep(safe_step, 1)

    acc = acc_sc[...]
    out = acc[:, 0:hd] / acc[:, hd:hd + 1]
    o_ref[0] = jnp.concatenate([out[h * tq:(h + 1) * tq] for h in range(g)], axis=1)


def attention(fast, q, k, v, *, n_ctx, skip):
    b, hq, s, lanes = q.shape
    hkv = k.shape[1]
    g = hq // hkv
    tq = TOKEN_TILE
    n_lat = s - n_ctx
    assert n_lat % KV_TILE == 0 and n_ctx % tq == 0 and lanes == 2 * HEAD_DIM
    nq = s // tq - skip
    grid_spec = pltpu.PrefetchScalarGridSpec(
        num_scalar_prefetch=1,
        grid=(b, hkv, nq),
        in_specs=[pl.BlockSpec((1, g, tq, lanes), lambda bi, gi, i, f: (bi, gi, i + skip, 0)),
                  pl.BlockSpec((1, 1, s, lanes), lambda bi, gi, i, f: (bi, gi, 0, 0)),
                  pl.BlockSpec((1, 1, s, lanes), lambda bi, gi, i, f: (bi, gi, 0, 0))],
        out_specs=pl.BlockSpec((1, tq, g * HEAD_DIM), lambda bi, gi, i, f: (bi, i, gi)),
        scratch_shapes=[pltpu.VMEM((g * tq, 128), F32), pltpu.VMEM((g * tq, lanes), F32)],
    )
    return pl.pallas_call(
        functools.partial(_attn_kernel, n_ctx=n_ctx, n_lat_tiles=n_lat // KV_TILE, skip=skip),
        grid_spec=grid_spec,
        out_shape=jax.ShapeDtypeStruct((b, nq * tq, hq * HEAD_DIM), F32),
        compiler_params=_cparams("parallel", "parallel", "arbitrary"),
        name="attention",
    )(fast, q, k, v)


def _rev_chunk(c, nc_ctx, nc):
    return jnp.where(c < nc_ctx, nc_ctx - 1 - c, nc - 1 + nc_ctx - c)


def _ret_state_kernel(kf_ref, vf_ref, kb_ref, vb_ref, dec_ref, sf_ref, sb_ref, s_sc):
    c = pl.program_id(1)

    @pl.when(c == 0)
    def _():
        s_sc[...] = jnp.zeros(s_sc.shape, F32)

    w = kf_ref.shape[2]
    lg = jnp.log1p(-jnp.exp(dec_ref[...]))
    pos = lax.broadcasted_iota(jnp.int32, (CHUNK, 1), 0).astype(F32)
    bd = _block_diag_mask(w)
    for d, (k_ref, v_ref, out_ref) in enumerate(((kf_ref, vf_ref, sf_ref), (kb_ref, vb_ref, sb_ref))):
        lgd = lg[d:d + 1, :]
        e = (CHUNK - 1.0 - pos) if d == 0 else pos
        kscale = jnp.exp(e * lgd)
        chunk_decay = jnp.exp(CHUNK * lgd)
        for sub in (range(SUB_CHUNKS) if d == 0 else reversed(range(SUB_CHUNKS))):
            rows = slice(sub * CHUNK, (sub + 1) * CHUNK)
            kdec = k_ref[0, rows, :] * kscale
            inc = jnp.dot(kdec.T.astype(BF16), v_ref[0, rows, :].astype(BF16), preferred_element_type=F32)
            s_old = s_sc[d]
            out_ref[0, sub] = s_old.astype(BF16)
            s_sc[d] = chunk_decay * s_old + jnp.where(bd, inc, 0.0)


def ret_states(r, dec, *, n_ctx):
    b, s, w4 = r.shape
    w = w4 // 4
    nc, nc_ctx = s // CHUNK, n_ctx // CHUNK
    assert nc % SUB_CHUNKS == 0 and nc_ctx % SUB_CHUNKS == 0
    ng, rows = nc // SUB_CHUNKS, SUB_CHUNKS * CHUNK
    rev = lambda c: _rev_chunk(c, nc_ctx // SUB_CHUNKS, ng)
    st = jax.ShapeDtypeStruct((b, nc, w, w), BF16)
    return pl.pallas_call(
        _ret_state_kernel,
        grid=(b, ng),
        in_specs=[pl.BlockSpec((1, rows, w), lambda bi, c: (bi, c, 1)),
                  pl.BlockSpec((1, rows, w), lambda bi, c: (bi, c, 2)),
                  pl.BlockSpec((1, rows, w), lambda bi, c: (bi, rev(c), 1)),
                  pl.BlockSpec((1, rows, w), lambda bi, c: (bi, rev(c), 2)),
                  pl.BlockSpec((2, w), lambda bi, c: (0, 0))],
        out_specs=(pl.BlockSpec((1, SUB_CHUNKS, w, w), lambda bi, c: (bi, c, 0, 0)),
                   pl.BlockSpec((1, SUB_CHUNKS, w, w), lambda bi, c: (bi, rev(c), 0, 0))),
        out_shape=(st, st),
        scratch_shapes=[pltpu.VMEM((2, w, w), F32)],
        compiler_params=_cparams("parallel", "arbitrary"),
        name="ret_states",
    )(r, r, r, r, dec)


def _ret_out_kernel(r_ref, sf_ref, sb_ref, dec_ref, nw_ref, gmat_ref, o_ref):
    w = o_ref.shape[2]
    n_heads = w // HEAD_DIM
    lg = jnp.log1p(-jnp.exp(dec_ref[...]))
    pos = lax.broadcasted_iota(jnp.int32, (CHUNK, 1), 0).astype(F32)
    row = lax.broadcasted_iota(jnp.int32, (CHUNK, CHUNK), 0)
    col = lax.broadcasted_iota(jnp.int32, (CHUNK, CHUNK), 1)
    lane = lax.broadcasted_iota(jnp.int32, (CHUNK, w), 1)
    head_masks = [(lane >= h * HEAD_DIM) & (lane < (h + 1) * HEAD_DIM) for h in range(n_heads)]
    q_scale = []
    decay = [jnp.zeros((CHUNK, CHUNK), F32)] * n_heads
    for d in range(2):
        lgd = lg[d:d + 1, :]
        q_scale.append(jnp.exp(((pos + 1.0) if d == 0 else (CHUNK - pos)) * lgd))
        lag = ((row - col) if d == 0 else (col - row)).astype(F32)
        decay = [decay[h] + jnp.where(lag >= 0, jnp.exp(jnp.maximum(lag, 0.0) * lgd[:, h * HEAD_DIM:h * HEAD_DIM + 1]), 0.0)
                 for h in range(n_heads)]
    for sub in range(SUB_CHUNKS):
        x = r_ref[0, sub * CHUNK:(sub + 1) * CHUNK, :]
        q, k, v, g = x[:, 0:w], x[:, w:2 * w], x[:, 2 * w:3 * w], x[:, 3 * w:4 * w]
        kb, vb = k.astype(BF16), v.astype(BF16)
        total = jnp.dot((q * q_scale[0]).astype(BF16), sf_ref[0, sub], preferred_element_type=F32)
        total = total + jnp.dot((q * q_scale[1]).astype(BF16), sb_ref[0, sub], preferred_element_type=F32)
        for h in range(n_heads):
            sc = _dot_nt(jnp.where(head_masks[h], q, 0.0).astype(BF16), kb)
            pv = jnp.dot((sc * decay[h]).astype(BF16), vb, preferred_element_type=F32)
            total = total + jnp.where(head_masks[h], pv, 0.0)
        o_ref[0, sub * CHUNK:(sub + 1) * CHUNK, :] = _group_layer_norm(total, gmat_ref[...], nw_ref[...]) * _silu(g)


def ret_out(r, sf, sb, dec, norm_w, gmat, *, skip_chunks):
    b, s, w4 = r.shape
    w = w4 // 4
    assert skip_chunks % SUB_CHUNKS == 0 and (s // CHUNK) % SUB_CHUNKS == 0
    sk = skip_chunks // SUB_CHUNKS
    ng, rows = (s // CHUNK) // SUB_CHUNKS - sk, SUB_CHUNKS * CHUNK
    return pl.pallas_call(
        _ret_out_kernel,
        grid=(b, ng),
        in_specs=[pl.BlockSpec((1, rows, w4), lambda bi, c: (bi, c + sk, 0)),
                  pl.BlockSpec((1, SUB_CHUNKS, w, w), lambda bi, c: (bi, c + sk, 0, 0)),
                  pl.BlockSpec((1, SUB_CHUNKS, w, w), lambda bi, c: (bi, c + sk, 0, 0)),
                  pl.BlockSpec((2, w), lambda bi, c: (0, 0)),
                  pl.BlockSpec((1, w), lambda bi, c: (0, 0)),
                  pl.BlockSpec((w, w), lambda bi, c: (0, 0))],
        out_specs=pl.BlockSpec((1, rows, w), lambda bi, c: (bi, c, 0)),
        out_shape=jax.ShapeDtypeStruct((b, ng * rows, w), F32),
        compiler_params=_cparams("parallel", "parallel"),
        name="ret_out",
    )(r, sf, sb, dec, norm_w, gmat)


def _tri_masks():
    row = lax.broadcasted_iota(jnp.int32, (CHUNK, CHUNK), 0)
    col = lax.broadcasted_iota(jnp.int32, (CHUNK, CHUNK), 1)
    return col <= row, col >= row


def _ones_dot(ones_bf16, x, parts=3):
    acc = None
    r = x
    for _ in range(parts):
        hi = r.astype(BF16)
        d = jnp.dot(ones_bf16, hi, preferred_element_type=F32)
        acc = d if acc is None else acc + d
        r = r - hi.astype(F32)
    return acc


def _mlstm_state_kernel(kf_ref, vf_ref, gf_ref, kb_ref, vb_ref, gb_ref, bias_ref,
                        cf_ref, nmf_ref, cb_ref, nmb_ref, c_sc, nm_sc, *, n_heads):
    c = pl.program_id(1)

    @pl.when(c == 0)
    def _():
        c_sc[...] = jnp.zeros(c_sc.shape, F32)
        nm_sc[...] = jnp.zeros(nm_sc.shape, F32)

    w = kf_ref.shape[2]
    hh = n_heads
    tril, triu = _tri_masks()
    lane_t = lax.broadcasted_iota(jnp.int32, (CHUNK, w), 1)
    lane_1 = lax.broadcasted_iota(jnp.int32, (1, w), 1)
    bd = _block_diag_mask(w)
    dirs = ((kf_ref, vf_ref, gf_ref, cf_ref, nmf_ref, tril), (kb_ref, vb_ref, gb_ref, cb_ref, nmb_ref, triu))
    for d, (k_ref, v_ref, g_ref, c_out, nm_out, tmat) in enumerate(dirs):
        for sub in (range(SUB_CHUNKS) if d == 0 else reversed(range(SUB_CHUNKS))):
            rows = slice(sub * CHUNK, (sub + 1) * CHUNK)
            pre = g_ref[0, rows, :] + bias_ref[...]
            ig = pre[:, d * hh:(d + 1) * hh]
            lf_all = jax.nn.log_sigmoid(pre)
            b_all = _ones_dot(tmat.astype(BF16), lf_all)
            lf = lf_all[:, (2 + d) * hh:(3 + d) * hh]
            bcum = b_all[:, (2 + d) * hh:(3 + d) * hh]
            b_end = jnp.sum(lf, axis=0, keepdims=True)
            a = b_end - bcum + ig
            m_loc = jnp.max(a, axis=0, keepdims=True)
            wgt = jnp.exp(a - m_loc)
            kw = k_ref[0, rows, :] * _expand_heads(wgt, hh, lane_t)
            c_inc = jnp.dot(kw.T.astype(BF16), v_ref[0, rows, :].astype(BF16), preferred_element_type=F32)
            n_inc = jnp.sum(kw, axis=0, keepdims=True)
            be = _expand_heads(b_end, hh, lane_1)
            ml = _expand_heads(m_loc, hh, lane_1)
            c_old = c_sc[d]
            n_old = nm_sc[d, 0:1, :]
            m_old = nm_sc[d, 1:2, :]
            c_out[0, sub] = c_old.astype(BF16)
            nm_out[0, sub] = nm_sc[d]
            m_new = jnp.maximum(be + m_old, ml)
            w_prev = jnp.exp(be + m_old - m_new)
            w_inc = jnp.exp(ml - m_new)
            c_sc[d] = w_prev * c_old + w_inc * jnp.where(bd, c_inc, 0.0)
            nm_sc[d, 0:1, :] = w_prev * n_old + w_inc * n_inc
            nm_sc[d, 1:2, :] = m_new


def mlstm_states(qk, m, gcol, bias_row, *, n_ctx, n_heads):
    b, s, w2 = qk.shape
    w = w2 // 2
    ng = gcol.shape[2]
    nc, nc_ctx = s // CHUNK, n_ctx // CHUNK
    assert nc % SUB_CHUNKS == 0 and nc_ctx % SUB_CHUNKS == 0
    n_groups, rows = nc // SUB_CHUNKS, SUB_CHUNKS * CHUNK
    rev = lambda c: _rev_chunk(c, nc_ctx // SUB_CHUNKS, n_groups)
    cst = jax.ShapeDtypeStruct((b, nc, w, w), BF16)
    nmst = jax.ShapeDtypeStruct((b, nc, 8, w), F32)
    fwd3 = lambda j: (lambda bi, c: (bi, c, j))
    bwd3 = lambda j: (lambda bi, c: (bi, rev(c), j))
    return pl.pallas_call(
        functools.partial(_mlstm_state_kernel, n_heads=n_heads),
        grid=(b, n_groups),
        in_specs=[pl.BlockSpec((1, rows, w), fwd3(1)),
                  pl.BlockSpec((1, rows, w), fwd3(2)),
                  pl.BlockSpec((1, rows, ng), fwd3(0)),
                  pl.BlockSpec((1, rows, w), bwd3(1)),
                  pl.BlockSpec((1, rows, w), bwd3(2)),
                  pl.BlockSpec((1, rows, ng), bwd3(0)),
                  pl.BlockSpec((1, ng), lambda bi, c: (0, 0))],
        out_specs=(pl.BlockSpec((1, SUB_CHUNKS, w, w), lambda bi, c: (bi, c, 0, 0)),
                   pl.BlockSpec((1, SUB_CHUNKS, 8, w), lambda bi, c: (bi, c, 0, 0)),
                   pl.BlockSpec((1, SUB_CHUNKS, w, w), lambda bi, c: (bi, rev(c), 0, 0)),
                   pl.BlockSpec((1, SUB_CHUNKS, 8, w), lambda bi, c: (bi, rev(c), 0, 0))),
        out_shape=(cst, nmst, cst, nmst),
        scratch_shapes=[pltpu.VMEM((2, w, w), F32), pltpu.VMEM((2, 8, w), F32)],
        compiler_params=_cparams("parallel", "arbitrary"),
        name="mlstm_states",
    )(qk, m, gcol, qk, m, gcol, bias_row)


def _mlstm_out_kernel(qk_ref, m_ref, gc_ref, gr_ref, bc_ref, br_ref, cf_ref, nmf_ref, cb_ref, nmb_ref,
                      nw_ref, gmat_ref, o_ref, *, n_heads):
    w = o_ref.shape[2]
    hh = n_heads
    tril, triu = _tri_masks()
    lane = lax.broadcasted_iota(jnp.int32, (CHUNK, w), 1)
    head_masks = [(lane >= h * HEAD_DIM) & (lane < (h + 1) * HEAD_DIM) for h in range(hh)]
    gmat = gmat_ref[...]
    for sub in range(SUB_CHUNKS):
        rows = slice(sub * CHUNK, (sub + 1) * CHUNK)
        qk = qk_ref[0, rows, :]
        q, k = qk[:, 0:w], qk[:, w:2 * w]
        mm = m_ref[0, rows, :]
        v, og = mm[:, 2 * w:3 * w], mm[:, 3 * w:4 * w]
        pre_c = gc_ref[0, rows, :] + bc_ref[...]
        pre_r = gr_ref[0, :, rows] + br_ref[...]
        lf_c = jax.nn.log_sigmoid(pre_c)
        lf_r = jax.nn.log_sigmoid(pre_r)
        kb, vb, qb = k.astype(BF16), v.astype(BF16), q.astype(BF16)
        scores = [_dot_nt(jnp.where(hm, q, 0.0).astype(BF16), kb) for hm in head_masks]
        total = jnp.zeros((CHUNK, w), F32)
        for d, (c_ref, nm_ref) in enumerate(((cf_ref, nmf_ref), (cb_ref, nmb_ref))):
            tmat = tril if d == 0 else triu
            tmat_t = triu if d == 0 else tril
            b_c = _ones_dot(tmat.astype(BF16), lf_c)
            b_r = _split_dot(lf_r, tmat_t.astype(BF16), parts=3)
            nm = nm_ref[0, sub]
            n_row, m_row = nm[0:1, :], nm[1:2, :]
            cross = jnp.dot(qb, c_ref[0, sub], preferred_element_type=F32)
            qn = _split_dot(q * n_row, gmat)
            num = jnp.zeros((CHUNK, w), F32)
            den_e = jnp.zeros((CHUNK, w), F32)
            wp_e = jnp.zeros((CHUNK, w), F32)
            mt_e = jnp.zeros((CHUNK, w), F32)
            for h in range(hh):
                fc, ic = (2 + d) * hh + h, d * hh + h
                hm = head_masks[h]
                bcol = b_c[:, fc:fc + 1]
                logd = jnp.where(tmat, bcol - b_r[fc:fc + 1, :] + pre_r[ic:ic + 1, :], -jnp.inf)
                gg = bcol + m_row[:, h * HEAD_DIM:h * HEAD_DIM + 1]
                m_t = jnp.maximum(gg, jnp.max(logd, axis=-1, keepdims=True))
                p = scores[h] * jnp.exp(logd - m_t)
                pv = jnp.dot(p.astype(BF16), vb, preferred_element_type=F32)
                num = num + jnp.where(hm, pv, 0.0)
                den_e = jnp.where(hm, jnp.sum(p, axis=-1, keepdims=True), den_e)
                wp_e = jnp.where(hm, jnp.exp(gg - m_t), wp_e)
                mt_e = jnp.where(hm, m_t, mt_e)
            num = num + wp_e * cross
            den = den_e + wp_e * qn
            total = total + num / jnp.maximum(jnp.abs(den), jnp.exp(-mt_e))
        o_ref[0, rows, :] = _group_layer_norm(total * jax.nn.sigmoid(og), gmat, nw_ref[...])


def mlstm_out(qk, m, gcol, grow, bias_row, bias_col, cf, nmf, cb, nmb, norm_w, gmat, *, skip_chunks, n_heads):
    b, s, w2 = qk.shape
    w = w2 // 2
    ng = gcol.shape[2]
    assert skip_chunks % SUB_CHUNKS == 0 and (s // CHUNK) % SUB_CHUNKS == 0
    sk = skip_chunks // SUB_CHUNKS
    n_groups, rows = (s // CHUNK) // SUB_CHUNKS - sk, SUB_CHUNKS * CHUNK
    return pl.pallas_call(
        functools.partial(_mlstm_out_kernel, n_heads=n_heads),
        grid=(b, n_groups),
        in_specs=[pl.BlockSpec((1, rows, w2), lambda bi, c: (bi, c + sk, 0)),
                  pl.BlockSpec((1, rows, 4 * w), lambda bi, c: (bi, c + sk, 0)),
                  pl.BlockSpec((1, rows, ng), lambda bi, c: (bi, c + sk, 0)),
                  pl.BlockSpec((1, ng, rows), lambda bi, c: (bi, 0, c + sk)),
                  pl.BlockSpec((1, ng), lambda bi, c: (0, 0)),
                  pl.BlockSpec((ng, 1), lambda bi, c: (0, 0)),
                  pl.BlockSpec((1, SUB_CHUNKS, w, w), lambda bi, c: (bi, c + sk, 0, 0)),
                  pl.BlockSpec((1, SUB_CHUNKS, 8, w), lambda bi, c: (bi, c + sk, 0, 0)),
                  pl.BlockSpec((1, SUB_CHUNKS, w, w), lambda bi, c: (bi, c + sk, 0, 0)),
                  pl.BlockSpec((1, SUB_CHUNKS, 8, w), lambda bi, c: (bi, c + sk, 0, 0)),
                  pl.BlockSpec((1, w), lambda bi, c: (0, 0)),
                  pl.BlockSpec((w, w), lambda bi, c: (0, 0))],
        out_specs=pl.BlockSpec((1, rows, w), lambda bi, c: (bi, c, 0)),
        out_shape=jax.ShapeDtypeStruct((b, n_groups * rows, w), F32),
        compiler_params=_cparams("parallel", "parallel"),
        name="mlstm_out",
    )(qk, m, gcol, grow, bias_row, bias_col, cf, nmf, cb, nmb, norm_w, gmat)


def _proj_out_kernel(att_ref, ret_ref, ml_ref, h_ref, mod_ref, w_ref, o_ref):
    aw, rw = att_ref.shape[2], ret_ref.shape[2]
    y = jnp.dot(att_ref[0].astype(BF16), w_ref[0:aw, :], preferred_element_type=F32)
    y = y + jnp.dot(ret_ref[0].astype(BF16), w_ref[aw:aw + rw, :], preferred_element_type=F32)
    y = y + jnp.dot(ml_ref[0].astype(BF16), w_ref[aw + rw:, :], preferred_element_type=F32)
    o_ref[0] = h_ref[0] + mod_ref[0, 2:3, :] * y


def _mod_map(n_ctx_tiles, ctx_row, skip):
    return lambda bi, i: (jnp.where(i + skip < n_ctx_tiles, ctx_row, bi), 0, 0)


def proj_out(att, ret, ml, h, mod, w_out, *, n_ctx, skip):
    b, s, d = h.shape
    tm = TOKEN_TILE
    nt = s // tm - skip
    tok = lambda bi, i: (bi, i, 0)
    return pl.pallas_call(
        _proj_out_kernel,
        grid=(b, nt),
        in_specs=[pl.BlockSpec((1, tm, att.shape[2]), tok),
                  pl.BlockSpec((1, tm, ret.shape[2]), tok),
                  pl.BlockSpec((1, tm, ml.shape[2]), tok),
                  pl.BlockSpec((1, tm, d), lambda bi, i: (bi, i + skip, 0)),
                  pl.BlockSpec((1, 6, d), _mod_map(n_ctx // tm, mod.shape[0] - 1, skip)),
                  pl.BlockSpec(w_out.shape, lambda bi, i: (0, 0))],
        out_specs=pl.BlockSpec((1, tm, d), tok),
        out_shape=jax.ShapeDtypeStruct((b, nt * tm, d), F32),
        compiler_params=_cparams("parallel", "parallel"),
        name="proj_out",
    )(att, ret, ml, h, mod, w_out)


def _ffn_kernel(h_ref, mod_ref, nw_ref, wg_ref, wu_ref, wd_ref, o_ref):
    x = h_ref[0]
    a = _norm_mod(x, nw_ref[...], mod_ref[0, 3:4, :], mod_ref[0, 4:5, :]).astype(BF16)
    hg = jnp.dot(a, wg_ref[...], preferred_element_type=F32)
    hu = jnp.dot(a, wu_ref[...], preferred_element_type=F32)
    act = (_silu(hg) * hu).astype(BF16)
    y = jnp.dot(act, wd_ref[...], preferred_element_type=F32)
    o_ref[0] = x + mod_ref[0, 5:6, :] * y


def ffn_dense(h, mod, norm_w, wg, wu, wd, *, n_ctx):
    b, s, d = h.shape
    tm = TOKEN_TILE
    tok = lambda bi, i: (bi, i, 0)
    const = lambda bi, i: (0, 0)
    resident = functools.partial(pl.BlockSpec, pipeline_mode=pl.Buffered(1))
    return pl.pallas_call(
        _ffn_kernel,
        grid=(b, s // tm),
        in_specs=[pl.BlockSpec((1, tm, d), tok),
                  pl.BlockSpec((1, 6, d), _mod_map(n_ctx // tm, mod.shape[0] - 1, 0)),
                  pl.BlockSpec((1, d), const),
                  resident(wg.shape, const),
                  resident(wu.shape, const),
                  resident(wd.shape, const)],
        out_specs=pl.BlockSpec((1, tm, d), tok),
        out_shape=jax.ShapeDtypeStruct((b, s, d), F32),
        compiler_params=_cparams("parallel", "parallel"),
        name="ffn_dense",
    )(h, mod, norm_w, wg, wu, wd)


def _moe_pre_kernel(h_ref, mod_ref, nw_ref, rw_ref, rb_ref, xn_ref, route_ref, *, n_experts):
    x = h_ref[0]
    a = _norm_mod(x, nw_ref[...], mod_ref[0, 3:4, :], mod_ref[0, 4:5, :])
    a_hi = a.astype(BF16)
    xn_ref[0] = a_hi
    a_lo = (a - a_hi.astype(F32)).astype(BF16)
    rw = rw_ref[...]
    w_hi = rw.astype(BF16)
    w_lo = (rw - w_hi.astype(F32)).astype(BF16)
    logits = (jnp.dot(a_hi, w_hi, preferred_element_type=F32) + jnp.dot(a_lo, w_hi, preferred_element_type=F32)
              + jnp.dot(a_hi, w_lo, preferred_element_type=F32)) + rb_ref[...]
    lane = lax.broadcasted_iota(jnp.int32, logits.shape, 1)
    l1 = jnp.where(lane < n_experts, logits, -jnp.inf)
    v1 = jnp.max(l1, axis=-1, keepdims=True)
    i1 = jnp.min(jnp.where(l1 == v1, lane, 128), axis=-1, keepdims=True)
    l2 = jnp.where(lane == i1, -jnp.inf, l1)
    v2 = jnp.max(l2, axis=-1, keepdims=True)
    i2 = jnp.min(jnp.where(l2 == v2, lane, 128), axis=-1, keepdims=True)
    e2 = jnp.exp(v2 - v1)
    w1 = 1.0 / (1.0 + e2)
    w2 = e2 * w1
    lane8 = lax.broadcasted_iota(jnp.int32, route_ref.shape[1:], 1)
    route = jnp.where(lane8 == 0, i1.astype(F32),
                      jnp.where(lane8 == 1, i2.astype(F32),
                                jnp.where(lane8 == 2, w1, jnp.where(lane8 == 3, w2, 0.0))))
    route_ref[0] = route


def moe_pre(h, mod, norm_w, router_w_pad, router_b_pad, *, n_experts):
    b, t, d = h.shape
    tm = TOKEN_TILE
    tok = lambda bi, i: (bi, i, 0)
    const = lambda bi, i: (0, 0)
    return pl.pallas_call(
        functools.partial(_moe_pre_kernel, n_experts=n_experts),
        grid=(b, t // tm),
        in_specs=[pl.BlockSpec((1, tm, d), tok),
                  pl.BlockSpec((1, 6, d), lambda bi, i: (bi, 0, 0)),
                  pl.BlockSpec((1, d), const),
                  pl.BlockSpec(router_w_pad.shape, const),
                  pl.BlockSpec((1, 128), const)],
        out_specs=(pl.BlockSpec((1, tm, d), tok), pl.BlockSpec((1, tm, 8), tok)),
        out_shape=(jax.ShapeDtypeStruct((b, t, d), BF16), jax.ShapeDtypeStruct((b, t, 8), F32)),
        compiler_params=_cparams("parallel", "parallel"),
        name="moe_pre",
    )(h, mod, norm_w, router_w_pad, router_b_pad)


def _moe_gmm_kernel(be_ref, nu_ref, x_ref, wg_ref, wu_ref, wd_ref, o_ref, acc_sc):
    i, j = pl.program_id(0), pl.program_id(1)
    nj = pl.num_programs(1)
    used = i < nu_ref[0]

    @pl.when(used)
    def _():
        x = x_ref[...]
        hg = jnp.dot(x, wg_ref[0], preferred_element_type=F32)
        hu = jnp.dot(x, wu_ref[0], preferred_element_type=F32)
        y = jnp.dot((_silu(hg) * hu).astype(BF16), wd_ref[0], preferred_element_type=F32)

        @pl.when(j == 0)
        def _():
            acc_sc[...] = y

        @pl.when(j > 0)
        def _():
            acc_sc[...] += y

    @pl.when(j == nj - 1)
    def _():
        o_ref[...] = jnp.where(used, acc_sc[...], 0.0)


def moe_gmm(block_e, n_used, xb, wg, wu, wd):
    n_slots, d = xb.shape
    tm, tf = MOE_TILE, MOE_FF_TILE
    n_blocks = n_slots // tm
    f = wg.shape[2]
    nj = f // tf
    last = nj - 1

    def jj(i, j, nu):
        return jnp.where(i < nu[0], j, last)

    grid_spec = pltpu.PrefetchScalarGridSpec(
        num_scalar_prefetch=2,
        grid=(n_blocks, nj),
        in_specs=[pl.BlockSpec((tm, d), lambda i, j, be, nu: (i, 0)),
                  pl.BlockSpec((1, d, tf), lambda i, j, be, nu: (be[i], 0, jj(i, j, nu))),
                  pl.BlockSpec((1, d, tf), lambda i, j, be, nu: (be[i], 0, jj(i, j, nu))),
                  pl.BlockSpec((1, tf, d), lambda i, j, be, nu: (be[i], jj(i, j, nu), 0))],
        out_specs=pl.BlockSpec((tm, d), lambda i, j, be, nu: (i, 0)),
        scratch_shapes=[pltpu.VMEM((tm, d), F32)],
    )
    return pl.pallas_call(
        _moe_gmm_kernel,
        grid_spec=grid_spec,
        out_shape=jax.ShapeDtypeStruct((n_slots, d), F32),
        compiler_params=_cparams("arbitrary", "arbitrary"),
        name="moe_gmm",
    )(block_e, n_used, xb, wg, wu, wd)


def _moe_combine_kernel(h_ref, mod_ref, route_ref, y1_ref, y2_ref, o_ref):
    r = route_ref[0]
    y = r[:, 2:3] * y1_ref[0] + r[:, 3:4] * y2_ref[0]
    o_ref[0] = h_ref[0] + mod_ref[0, 5:6, :] * y


def moe_combine(h, mod, route, y1, y2):
    b, t, d = h.shape
    tm = TOKEN_TILE
    tok = lambda bi, i: (bi, i, 0)
    return pl.pallas_call(
        _moe_combine_kernel,
        grid=(b, t // tm),
        in_specs=[pl.BlockSpec((1, tm, d), tok),
                  pl.BlockSpec((1, 6, d), lambda bi, i: (bi, 0, 0)),
                  pl.BlockSpec((1, tm, 8), tok),
                  pl.BlockSpec((1, tm, d), tok),
                  pl.BlockSpec((1, tm, d), tok)],
        out_specs=pl.BlockSpec((1, tm, d), tok),
        out_shape=jax.ShapeDtypeStruct((b, t, d), F32),
        compiler_params=_cparams("parallel", "parallel"),
        name="moe_combine",
    )(h, mod, route, y1, y2)


def _moe_dispatch_plan(expert_idx, n_experts):
    n = expert_idx.shape[0]
    tile = MOE_TILE
    flat_e = expert_idx.reshape(-1)
    onehot = (flat_e[None, :] == jnp.arange(n_experts, dtype=jnp.int32)[:, None]).astype(jnp.int32)
    csum = jnp.cumsum(onehot, axis=1)
    counts = csum[:, -1]
    padded = (counts + tile - 1) // tile * tile
    pad_ends = jnp.cumsum(padded)
    pad_starts = pad_ends - padded
    slot = jnp.sum(onehot * (pad_starts[:, None] + csum - 1), axis=0)
    n_blocks = -(-(n * TOP_K) // tile) + n_experts
    n_used = pad_ends[-1] // tile
    blk = jnp.arange(n_blocks, dtype=jnp.int32)
    block_e = jnp.minimum(jnp.searchsorted(pad_ends, blk * tile, side="right"), n_experts - 1).astype(jnp.int32)
    block_e = jnp.where(blk < n_used, block_e, block_e[jnp.maximum(n_used - 1, 0)])
    slot_tok = jnp.zeros((n_blocks * tile,), jnp.int32).at[slot].set(jnp.arange(n * TOP_K, dtype=jnp.int32) // TOP_K)
    return slot.reshape(n, TOP_K), slot_tok, block_e, n_used.astype(jnp.int32).reshape(1)


def _rope_tables(n_ctx, n_lat):
    rows = n_lat // GRID_W
    row = jnp.repeat(jnp.arange(rows, dtype=F32), GRID_W)
    col = jnp.tile(jnp.arange(GRID_W, dtype=F32), rows)
    n_freq = HEAD_DIM // 4
    inv_freq = ROPE_THETA ** (-jnp.arange(n_freq, dtype=F32) / n_freq)
    ang = jnp.concatenate([row[:, None] * inv_freq, col[:, None] * inv_freq], -1)
    cos, sin = jnp.cos(ang), jnp.sin(ang)
    cos_t = jnp.concatenate([cos, cos, cos, cos], axis=1)
    sin_t = jnp.concatenate([-sin, sin, -sin, sin], axis=1)
    cos_t = jnp.concatenate([jnp.ones((n_ctx, 128), F32), cos_t], axis=0)
    sin_t = jnp.concatenate([jnp.zeros((n_ctx, 128), F32), sin_t], axis=0)
    return cos_t, sin_t


def _block_ones(n):
    idx = jnp.arange(n) // HEAD_DIM
    return (idx[:, None] == idx[None, :]).astype(BF16)


def kernel(x, c, ctx, c_ctx, mod_w, mod_b, norm1_w, norm2_w, w_in, attn_qn_w, attn_kn_w, ret_decay, ret_norm_w,
           mlstm_conv_w, mlstm_gate_b, mlstm_norm_w, w_out, ffn_w_gate, ffn_w_up, ffn_w_down, router_w, router_b,
           moe_w_gate, moe_w_up, moe_w_down):
    b, n_lat, d = x.shape
    n_ctx = ctx.shape[1]
    depth = mod_w.shape[0]
    n_experts = router_w.shape[2]
    ret_heads = ret_decay.shape[2]
    ml_heads = mlstm_gate_b.shape[2]
    ret_w, ml_w = ret_heads * HEAD_DIM, ml_heads * HEAD_DIM
    n_gate = N_GATES * ml_heads
    kv_w = (w_in.shape[2] - 4 * ret_w - 4 * ml_w - n_gate) // 6
    att_w = 4 * kv_w
    dims = (att_w, kv_w, ret_w, ml_w, n_gate)
    assert n_ctx % TOKEN_TILE == 0 and n_lat % KV_TILE == 0 and b + 1 <= 16
    n_main = w_in.shape[2] - n_gate
    n_pad = -(-w_in.shape[2] // 128) * 128

    cos_t, sin_t = _rope_tables(n_ctx, n_lat)
    gq, gk, g_ret, g_ml = _block_ones(att_w), _block_ones(kv_w), _block_ones(ret_w), _block_ones(ml_w)
    cond = jnp.concatenate([c, c_ctx[None, :], jnp.zeros((16 - b - 1, d), F32)], axis=0)
    ctx_tiles = n_ctx // TOKEN_TILE
    ctx_chunks = n_ctx // CHUNK

    h = jnp.concatenate([ctx, x], axis=1)
    for layer in range(depth):
        last = layer == depth - 1
        skip = ctx_tiles if last else 0
        mod = adaln(cond, mod_w[layer], mod_b[layer]).reshape(16, 6, d)[:b + 1]
        w_all = jnp.pad(w_in[layer].astype(BF16), ((0, 0), (0, n_pad - w_in.shape[2])))
        w_gate_t = w_in[layer][:, n_main:].T.astype(BF16)
        qn = jnp.tile(attn_qn_w[layer], att_w // HEAD_DIM)[None, :]
        kn = jnp.tile(attn_kn_w[layer], kv_w // HEAD_DIM)[None, :]
        k_norm_max = math.sqrt(HEAD_DIM) * jnp.max(jnp.abs(attn_kn_w[layer])) * (1.0 + 2.0 ** -6)
        q_norm_max = math.sqrt(HEAD_DIM) * jnp.max(jnp.abs(attn_qn_w[layer])) * (QK_SCALE * LOG2E)
        fast = (2.0 * q_norm_max * k_norm_max <= MAX_SCORE_SPAN_LOG2).astype(jnp.int32).reshape(1)
        kbound = jnp.full((1, att_w), k_norm_max, F32)
        q, k, v, r, m, gcol, grow = proj_in(h, mod, norm1_w[layer][None, :], cos_t, sin_t, w_all, w_gate_t, qn, kn,
                                            kbound, gq, gk, n_ctx=n_ctx, dims=dims)
        att = attention(fast, q, k, v, n_ctx=n_ctx, skip=skip)
        dec = jnp.repeat(ret_decay[layer], HEAD_DIM, axis=1)
        sf, sb = ret_states(r, dec, n_ctx=n_ctx)
        ret = ret_out(r, sf, sb, dec, ret_norm_w[layer][None, :], g_ret, skip_chunks=ctx_chunks if last else 0)
        qk = mlstm_conv(m, mlstm_conv_w[layer], n_ctx=n_ctx)
        bias = mlstm_gate_b[layer].reshape(-1)
        cf, nmf, cb, nmb = mlstm_states(qk, m, gcol, bias[None, :], n_ctx=n_ctx, n_heads=ml_heads)
        ml = mlstm_out(qk, m, gcol, grow, bias[None, :], bias[:, None], cf, nmf, cb, nmb, mlstm_norm_w[layer][None, :],
                       g_ml, skip_chunks=ctx_chunks if last else 0, n_heads=ml_heads)
        h = proj_out(att, ret, ml, h, mod, w_out[layer].astype(BF16), n_ctx=n_ctx, skip=skip)
        if layer % 2 == 0:
            li = layer // 2
            assert not last, "dense FFN on a latent-only stream is not wired up"
            h = ffn_dense(h, mod, norm2_w[layer][None, :], ffn_w_gate[li].astype(BF16), ffn_w_up[li].astype(BF16),
                          ffn_w_down[li].astype(BF16), n_ctx=n_ctx)
        else:
            li = layer // 2
            assert last, "expert FFN on the combined stream is not wired up"
            mod_lat = mod[:b]
            rw = jnp.pad(router_w[li], ((0, 0), (0, 128 - n_experts)))
            rb = jnp.pad(router_b[li], (0, 128 - n_experts))[None, :]
            xn, route = moe_pre(h, mod_lat, norm2_w[layer][None, :], rw, rb, n_experts=n_experts)
            t = h.shape[1]
            route_flat = route.reshape(b * t, 8)
            slot, slot_tok, block_e, n_used = _moe_dispatch_plan(route_flat[:, 0:TOP_K].astype(jnp.int32), n_experts)
            xb = jnp.take(xn.reshape(b * t, d), slot_tok, axis=0)
            yb = moe_gmm(block_e, n_used, xb, moe_w_gate[li].astype(BF16), moe_w_up[li].astype(BF16),
                         moe_w_down[li].astype(BF16))
            y1 = jnp.take(yb, slot[:, 0], axis=0).reshape(b, t, d)
            y2 = jnp.take(yb, slot[:, 1], axis=0).reshape(b, t, d)
            h = moe_combine(h, mod_lat, route, y1, y2)
    return h if h.shape[1] == n_lat else h[:, n_ctx:, :]
```

```python
import functools
import math

import jax
import jax.numpy as jnp
from jax import lax
from jax.experimental import pallas as pl
from jax.experimental.pallas import tpu as pltpu

F32 = jnp.float32
BF16 = jnp.bfloat16

HEAD_DIM = 64
GRID_W = 64
ROPE_THETA = 10000.0
EPS = 1e-6
CHUNK = 128
SUB_CHUNKS = 2
TOKEN_TILE = 256
KV_TILE = 1024
N_GATES = 4
TOP_K = 2
MOE_TILE = 512
MOE_FF_TILE = 1792
VMEM_LIMIT = 56 * 1024 * 1024
QK_SCALE = HEAD_DIM ** -0.5
LOG2E = math.log2(math.e)
MAX_SCORE_SPAN_LOG2 = 100.0


def _cparams(*sem):
    return pltpu.CompilerParams(dimension_semantics=sem, vmem_limit_bytes=VMEM_LIMIT)


def _silu(x):
    return x * jax.nn.sigmoid(x)


def _split_dot(x, ones_bf16, parts=2):
    acc = None
    r = x
    for _ in range(parts):
        hi = r.astype(BF16)
        d = jnp.dot(hi, ones_bf16, preferred_element_type=F32)
        acc = d if acc is None else acc + d
        r = r - hi.astype(F32)
    return acc


def _dot_nt(a, b):
    return lax.dot_general(a, b, (((1,), (1,)), ((), ())), preferred_element_type=F32)


def _rope(x, cos_t, sin_t):
    w = x.shape[1]
    reps = w // 128
    c = jnp.concatenate([cos_t] * reps, axis=1) if reps > 1 else cos_t
    s = jnp.concatenate([sin_t] * reps, axis=1) if reps > 1 else sin_t
    lane = lax.broadcasted_iota(jnp.int32, x.shape, 1)
    first_half = (lane % HEAD_DIM) < (HEAD_DIM // 2)
    swapped = jnp.where(first_half, pltpu.roll(x, w - HEAD_DIM // 2, 1), pltpu.roll(x, HEAD_DIM // 2, 1))
    return x * c + swapped * s


def _group_rms(x, gmat, gain):
    x2 = x * x
    blk = 256
    if x.shape[1] > blk and x.shape[1] % blk == 0:
        ms = jnp.concatenate([_split_dot(x2[:, c:c + blk], gmat[c:c + blk, c:c + blk])
                              for c in range(0, x.shape[1], blk)], axis=1)
    else:
        ms = _split_dot(x2, gmat)
    return x * lax.rsqrt(ms * (1.0 / HEAD_DIM) + EPS) * gain


def _group_layer_norm(x, gmat, gain):
    mu = _split_dot(x, gmat) * (1.0 / HEAD_DIM)
    xc = x - mu
    var = _split_dot(xc * xc, gmat) * (1.0 / HEAD_DIM)
    return xc * lax.rsqrt(var + EPS) * gain


def _expand_heads(x, n_heads, lane):
    out = x[:, n_heads - 1:n_heads]
    for h in range(n_heads - 2, -1, -1):
        out = jnp.where(lane < (h + 1) * HEAD_DIM, x[:, h:h + 1], out)
    return out


def _block_diag_mask(n):
    r = lax.broadcasted_iota(jnp.int32, (n, n), 0) // HEAD_DIM
    c = lax.broadcasted_iota(jnp.int32, (n, n), 1) // HEAD_DIM
    return r == c


def _adaln_kernel(c_ref, w_ref, b_ref, o_ref):
    a = _silu(c_ref[...]).astype(BF16)
    o_ref[...] = jnp.dot(a, w_ref[...].astype(BF16), preferred_element_type=F32) + b_ref[...]


def adaln(cond, w, b):
    r, d = cond.shape
    n = w.shape[1]
    tn = 1536 if n % 1536 == 0 else n
    return pl.pallas_call(
        _adaln_kernel,
        grid=(n // tn,),
        in_specs=[pl.BlockSpec((r, d), lambda j: (0, 0)),
                  pl.BlockSpec((d, tn), lambda j: (0, j)),
                  pl.BlockSpec((1, tn), lambda j: (0, j))],
        out_specs=pl.BlockSpec((r, tn), lambda j: (0, j)),
        out_shape=jax.ShapeDtypeStruct((r, n), F32),
        compiler_params=_cparams("arbitrary"),
        name="adaln",
    )(cond, w, b.reshape(1, n))


def _norm_mod(x, nw, shift, scale):
    ms = jnp.mean(x * x, axis=-1, keepdims=True)
    y = x * lax.rsqrt(ms + EPS) * nw
    return y * (1.0 + scale) + shift


def _proj_in_kernel(h_ref, mod_ref, nw_ref, cos_ref, sin_ref, w_ref, wgt_ref, qn_ref, kn_ref, kb_ref, gq_ref, gk_ref,
                    q_ref, k_ref, v_ref, r_ref, m_ref, gc_ref, gr_ref, *, dims):
    att_w, kv_w, ret_w, ml_w, n_gate = dims
    x = h_ref[0]
    a = _norm_mod(x, nw_ref[...], mod_ref[0, 0:1, :], mod_ref[0, 1:2, :]).astype(BF16)
    p = jnp.dot(a, w_ref[...], preferred_element_type=F32)
    cos_t, sin_t = cos_ref[...], sin_ref[...]
    o = 0
    aq = p[:, o:o + att_w]; o += att_w
    ak = p[:, o:o + kv_w]; o += kv_w
    av = p[:, o:o + kv_w]; o += kv_w
    aq = _rope(_group_rms(aq, gq_ref[...], qn_ref[...]), cos_t, sin_t) * (QK_SCALE * LOG2E)
    ak = _rope(_group_rms(ak, gk_ref[...], kn_ref[...]), cos_t, sin_t)
    neg_bound = -kb_ref[...]
    tm = x.shape[0]
    lane = lax.broadcasted_iota(jnp.int32, (tm, 2 * HEAD_DIM), 1)

    def widen(a, extra):
        return jnp.where(lane == HEAD_DIM, extra, jnp.concatenate([a, jnp.zeros_like(a)], axis=1)).astype(BF16)

    for hh in range(att_w // HEAD_DIM):
        cols = slice(hh * HEAD_DIM, (hh + 1) * HEAD_DIM)
        q_ref[0, hh] = widen(aq[:, cols], neg_bound[:, hh * HEAD_DIM:hh * HEAD_DIM + 1])
    for hh in range(kv_w // HEAD_DIM):
        cols = slice(hh * HEAD_DIM, (hh + 1) * HEAD_DIM)
        k_ref[0, hh] = widen(ak[:, cols], 1.0)
        v_ref[0, hh] = widen(av[:, cols], 1.0)
    rq = _rope(p[:, o:o + ret_w], cos_t, sin_t)
    rk = _rope(p[:, o + ret_w:o + 2 * ret_w] * QK_SCALE, cos_t, sin_t)
    r_ref[0, :, 0:ret_w] = rq
    r_ref[0, :, ret_w:2 * ret_w] = rk
    r_ref[0, :, 2 * ret_w:4 * ret_w] = p[:, o + 2 * ret_w:o + 4 * ret_w]
    o += 4 * ret_w
    m_ref[0] = p[:, o:o + 4 * ml_w]
    o += 4 * ml_w
    gc_ref[0] = p[:, o:o + n_gate]
    gr_ref[0] = _dot_nt(wgt_ref[...], a)


def proj_in(h, mod, norm_w, cos_t, sin_t, w_all, w_gate_t, qn, kn, kbound, gq, gk, *, n_ctx, dims):
    att_w, kv_w, ret_w, ml_w, n_gate = dims
    b, s, d = h.shape
    tm = TOKEN_TILE
    n_ctx_tiles = n_ctx // tm
    n_pad = w_all.shape[1]
    ctx_row = mod.shape[0] - 1

    def mod_map(bi, i):
        return (jnp.where(i < n_ctx_tiles, ctx_row, bi), 0, 0)

    const2 = lambda bi, i: (0, 0)
    out_shape = (
        jax.ShapeDtypeStruct((b, att_w // HEAD_DIM, s, 2 * HEAD_DIM), BF16),
        jax.ShapeDtypeStruct((b, kv_w // HEAD_DIM, s, 2 * HEAD_DIM), BF16),
        jax.ShapeDtypeStruct((b, kv_w // HEAD_DIM, s, 2 * HEAD_DIM), BF16),
        jax.ShapeDtypeStruct((b, s, 4 * ret_w), F32),
        jax.ShapeDtypeStruct((b, s, 4 * ml_w), F32),
        jax.ShapeDtypeStruct((b, s, n_gate), F32),
        jax.ShapeDtypeStruct((b, n_gate, s), F32),
    )
    return pl.pallas_call(
        functools.partial(_proj_in_kernel, dims=dims),
        grid=(b, s // tm),
        in_specs=[
            pl.BlockSpec((1, tm, d), lambda bi, i: (bi, i, 0)),
            pl.BlockSpec((1, 6, d), mod_map),
            pl.BlockSpec((1, d), const2),
            pl.BlockSpec((tm, 128), lambda bi, i: (i, 0)),
            pl.BlockSpec((tm, 128), lambda bi, i: (i, 0)),
            pl.BlockSpec((d, n_pad), const2),
            pl.BlockSpec((n_gate, d), const2),
            pl.BlockSpec((1, att_w), const2),
            pl.BlockSpec((1, kv_w), const2),
            pl.BlockSpec((1, att_w), const2),
            pl.BlockSpec((att_w, att_w), const2),
            pl.BlockSpec((kv_w, kv_w), const2),
        ],
        out_specs=(
            pl.BlockSpec((1, att_w // HEAD_DIM, tm, 2 * HEAD_DIM), lambda bi, i: (bi, 0, i, 0)),
            pl.BlockSpec((1, kv_w // HEAD_DIM, tm, 2 * HEAD_DIM), lambda bi, i: (bi, 0, i, 0)),
            pl.BlockSpec((1, kv_w // HEAD_DIM, tm, 2 * HEAD_DIM), lambda bi, i: (bi, 0, i, 0)),
            pl.BlockSpec((1, tm, 4 * ret_w), lambda bi, i: (bi, i, 0)),
            pl.BlockSpec((1, tm, 4 * ml_w), lambda bi, i: (bi, i, 0)),
            pl.BlockSpec((1, tm, n_gate), lambda bi, i: (bi, i, 0)),
            pl.BlockSpec((1, n_gate, tm), lambda bi, i: (bi, 0, i)),
        ),
        out_shape=out_shape,
        compiler_params=_cparams("parallel", "parallel"),
        name="proj_in",
    )(h, mod, norm_w, cos_t, sin_t, w_all, w_gate_t, qn, kn, kbound, gq, gk)


def _conv_kernel(x_ref, prev_ref, next_ref, w_ref, o_ref, *, n_ctx, seq, half):
    i = pl.program_id(1)
    x = x_ref[0]
    tm, w = x.shape
    row = lax.broadcasted_iota(jnp.int32, (tm, 1), 0)
    grow = row + i * tm
    xp = jnp.where(row == 0, prev_ref[0, 7:8, :], pltpu.roll(x, 1, 0))
    xn = jnp.where(row == tm - 1, next_ref[0, 0:1, :], pltpu.roll(x, tm - 1, 0))
    xp = jnp.where((grow == 0) | (grow == n_ctx), 0.0, xp)
    xn = jnp.where((grow == n_ctx - 1) | (grow == seq - 1), 0.0, xn)
    y = _silu(xp * w_ref[0:1, :] + x * w_ref[1:2, :] + xn * w_ref[2:3, :])
    lane = lax.broadcasted_iota(jnp.int32, (tm, w), 1)
    o_ref[0] = jnp.where(lane >= half, y * QK_SCALE, y)


def mlstm_conv(m, conv_w, *, n_ctx):
    b, s, _ = m.shape
    w = conv_w.shape[1]
    tm = TOKEN_TILE
    r8 = tm // 8
    last8 = s // 8 - 1
    return pl.pallas_call(
        functools.partial(_conv_kernel, n_ctx=n_ctx, seq=s, half=w // 2),
        grid=(b, s // tm),
        in_specs=[pl.BlockSpec((1, tm, w), lambda bi, i: (bi, i, 0)),
                  pl.BlockSpec((1, 8, w), lambda bi, i: (bi, jnp.maximum(i * r8 - 1, 0), 0)),
                  pl.BlockSpec((1, 8, w), lambda bi, i: (bi, jnp.minimum((i + 1) * r8, last8), 0)),
                  pl.BlockSpec((3, w), lambda bi, i: (0, 0))],
        out_specs=pl.BlockSpec((1, tm, w), lambda bi, i: (bi, i, 0)),
        out_shape=jax.ShapeDtypeStruct((b, s, w), F32),
        compiler_params=_cparams("parallel", "parallel"),
        name="mlstm_conv",
    )(m, m, m, conv_w)


def _attn_kernel(fast_ref, q_ref, k_ref, v_ref, o_ref, m_sc, acc_sc, *, n_ctx, n_lat_tiles, skip):
    i = pl.program_id(2) + skip
    g, tq = q_ref.shape[1], q_ref.shape[2]
    hd = HEAD_DIM

    def fast_step(k, v):
        for h in range(g):
            rows = slice(h * tq, (h + 1) * tq)
            p = jnp.exp2(_dot_nt(q_ref[0, h], k))
            acc_sc[rows] += jnp.dot(p.astype(BF16), v, preferred_element_type=F32)

    def safe_step(k, v):
        for h in range(g):
            rows = slice(h * tq, (h + 1) * tq)
            s = _dot_nt(q_ref[0, h], k)
            m_old = m_sc[rows]
            m_new = jnp.maximum(m_old, jnp.max(s, axis=-1, keepdims=True))
            p = jnp.exp2(s - jnp.concatenate([m_new] * (s.shape[1] // 128), axis=1))
            acc_sc[rows] = jnp.exp2(m_old - m_new) * acc_sc[rows] + jnp.dot(p.astype(BF16), v, preferred_element_type=F32)
            m_sc[rows] = m_new

    def sweep(step, unroll):
        step(k_ref[0, 0, 0:n_ctx, :], v_ref[0, 0, 0:n_ctx, :])

        @pl.when(i * tq >= n_ctx)
        def _():
            def body(j, carry):
                start = pl.multiple_of(n_ctx + j * KV_TILE, math.gcd(n_ctx, KV_TILE))
                step(k_ref[0, 0, pl.ds(start, KV_TILE), :], v_ref[0, 0, pl.ds(start, KV_TILE), :])
                return carry
            lax.fori_loop(0, n_lat_tiles, body, 0, unroll=unroll)

    acc_sc[...] = jnp.zeros(acc_sc.shape, F32)

    @pl.when(fast_ref[0] == 1)
    def _():
        sweep(fast_step, 4)

    @pl.when(fast_ref[0] != 1)
    def _():
        m_sc[...] = jnp.full(m_sc.shape, -jnp.inf, F32)
        sweep(safe_step, 1)

    acc = acc_sc[...]
    out = acc[:, 0:hd] / acc[:, hd:hd + 1]
    o_ref[0] = jnp.concatenate([out[h * tq:(h + 1) * tq] for h in range(g)], axis=1)


def attention(fast, q, k, v, *, n_ctx, skip):
    b, hq, s, lanes = q.shape
    hkv = k.shape[1]
    g = hq // hkv
    tq = TOKEN_TILE
    n_lat = s - n_ctx
    assert n_lat % KV_TILE == 0 and n_ctx % tq == 0 and lanes == 2 * HEAD_DIM
    nq = s // tq - skip
    grid_spec = pltpu.PrefetchScalarGridSpec(
        num_scalar_prefetch=1,
        grid=(b, hkv, nq),
        in_specs=[pl.BlockSpec((1, g, tq, lanes), lambda bi, gi, i, f: (bi, gi, i + skip, 0)),
                  pl.BlockSpec((1, 1, s, lanes), lambda bi, gi, i, f: (bi, gi, 0, 0)),
                  pl.BlockSpec((1, 1, s, lanes), lambda bi, gi, i, f: (bi, gi, 0, 0))],
        out_specs=pl.BlockSpec((1, tq, g * HEAD_DIM), lambda bi, gi, i, f: (bi, i, gi)),
        scratch_shapes=[pltpu.VMEM((g * tq, 128), F32), pltpu.VMEM((g * tq, lanes), F32)],
    )
    return pl.pallas_call(
        functools.partial(_attn_kernel, n_ctx=n_ctx, n_lat_tiles=n_lat // KV_TILE, skip=skip),
        grid_spec=grid_spec,
        out_shape=jax.ShapeDtypeStruct((b, nq * tq, hq * HEAD_DIM), F32),
        compiler_params=_cparams("parallel", "parallel", "arbitrary"),
        name="attention",
    )(fast, q, k, v)


def _rev_chunk(c, nc_ctx, nc):
    return jnp.where(c < nc_ctx, nc_ctx - 1 - c, nc - 1 + nc_ctx - c)


def _ret_state_kernel(kf_ref, vf_ref, kb_ref, vb_ref, dec_ref, sf_ref, sb_ref, s_sc):
    c = pl.program_id(1)

    @pl.when(c == 0)
    def _():
        s_sc[...] = jnp.zeros(s_sc.shape, F32)

    w = kf_ref.shape[2]
    lg = jnp.log1p(-jnp.exp(dec_ref[...]))
    pos = lax.broadcasted_iota(jnp.int32, (CHUNK, 1), 0).astype(F32)
    bd = _block_diag_mask(w)
    for d, (k_ref, v_ref, out_ref) in enumerate(((kf_ref, vf_ref, sf_ref), (kb_ref, vb_ref, sb_ref))):
        lgd = lg[d:d + 1, :]
        e = (CHUNK - 1.0 - pos) if d == 0 else pos
        kscale = jnp.exp(e * lgd)
        chunk_decay = jnp.exp(CHUNK * lgd)
        for sub in (range(SUB_CHUNKS) if d == 0 else reversed(range(SUB_CHUNKS))):
            rows = slice(sub * CHUNK, (sub + 1) * CHUNK)
            kdec = k_ref[0, rows, :] * kscale
            inc = jnp.dot(kdec.T.astype(BF16), v_ref[0, rows, :].astype(BF16), preferred_element_type=F32)
            s_old = s_sc[d]
            out_ref[0, sub] = s_old.astype(BF16)
            s_sc[d] = chunk_decay * s_old + jnp.where(bd, inc, 0.0)


def ret_states(r, dec, *, n_ctx):
    b, s, w4 = r.shape
    w = w4 // 4
    nc, nc_ctx = s // CHUNK, n_ctx // CHUNK
    assert nc % SUB_CHUNKS == 0 and nc_ctx % SUB_CHUNKS == 0
    ng, rows = nc // SUB_CHUNKS, SUB_CHUNKS * CHUNK
    rev = lambda c: _rev_chunk(c, nc_ctx // SUB_CHUNKS, ng)
    st = jax.ShapeDtypeStruct((b, nc, w, w), BF16)
    return pl.pallas_call(
        _ret_state_kernel,
        grid=(b, ng),
        in_specs=[pl.BlockSpec((1, rows, w), lambda bi, c: (bi, c, 1)),
                  pl.BlockSpec((1, rows, w), lambda bi, c: (bi, c, 2)),
                  pl.BlockSpec((1, rows, w), lambda bi, c: (bi, rev(c), 1)),
                  pl.BlockSpec((1, rows, w), lambda bi, c: (bi, rev(c), 2)),
                  pl.BlockSpec((2, w), lambda bi, c: (0, 0))],
        out_specs=(pl.BlockSpec((1, SUB_CHUNKS, w, w), lambda bi, c: (bi, c, 0, 0)),
                   pl.BlockSpec((1, SUB_CHUNKS, w, w), lambda bi, c: (bi, rev(c), 0, 0))),
        out_shape=(st, st),
        scratch_shapes=[pltpu.VMEM((2, w, w), F32)],
        compiler_params=_cparams("parallel", "arbitrary"),
        name="ret_states",
    )(r, r, r, r, dec)


def _ret_out_kernel(r_ref, sf_ref, sb_ref, dec_ref, nw_ref, gmat_ref, o_ref):
    w = o_ref.shape[2]
    n_heads = w // HEAD_DIM
    lg = jnp.log1p(-jnp.exp(dec_ref[...]))
    pos = lax.broadcasted_iota(jnp.int32, (CHUNK, 1), 0).astype(F32)
    row = lax.broadcasted_iota(jnp.int32, (CHUNK, CHUNK), 0)
    col = lax.broadcasted_iota(jnp.int32, (CHUNK, CHUNK), 1)
    lane = lax.broadcasted_iota(jnp.int32, (CHUNK, w), 1)
    head_masks = [(lane >= h * HEAD_DIM) & (lane < (h + 1) * HEAD_DIM) for h in range(n_heads)]
    q_scale = []
    decay = [jnp.zeros((CHUNK, CHUNK), F32)] * n_heads
    for d in range(2):
        lgd = lg[d:d + 1, :]
        q_scale.append(jnp.exp(((pos + 1.0) if d == 0 else (CHUNK - pos)) * lgd))
        lag = ((row - col) if d == 0 else (col - row)).astype(F32)
        decay = [decay[h] + jnp.where(lag >= 0, jnp.exp(jnp.maximum(lag, 0.0) * lgd[:, h * HEAD_DIM:h * HEAD_DIM + 1]), 0.0)
                 for h in range(n_heads)]
    for sub in range(SUB_CHUNKS):
        x = r_ref[0, sub * CHUNK:(sub + 1) * CHUNK, :]
        q, k, v, g = x[:, 0:w], x[:, w:2 * w], x[:, 2 * w:3 * w], x[:, 3 * w:4 * w]
        kb, vb = k.astype(BF16), v.astype(BF16)
        total = jnp.dot((q * q_scale[0]).astype(BF16), sf_ref[0, sub], preferred_element_type=F32)
        total = total + jnp.dot((q * q_scale[1]).astype(BF16), sb_ref[0, sub], preferred_element_type=F32)
        for h in range(n_heads):
            sc = _dot_nt(jnp.where(head_masks[h], q, 0.0).astype(BF16), kb)
            pv = jnp.dot((sc * decay[h]).astype(BF16), vb, preferred_element_type=F32)
            total = total + jnp.where(head_masks[h], pv, 0.0)
        o_ref[0, sub * CHUNK:(sub + 1) * CHUNK, :] = _group_layer_norm(total, gmat_ref[...], nw_ref[...]) * _silu(g)


def ret_out(r, sf, sb, dec, norm_w, gmat, *, skip_chunks):
    b, s, w4 = r.shape
    w = w4 // 4
    assert skip_chunks % SUB_CHUNKS == 0 and (s // CHUNK) % SUB_CHUNKS == 0
    sk = skip_chunks // SUB_CHUNKS
    ng, rows = (s // CHUNK) // SUB_CHUNKS - sk, SUB_CHUNKS * CHUNK
    return pl.pallas_call(
        _ret_out_kernel,
        grid=(b, ng),
        in_specs=[pl.BlockSpec((1, rows, w4), lambda bi, c: (bi, c + sk, 0)),
                  pl.BlockSpec((1, SUB_CHUNKS, w, w), lambda bi, c: (bi, c + sk, 0, 0)),
                  pl.BlockSpec((1, SUB_CHUNKS, w, w), lambda bi, c: (bi, c + sk, 0, 0)),
                  pl.BlockSpec((2, w), lambda bi, c: (0, 0)),
                  pl.BlockSpec((1, w), lambda bi, c: (0, 0)),
                  pl.BlockSpec((w, w), lambda bi, c: (0, 0))],
        out_specs=pl.BlockSpec((1, rows, w), lambda bi, c: (bi, c, 0)),
        out_shape=jax.ShapeDtypeStruct((b, ng * rows, w), F32),
        compiler_params=_cparams("parallel", "parallel"),
        name="ret_out",
    )(r, sf, sb, dec, norm_w, gmat)


def _tri_masks():
    row = lax.broadcasted_iota(jnp.int32, (CHUNK, CHUNK), 0)
    col = lax.broadcasted_iota(jnp.int32, (CHUNK, CHUNK), 1)
    return col <= row, col >= row


def _ones_dot(ones_bf16, x, parts=3):
    acc = None
    r = x
    for _ in range(parts):
        hi = r.astype(BF16)
        d = jnp.dot(ones_bf16, hi, preferred_element_type=F32)
        acc = d if acc is None else acc + d
        r = r - hi.astype(F32)
    return acc


def _mlstm_state_kernel(kf_ref, vf_ref, gf_ref, kb_ref, vb_ref, gb_ref, bias_ref,
                        cf_ref, nmf_ref, cb_ref, nmb_ref, c_sc, nm_sc, *, n_heads):
    c = pl.program_id(1)

    @pl.when(c == 0)
    def _():
        c_sc[...] = jnp.zeros(c_sc.shape, F32)
        nm_sc[...] = jnp.zeros(nm_sc.shape, F32)

    w = kf_ref.shape[2]
    hh = n_heads
    tril, triu = _tri_masks()
    lane_t = lax.broadcasted_iota(jnp.int32, (CHUNK, w), 1)
    lane_1 = lax.broadcasted_iota(jnp.int32, (1, w), 1)
    bd = _block_diag_mask(w)
    dirs = ((kf_ref, vf_ref, gf_ref, cf_ref, nmf_ref, tril), (kb_ref, vb_ref, gb_ref, cb_ref, nmb_ref, triu))
    for d, (k_ref, v_ref, g_ref, c_out, nm_out, tmat) in enumerate(dirs):
        for sub in (range(SUB_CHUNKS) if d == 0 else reversed(range(SUB_CHUNKS))):
            rows = slice(sub * CHUNK, (sub + 1) * CHUNK)
            pre = g_ref[0, rows, :] + bias_ref[...]
            ig = pre[:, d * hh:(d + 1) * hh]
            lf_all = jax.nn.log_sigmoid(pre)
            b_all = _ones_dot(tmat.astype(BF16), lf_all)
            lf = lf_all[:, (2 + d) * hh:(3 + d) * hh]
            bcum = b_all[:, (2 + d) * hh:(3 + d) * hh]
            b_end = jnp.sum(lf, axis=0, keepdims=True)
            a = b_end - bcum + ig
            m_loc = jnp.max(a, axis=0, keepdims=True)
            wgt = jnp.exp(a - m_loc)
            kw = k_ref[0, rows, :] * _expand_heads(wgt, hh, lane_t)
            c_inc = jnp.dot(kw.T.astype(BF16), v_ref[0, rows, :].astype(BF16), preferred_element_type=F32)
            n_inc = jnp.sum(kw, axis=0, keepdims=True)
            be = _expand_heads(b_end, hh, lane_1)
            ml = _expand_heads(m_loc, hh, lane_1)
            c_old = c_sc[d]
            n_old = nm_sc[d, 0:1, :]
            m_old = nm_sc[d, 1:2, :]
            c_out[0, sub] = c_old.astype(BF16)
            nm_out[0, sub] = nm_sc[d]
            m_new = jnp.maximum(be + m_old, ml)
            w_prev = jnp.exp(be + m_old - m_new)
            w_inc = jnp.exp(ml - m_new)
            c_sc[d] = w_prev * c_old + w_inc * jnp.where(bd, c_inc, 0.0)
            nm_sc[d, 0:1, :] = w_prev * n_old + w_inc * n_inc
            nm_sc[d, 1:2, :] = m_new


def mlstm_states(qk, m, gcol, bias_row, *, n_ctx, n_heads):
    b, s, w2 = qk.shape
    w = w2 // 2
    ng = gcol.shape[2]
    nc, nc_ctx = s // CHUNK, n_ctx // CHUNK
    assert nc % SUB_CHUNKS == 0 and nc_ctx % SUB_CHUNKS == 0
    n_groups, rows = nc // SUB_CHUNKS, SUB_CHUNKS * CHUNK
    rev = lambda c: _rev_chunk(c, nc_ctx // SUB_CHUNKS, n_groups)
    cst = jax.ShapeDtypeStruct((b, nc, w, w), BF16)
    nmst = jax.ShapeDtypeStruct((b, nc, 8, w), F32)
    fwd3 = lambda j: (lambda bi, c: (bi, c, j))
    bwd3 = lambda j: (lambda bi, c: (bi, rev(c), j))
    return pl.pallas_call(
        functools.partial(_mlstm_state_kernel, n_heads=n_heads),
        grid=(b, n_groups),
        in_specs=[pl.BlockSpec((1, rows, w), fwd3(1)),
                  pl.BlockSpec((1, rows, w), fwd3(2)),
                  pl.BlockSpec((1, rows, ng), fwd3(0)),
                  pl.BlockSpec((1, rows, w), bwd3(1)),
                  pl.BlockSpec((1, rows, w), bwd3(2)),
                  pl.BlockSpec((1, rows, ng), bwd3(0)),
                  pl.BlockSpec((1, ng), lambda bi, c: (0, 0))],
        out_specs=(pl.BlockSpec((1, SUB_CHUNKS, w, w), lambda bi, c: (bi, c, 0, 0)),
                   pl.BlockSpec((1, SUB_CHUNKS, 8, w), lambda bi, c: (bi, c, 0, 0)),
                   pl.BlockSpec((1, SUB_CHUNKS, w, w), lambda bi, c: (bi, rev(c), 0, 0)),
                   pl.BlockSpec((1, SUB_CHUNKS, 8, w), lambda bi, c: (bi, rev(c), 0, 0))),
        out_shape=(cst, nmst, cst, nmst),
        scratch_shapes=[pltpu.VMEM((2, w, w), F32), pltpu.VMEM((2, 8, w), F32)],
        compiler_params=_cparams("parallel", "arbitrary"),
        name="mlstm_states",
    )(qk, m, gcol, qk, m, gcol, bias_row)


def _mlstm_out_kernel(qk_ref, m_ref, gc_ref, gr_ref, bc_ref, br_ref, cf_ref, nmf_ref, cb_ref, nmb_ref,
                      nw_ref, gmat_ref, o_ref, *, n_heads):
    w = o_ref.shape[2]
    hh = n_heads
    tril, triu = _tri_masks()
    lane = lax.broadcasted_iota(jnp.int32, (CHUNK, w), 1)
    head_masks = [(lane >= h * HEAD_DIM) & (lane < (h + 1) * HEAD_DIM) for h in range(hh)]
    gmat = gmat_ref[...]
    for sub in range(SUB_CHUNKS):
        rows = slice(sub * CHUNK, (sub + 1) * CHUNK)
        qk = qk_ref[0, rows, :]
        q, k = qk[:, 0:w], qk[:, w:2 * w]
        mm = m_ref[0, rows, :]
        v, og = mm[:, 2 * w:3 * w], mm[:, 3 * w:4 * w]
        pre_c = gc_ref[0, rows, :] + bc_ref[...]
        pre_r = gr_ref[0, :, rows] + br_ref[...]
        lf_c = jax.nn.log_sigmoid(pre_c)
        lf_r = jax.nn.log_sigmoid(pre_r)
        kb, vb, qb = k.astype(BF16), v.astype(BF16), q.astype(BF16)
        scores = [_dot_nt(jnp.where(hm, q, 0.0).astype(BF16), kb) for hm in head_masks]
        total = jnp.zeros((CHUNK, w), F32)
        for d, (c_ref, nm_ref) in enumerate(((cf_ref, nmf_ref), (cb_ref, nmb_ref))):
            tmat = tril if d == 0 else triu
            tmat_t = triu if d == 0 else tril
            b_c = _ones_dot(tmat.astype(BF16), lf_c)
            b_r = _split_dot(lf_r, tmat_t.astype(BF16), parts=3)
            nm = nm_ref[0, sub]
            n_row, m_row = nm[0:1, :], nm[1:2, :]
            cross = jnp.dot(qb, c_ref[0, sub], preferred_element_type=F32)
            qn = _split_dot(q * n_row, gmat)
            num = jnp.zeros((CHUNK, w), F32)
            den_e = jnp.zeros((CHUNK, w), F32)
            wp_e = jnp.zeros((CHUNK, w), F32)
            mt_e = jnp.zeros((CHUNK, w), F32)
            for h in range(hh):
                fc, ic = (2 + d) * hh + h, d * hh + h
                hm = head_masks[h]
                bcol = b_c[:, fc:fc + 1]
                logd = jnp.where(tmat, bcol - b_r[fc:fc + 1, :] + pre_r[ic:ic + 1, :], -jnp.inf)
                gg = bcol + m_row[:, h * HEAD_DIM:h * HEAD_DIM + 1]
                m_t = jnp.maximum(gg, jnp.max(logd, axis=-1, keepdims=True))
                p = scores[h] * jnp.exp(logd - m_t)
                pv = jnp.dot(p.astype(BF16), vb, preferred_element_type=F32)
                num = num + jnp.where(hm, pv, 0.0)
                den_e = jnp.where(hm, jnp.sum(p, axis=-1, keepdims=True), den_e)
                wp_e = jnp.where(hm, jnp.exp(gg - m_t), wp_e)
                mt_e = jnp.where(hm, m_t, mt_e)
            num = num + wp_e * cross
            den = den_e + wp_e * qn
            total = total + num / jnp.maximum(jnp.abs(den), jnp.exp(-mt_e))
        o_ref[0, rows, :] = _group_layer_norm(total * jax.nn.sigmoid(og), gmat, nw_ref[...])


def mlstm_out(qk, m, gcol, grow, bias_row, bias_col, cf, nmf, cb, nmb, norm_w, gmat, *, skip_chunks, n_heads):
    b, s, w2 = qk.shape
    w = w2 // 2
    ng = gcol.shape[2]
    assert skip_chunks % SUB_CHUNKS == 0 and (s // CHUNK) % SUB_CHUNKS == 0
    sk = skip_chunks // SUB_CHUNKS
    n_groups, rows = (s // CHUNK) // SUB_CHUNKS - sk, SUB_CHUNKS * CHUNK
    return pl.pallas_call(
        functools.partial(_mlstm_out_kernel, n_heads=n_heads),
        grid=(b, n_groups),
        in_specs=[pl.BlockSpec((1, rows, w2), lambda bi, c: (bi, c + sk, 0)),
                  pl.BlockSpec((1, rows, 4 * w), lambda bi, c: (bi, c + sk, 0)),
                  pl.BlockSpec((1, rows, ng), lambda bi, c: (bi, c + sk, 0)),
                  pl.BlockSpec((1, ng, rows), lambda bi, c: (bi, 0, c + sk)),
                  pl.BlockSpec((1, ng), lambda bi, c: (0, 0)),
                  pl.BlockSpec((ng, 1), lambda bi, c: (0, 0)),
                  pl.BlockSpec((1, SUB_CHUNKS, w, w), lambda bi, c: (bi, c + sk, 0, 0)),
                  pl.BlockSpec((1, SUB_CHUNKS, 8, w), lambda bi, c: (bi, c + sk, 0, 0)),
                  pl.BlockSpec((1, SUB_CHUNKS, w, w), lambda bi, c: (bi, c + sk, 0, 0)),
                  pl.BlockSpec((1, SUB_CHUNKS, 8, w), lambda bi, c: (bi, c + sk, 0, 0)),
                  pl.BlockSpec((1, w), lambda bi, c: (0, 0)),
                  pl.BlockSpec((w, w), lambda bi, c: (0, 0))],
        out_specs=pl.BlockSpec((1, rows, w), lambda bi, c: (bi, c, 0)),
        out_shape=jax.ShapeDtypeStruct((b, n_groups * rows, w), F32),
        compiler_params=_cparams("parallel", "parallel"),
        name="mlstm_out",
    )(qk, m, gcol, grow, bias_row, bias_col, cf, nmf, cb, nmb, norm_w, gmat)


def _proj_out_kernel(att_ref, ret_ref, ml_ref, h_ref, mod_ref, w_ref, o_ref):
    aw, rw = att_ref.shape[2], ret_ref.shape[2]
    y = jnp.dot(att_ref[0].astype(BF16), w_ref[0:aw, :], preferred_element_type=F32)
    y = y + jnp.dot(ret_ref[0].astype(BF16), w_ref[aw:aw + rw, :], preferred_element_type=F32)
    y = y + jnp.dot(ml_ref[0].astype(BF16), w_ref[aw + rw:, :], preferred_element_type=F32)
    o_ref[0] = h_ref[0] + mod_ref[0, 2:3, :] * y


def _mod_map(n_ctx_tiles, ctx_row, skip):
    return lambda bi, i: (jnp.where(i + skip < n_ctx_tiles, ctx_row, bi), 0, 0)


def proj_out(att, ret, ml, h, mod, w_out, *, n_ctx, skip):
    b, s, d = h.shape
    tm = TOKEN_TILE
    nt = s // tm - skip
    tok = lambda bi, i: (bi, i, 0)
    return pl.pallas_call(
        _proj_out_kernel,
        grid=(b, nt),
        in_specs=[pl.BlockSpec((1, tm, att.shape[2]), tok),
                  pl.BlockSpec((1, tm, ret.shape[2]), tok),
                  pl.BlockSpec((1, tm, ml.shape[2]), tok),
                  pl.BlockSpec((1, tm, d), lambda bi, i: (bi, i + skip, 0)),
                  pl.BlockSpec((1, 6, d), _mod_map(n_ctx // tm, mod.shape[0] - 1, skip)),
                  pl.BlockSpec(w_out.shape, lambda bi, i: (0, 0))],
        out_specs=pl.BlockSpec((1, tm, d), tok),
        out_shape=jax.ShapeDtypeStruct((b, nt * tm, d), F32),
        compiler_params=_cparams("parallel", "parallel"),
        name="proj_out",
    )(att, ret, ml, h, mod, w_out)


def _ffn_kernel(h_ref, mod_ref, nw_ref, wg_ref, wu_ref, wd_ref, o_ref):
    x = h_ref[0]
    a = _norm_mod(x, nw_ref[...], mod_ref[0, 3:4, :], mod_ref[0, 4:5, :]).astype(BF16)
    hg = jnp.dot(a, wg_ref[...], preferred_element_type=F32)
    hu = jnp.dot(a, wu_ref[...], preferred_element_type=F32)
    act = (_silu(hg) * hu).astype(BF16)
    y = jnp.dot(act, wd_ref[...], preferred_element_type=F32)
    o_ref[0] = x + mod_ref[0, 5:6, :] * y


def ffn_dense(h, mod, norm_w, wg, wu, wd, *, n_ctx):
    b, s, d = h.shape
    tm = TOKEN_TILE
    tok = lambda bi, i: (bi, i, 0)
    const = lambda bi, i: (0, 0)
    resident = functools.partial(pl.BlockSpec, pipeline_mode=pl.Buffered(1))
    return pl.pallas_call(
        _ffn_kernel,
        grid=(b, s // tm),
        in_specs=[pl.BlockSpec((1, tm, d), tok),
                  pl.BlockSpec((1, 6, d), _mod_map(n_ctx // tm, mod.shape[0] - 1, 0)),
                  pl.BlockSpec((1, d), const),
                  resident(wg.shape, const),
                  resident(wu.shape, const),
                  resident(wd.shape, const)],
        out_specs=pl.BlockSpec((1, tm, d), tok),
        out_shape=jax.ShapeDtypeStruct((b, s, d), F32),
        compiler_params=_cparams("parallel", "parallel"),
        name="ffn_dense",
    )(h, mod, norm_w, wg, wu, wd)


def _moe_pre_kernel(h_ref, mod_ref, nw_ref, rw_ref, rb_ref, xn_ref, route_ref, *, n_experts):
    x = h_ref[0]
    a = _norm_mod(x, nw_ref[...], mod_ref[0, 3:4, :], mod_ref[0, 4:5, :])
    a_hi = a.astype(BF16)
    xn_ref[0] = a_hi
    a_lo = (a - a_hi.astype(F32)).astype(BF16)
    rw = rw_ref[...]
    w_hi = rw.astype(BF16)
    w_lo = (rw - w_hi.astype(F32)).astype(BF16)
    logits = (jnp.dot(a_hi, w_hi, preferred_element_type=F32) + jnp.dot(a_lo, w_hi, preferred_element_type=F32)
              + jnp.dot(a_hi, w_lo, preferred_element_type=F32)) + rb_ref[...]
    lane = lax.broadcasted_iota(jnp.int32, logits.shape, 1)
    l1 = jnp.where(lane < n_experts, logits, -jnp.inf)
    v1 = jnp.max(l1, axis=-1, keepdims=True)
    i1 = jnp.min(jnp.where(l1 == v1, lane, 128), axis=-1, keepdims=True)
    l2 = jnp.where(lane == i1, -jnp.inf, l1)
    v2 = jnp.max(l2, axis=-1, keepdims=True)
    i2 = jnp.min(jnp.where(l2 == v2, lane, 128), axis=-1, keepdims=True)
    e2 = jnp.exp(v2 - v1)
    w1 = 1.0 / (1.0 + e2)
    w2 = e2 * w1
    lane8 = lax.broadcasted_iota(jnp.int32, route_ref.shape[1:], 1)
    route = jnp.where(lane8 == 0, i1.astype(F32),
                      jnp.where(lane8 == 1, i2.astype(F32),
                                jnp.where(lane8 == 2, w1, jnp.where(lane8 == 3, w2, 0.0))))
    route_ref[0] = route


def moe_pre(h, mod, norm_w, router_w_pad, router_b_pad, *, n_experts):
    b, t, d = h.shape
    tm = TOKEN_TILE
    tok = lambda bi, i: (bi, i, 0)
    const = lambda bi, i: (0, 0)
    return pl.pallas_call(
        functools.partial(_moe_pre_kernel, n_experts=n_experts),
        grid=(b, t // tm),
        in_specs=[pl.BlockSpec((1, tm, d), tok),
                  pl.BlockSpec((1, 6, d), lambda bi, i: (bi, 0, 0)),
                  pl.BlockSpec((1, d), const),
                  pl.BlockSpec(router_w_pad.shape, const),
                  pl.BlockSpec((1, 128), const)],
        out_specs=(pl.BlockSpec((1, tm, d), tok), pl.BlockSpec((1, tm, 8), tok)),
        out_shape=(jax.ShapeDtypeStruct((b, t, d), BF16), jax.ShapeDtypeStruct((b, t, 8), F32)),
        compiler_params=_cparams("parallel", "parallel"),
        name="moe_pre",
    )(h, mod, norm_w, router_w_pad, router_b_pad)


def _moe_gmm_kernel(be_ref, nu_ref, x_ref, wg_ref, wu_ref, wd_ref, o_ref, acc_sc):
    i, j = pl.program_id(0), pl.program_id(1)
    nj = pl.num_programs(1)
    used = i < nu_ref[0]

    @pl.when(used)
    def _():
        x = x_ref[...]
        hg = jnp.dot(x, wg_ref[0], preferred_element_type=F32)
        hu = jnp.dot(x, wu_ref[0], preferred_element_type=F32)
        y = jnp.dot((_silu(hg) * hu).astype(BF16), wd_ref[0], preferred_element_type=F32)

        @pl.when(j == 0)
        def _():
            acc_sc[...] = y

        @pl.when(j > 0)
        def _():
            acc_sc[...] += y

    @pl.when(j == nj - 1)
    def _():
        o_ref[...] = jnp.where(used, acc_sc[...], 0.0)


def moe_gmm(block_e, n_used, xb, wg, wu, wd):
    n_slots, d = xb.shape
    tm, tf = MOE_TILE, MOE_FF_TILE
    n_blocks = n_slots // tm
    f = wg.shape[2]
    nj = f // tf
    last = nj - 1

    def jj(i, j, nu):
        return jnp.where(i < nu[0], j, last)

    grid_spec = pltpu.PrefetchScalarGridSpec(
        num_scalar_prefetch=2,
        grid=(n_blocks, nj),
        in_specs=[pl.BlockSpec((tm, d), lambda i, j, be, nu: (i, 0)),
                  pl.BlockSpec((1, d, tf), lambda i, j, be, nu: (be[i], 0, jj(i, j, nu))),
                  pl.BlockSpec((1, d, tf), lambda i, j, be, nu: (be[i], 0, jj(i, j, nu))),
                  pl.BlockSpec((1, tf, d), lambda i, j, be, nu: (be[i], jj(i, j, nu), 0))],
        out_specs=pl.BlockSpec((tm, d), lambda i, j, be, nu: (i, 0)),
        scratch_shapes=[pltpu.VMEM((tm, d), F32)],
    )
    return pl.pallas_call(
        _moe_gmm_kernel,
        grid_spec=grid_spec,
        out_shape=jax.ShapeDtypeStruct((n_slots, d), F32),
        compiler_params=_cparams("arbitrary", "arbitrary"),
        name="moe_gmm",
    )(block_e, n_used, xb, wg, wu, wd)


def _moe_combine_kernel(h_ref, mod_ref, route_ref, y1_ref, y2_ref, o_ref):
    r = route_ref[0]
    y = r[:, 2:3] * y1_ref[0] + r[:, 3:4] * y2_ref[0]
    o_ref[0] = h_ref[0] + mod_ref[0, 5:6, :] * y


def moe_combine(h, mod, route, y1, y2):
    b, t, d = h.shape
    tm = TOKEN_TILE
    tok = lambda bi, i: (bi, i, 0)
    return pl.pallas_call(
        _moe_combine_kernel,
        grid=(b, t // tm),
        in_specs=[pl.BlockSpec((1, tm, d), tok),
                  pl.BlockSpec((1, 6, d), lambda bi, i: (bi, 0, 0)),
                  pl.BlockSpec((1, tm, 8), tok),
                  pl.BlockSpec((1, tm, d), tok),
                  pl.BlockSpec((1, tm, d), tok)],
        out_specs=pl.BlockSpec((1, tm, d), tok),
        out_shape=jax.ShapeDtypeStruct((b, t, d), F32),
        compiler_params=_cparams("parallel", "parallel"),
        name="moe_combine",
    )(h, mod, route, y1, y2)


def _moe_dispatch_plan(expert_idx, n_experts):
    n = expert_idx.shape[0]
    tile = MOE_TILE
    flat_e = expert_idx.reshape(-1)
    onehot = (flat_e[None, :] == jnp.arange(n_experts, dtype=jnp.int32)[:, None]).astype(jnp.int32)
    csum = jnp.cumsum(onehot, axis=1)
    counts = csum[:, -1]
    padded = (counts + tile - 1) // tile * tile
    pad_ends = jnp.cumsum(padded)
    pad_starts = pad_ends - padded
    slot = jnp.sum(onehot * (pad_starts[:, None] + csum - 1), axis=0)
    n_blocks = -(-(n * TOP_K) // tile) + n_experts
    n_used = pad_ends[-1] // tile
    blk = jnp.arange(n_blocks, dtype=jnp.int32)
    block_e = jnp.minimum(jnp.searchsorted(pad_ends, blk * tile, side="right"), n_experts - 1).astype(jnp.int32)
    block_e = jnp.where(blk < n_used, block_e, block_e[jnp.maximum(n_used - 1, 0)])
    slot_tok = jnp.zeros((n_blocks * tile,), jnp.int32).at[slot].set(jnp.arange(n * TOP_K, dtype=jnp.int32) // TOP_K)
    return slot.reshape(n, TOP_K), slot_tok, block_e, n_used.astype(jnp.int32).reshape(1)


def _rope_tables(n_ctx, n_lat):
    rows = n_lat // GRID_W
    row = jnp.repeat(jnp.arange(rows, dtype=F32), GRID_W)
    col = jnp.tile(jnp.arange(GRID_W, dtype=F32), rows)
    n_freq = HEAD_DIM // 4
    inv_freq = ROPE_THETA ** (-jnp.arange(n_freq, dtype=F32) / n_freq)
    ang = jnp.concatenate([row[:, None] * inv_freq, col[:, None] * inv_freq], -1)
    cos, sin = jnp.cos(ang), jnp.sin(ang)
    cos_t = jnp.concatenate([cos, cos, cos, cos], axis=1)
    sin_t = jnp.concatenate([-sin, sin, -sin, sin], axis=1)
    cos_t = jnp.concatenate([jnp.ones((n_ctx, 128), F32), cos_t], axis=0)
    sin_t = jnp.concatenate([jnp.zeros((n_ctx, 128), F32), sin_t], axis=0)
    return cos_t, sin_t


def _block_ones(n):
    idx = jnp.arange(n) // HEAD_DIM
    return (idx[:, None] == idx[None, :]).astype(BF16)


def kernel(x, c, ctx, c_ctx, mod_w, mod_b, norm1_w, norm2_w, w_in, attn_qn_w, attn_kn_w, ret_decay, ret_norm_w,
           mlstm_conv_w, mlstm_gate_b, mlstm_norm_w, w_out, ffn_w_gate, ffn_w_up, ffn_w_down, router_w, router_b,
           moe_w_gate, moe_w_up, moe_w_down):
    b, n_lat, d = x.shape
    n_ctx = ctx.shape[1]
    depth = mod_w.shape[0]
    n_experts = router_w.shape[2]
    ret_heads = ret_decay.shape[2]
    ml_heads = mlstm_gate_b.shape[2]
    ret_w, ml_w = ret_heads * HEAD_DIM, ml_heads * HEAD_DIM
    n_gate = N_GATES * ml_heads
    kv_w = (w_in.shape[2] - 4 * ret_w - 4 * ml_w - n_gate) // 6
    att_w = 4 * kv_w
    dims = (att_w, kv_w, ret_w, ml_w, n_gate)
    assert n_ctx % TOKEN_TILE == 0 and n_lat % KV_TILE == 0 and b + 1 <= 16
    n_main = w_in.shape[2] - n_gate
    n_pad = -(-w_in.shape[2] // 128) * 128

    cos_t, sin_t = _rope_tables(n_ctx, n_lat)
    gq, gk, g_ret, g_ml = _block_ones(att_w), _block_ones(kv_w), _block_ones(ret_w), _block_ones(ml_w)
    cond = jnp.concatenate([c, c_ctx[None, :], jnp.zeros((16 - b - 1, d), F32)], axis=0)
    ctx_tiles = n_ctx // TOKEN_TILE
    ctx_chunks = n_ctx // CHUNK

    h = jnp.concatenate([ctx, x], axis=1)
    for layer in range(depth):
        last = layer == depth - 1
        skip = ctx_tiles if last else 0
        mod = adaln(cond, mod_w[layer], mod_b[layer]).reshape(16, 6, d)[:b + 1]
        w_all = jnp.pad(w_in[layer].astype(BF16), ((0, 0), (0, n_pad - w_in.shape[2])))
        w_gate_t = w_in[layer][:, n_main:].T.astype(BF16)
        qn = jnp.tile(attn_qn_w[layer], att_w // HEAD_DIM)[None, :]
        kn = jnp.tile(attn_kn_w[layer], kv_w // HEAD_DIM)[None, :]
        k_norm_max = math.sqrt(HEAD_DIM) * jnp.max(jnp.abs(attn_kn_w[layer])) * (1.0 + 2.0 ** -6)
        q_norm_max = math.sqrt(HEAD_DIM) * jnp.max(jnp.abs(attn_qn_w[layer])) * (QK_SCALE * LOG2E)
        fast = (2.0 * q_norm_max * k_norm_max <= MAX_SCORE_SPAN_LOG2).astype(jnp.int32).reshape(1)
        kbound = jnp.full((1, att_w), q_norm_max * k_norm_max, F32)
        q, k, v, r, m, gcol, grow = proj_in(h, mod, norm1_w[layer][None, :], cos_t, sin_t, w_all, w_gate_t, qn, kn,
                                            kbound, gq, gk, n_ctx=n_ctx, dims=dims)
        att = attention(fast, q, k, v, n_ctx=n_ctx, skip=skip)
        dec = jnp.repeat(ret_decay[layer], HEAD_DIM, axis=1)
        sf, sb = ret_states(r, dec, n_ctx=n_ctx)
        ret = ret_out(r, sf, sb, dec, ret_norm_w[layer][None, :], g_ret, skip_chunks=ctx_chunks if last else 0)
        qk = mlstm_conv(m, mlstm_conv_w[layer], n_ctx=n_ctx)
        bias = mlstm_gate_b[layer].reshape(-1)
        cf, nmf, cb, nmb = mlstm_states(qk, m, gcol, bias[None, :], n_ctx=n_ctx, n_heads=ml_heads)
        ml = mlstm_out(qk, m, gcol, grow, bias[None, :], bias[:, None], cf, nmf, cb, nmb, mlstm_norm_w[layer][None, :],
                       g_ml, skip_chunks=ctx_chunks if last else 0, n_heads=ml_heads)
        h = proj_out(att, ret, ml, h, mod, w_out[layer].astype(BF16), n_ctx=n_ctx, skip=skip)
        if layer % 2 == 0:
            li = layer // 2
            assert not last, "dense FFN on a latent-only stream is not wired up"
            h = ffn_dense(h, mod, norm2_w[layer][None, :], ffn_w_gate[li].astype(BF16), ffn_w_up[li].astype(BF16),
                          ffn_w_down[li].astype(BF16), n_ctx=n_ctx)
        else:
            li = layer // 2
            assert last, "expert FFN on the combined stream is not wired up"
            mod_lat = mod[:b]
            rw = jnp.pad(router_w[li], ((0, 0), (0, 128 - n_experts)))
            rb = jnp.pad(router_b[li], (0, 128 - n_experts))[None, :]
            xn, route = moe_pre(h, mod_lat, norm2_w[layer][None, :], rw, rb, n_experts=n_experts)
            t = h.shape[1]
            route_flat = route.reshape(b * t, 8)
            slot, slot_tok, block_e, n_used = _moe_dispatch_plan(route_flat[:, 0:TOP_K].astype(jnp.int32), n_experts)
            xb = jnp.take(xn.reshape(b * t, d), slot_tok, axis=0)
            yb = moe_gmm(block_e, n_used, xb, moe_w_gate[li].astype(BF16), moe_w_up[li].astype(BF16),
                         moe_w_down[li].astype(BF16))
            y1 = jnp.take(yb, slot[:, 0], axis=0).reshape(b, t, d)
            y2 = jnp.take(yb, slot[:, 1], axis=0).reshape(b, t, d)
            h = moe_combine(h, mod_lat, route, y1, y2)
    return h if h.shape[1] == n_lat else h[:, n_ctx:, :]
```

```python
import functools
import math

import jax
import jax.numpy as jnp
from jax import lax
from jax.experimental import pallas as pl
from jax.experimental.pallas import tpu as pltpu

F32 = jnp.float32
BF16 = jnp.bfloat16

HEAD_DIM = 64
GRID_W = 64
ROPE_THETA = 10000.0
EPS = 1e-6
CHUNK = 128
SUB_CHUNKS = 2
TOKEN_TILE = 256
KV_TILE = 1024
N_GATES = 4
TOP_K = 2
MOE_TILE = 512
MOE_FF_TILE = 1792
VMEM_LIMIT = 56 * 1024 * 1024
QK_SCALE = HEAD_DIM ** -0.5
LOG2E = math.log2(math.e)
MAX_SCORE_SPAN_LOG2 = 100.0


def _cparams(*sem):
    return pltpu.CompilerParams(dimension_semantics=sem, vmem_limit_bytes=VMEM_LIMIT)


def _silu(x):
    return x * jax.nn.sigmoid(x)


def _split_dot(x, ones_bf16, parts=2):
    acc = None
    r = x
    for _ in range(parts):
        hi = r.astype(BF16)
        d = jnp.dot(hi, ones_bf16, preferred_element_type=F32)
        acc = d if acc is None else acc + d
        r = r - hi.astype(F32)
    return acc


def _dot_nt(a, b):
    return lax.dot_general(a, b, (((1,), (1,)), ((), ())), preferred_element_type=F32)


def _rope(x, cos_t, sin_t):
    w = x.shape[1]
    reps = w // 128
    c = jnp.concatenate([cos_t] * reps, axis=1) if reps > 1 else cos_t
    s = jnp.concatenate([sin_t] * reps, axis=1) if reps > 1 else sin_t
    lane = lax.broadcasted_iota(jnp.int32, x.shape, 1)
    first_half = (lane % HEAD_DIM) < (HEAD_DIM // 2)
    swapped = jnp.where(first_half, pltpu.roll(x, w - HEAD_DIM // 2, 1), pltpu.roll(x, HEAD_DIM // 2, 1))
    return x * c + swapped * s


def _group_rms(x, gmat, gain):
    x2 = x * x
    blk = 256
    if x.shape[1] > blk and x.shape[1] % blk == 0:
        ms = jnp.concatenate([_split_dot(x2[:, c:c + blk], gmat[c:c + blk, c:c + blk])
                              for c in range(0, x.shape[1], blk)], axis=1)
    else:
        ms = _split_dot(x2, gmat)
    return x * lax.rsqrt(ms * (1.0 / HEAD_DIM) + EPS) * gain


def _group_layer_norm(x, gmat, gain):
    mu = _split_dot(x, gmat) * (1.0 / HEAD_DIM)
    xc = x - mu
    var = _split_dot(xc * xc, gmat) * (1.0 / HEAD_DIM)
    return xc * lax.rsqrt(var + EPS) * gain


def _expand_heads(x, n_heads, lane):
    out = x[:, n_heads - 1:n_heads]
    for h in range(n_heads - 2, -1, -1):
        out = jnp.where(lane < (h + 1) * HEAD_DIM, x[:, h:h + 1], out)
    return out


def _block_diag_mask(n):
    r = lax.broadcasted_iota(jnp.int32, (n, n), 0) // HEAD_DIM
    c = lax.broadcasted_iota(jnp.int32, (n, n), 1) // HEAD_DIM
    return r == c


def _adaln_kernel(c_ref, w_ref, b_ref, o_ref):
    a = _silu(c_ref[...]).astype(BF16)
    o_ref[...] = jnp.dot(a, w_ref[...].astype(BF16), preferred_element_type=F32) + b_ref[...]


def adaln(cond, w, b):
    r, d = cond.shape
    n = w.shape[1]
    tn = 1536 if n % 1536 == 0 else n
    return pl.pallas_call(
        _adaln_kernel,
        grid=(n // tn,),
        in_specs=[pl.BlockSpec((r, d), lambda j: (0, 0)),
                  pl.BlockSpec((d, tn), lambda j: (0, j)),
                  pl.BlockSpec((1, tn), lambda j: (0, j))],
        out_specs=pl.BlockSpec((r, tn), lambda j: (0, j)),
        out_shape=jax.ShapeDtypeStruct((r, n), F32),
        compiler_params=_cparams("arbitrary"),
        name="adaln",
    )(cond, w, b.reshape(1, n))


def _norm_mod(x, nw, shift, scale):
    ms = jnp.mean(x * x, axis=-1, keepdims=True)
    y = x * lax.rsqrt(ms + EPS) * nw
    return y * (1.0 + scale) + shift


def _proj_in_kernel(h_ref, mod_ref, nw_ref, cos_ref, sin_ref, w_ref, wgt_ref, qn_ref, kn_ref, kb_ref, gq_ref, gk_ref,
                    q_ref, k_ref, v_ref, r_ref, m_ref, gc_ref, gr_ref, *, dims):
    att_w, kv_w, ret_w, ml_w, n_gate = dims
    x = h_ref[0]
    a = _norm_mod(x, nw_ref[...], mod_ref[0, 0:1, :], mod_ref[0, 1:2, :]).astype(BF16)
    p = jnp.dot(a, w_ref[...], preferred_element_type=F32)
    cos_t, sin_t = cos_ref[...], sin_ref[...]
    o = 0
    aq = p[:, o:o + att_w]; o += att_w
    ak = p[:, o:o + kv_w]; o += kv_w
    av = p[:, o:o + kv_w]; o += kv_w
    aq = _rope(_group_rms(aq, gq_ref[...], qn_ref[...]), cos_t, sin_t) * (QK_SCALE * LOG2E)
    ak = _rope(_group_rms(ak, gk_ref[...], kn_ref[...]), cos_t, sin_t)
    neg_bound = -kb_ref[...]
    tm = x.shape[0]
    lane = lax.broadcasted_iota(jnp.int32, (tm, 2 * HEAD_DIM), 1)

    def widen(a, extra):
        return jnp.where(lane == HEAD_DIM, extra, jnp.concatenate([a, jnp.zeros_like(a)], axis=1)).astype(BF16)

    for hh in range(att_w // HEAD_DIM):
        cols = slice(hh * HEAD_DIM, (hh + 1) * HEAD_DIM)
        q_ref[0, hh] = widen(aq[:, cols], neg_bound[:, hh * HEAD_DIM:hh * HEAD_DIM + 1])
    for hh in range(kv_w // HEAD_DIM):
        cols = slice(hh * HEAD_DIM, (hh + 1) * HEAD_DIM)
        k_ref[0, hh] = widen(ak[:, cols], 1.0)
        v_ref[0, hh] = widen(av[:, cols], 1.0)
    rq = _rope(p[:, o:o + ret_w], cos_t, sin_t)
    rk = _rope(p[:, o + ret_w:o + 2 * ret_w] * QK_SCALE, cos_t, sin_t)
    r_ref[0, :, 0:ret_w] = rq
    r_ref[0, :, ret_w:2 * ret_w] = rk
    r_ref[0, :, 2 * ret_w:4 * ret_w] = p[:, o + 2 * ret_w:o + 4 * ret_w]
    o += 4 * ret_w
    m_ref[0] = p[:, o:o + 4 * ml_w]
    o += 4 * ml_w
    gc_ref[0] = p[:, o:o + n_gate]
    gr_ref[0] = _dot_nt(wgt_ref[...], a)


def proj_in(h, mod, norm_w, cos_t, sin_t, w_all, w_gate_t, qn, kn, kbound, gq, gk, *, n_ctx, dims):
    att_w, kv_w, ret_w, ml_w, n_gate = dims
    b, s, d = h.shape
    tm = TOKEN_TILE
    n_ctx_tiles = n_ctx // tm
    n_pad = w_all.shape[1]
    ctx_row = mod.shape[0] - 1

    def mod_map(bi, i):
        return (jnp.where(i < n_ctx_tiles, ctx_row, bi), 0, 0)

    const2 = lambda bi, i: (0, 0)
    out_shape = (
        jax.ShapeDtypeStruct((b, att_w // HEAD_DIM, s, 2 * HEAD_DIM), BF16),
        jax.ShapeDtypeStruct((b, kv_w // HEAD_DIM, s, 2 * HEAD_DIM), BF16),
        jax.ShapeDtypeStruct((b, kv_w // HEAD_DIM, s, 2 * HEAD_DIM), BF16),
        jax.ShapeDtypeStruct((b, s, 4 * ret_w), F32),
        jax.ShapeDtypeStruct((b, s, 4 * ml_w), F32),
        jax.ShapeDtypeStruct((b, s, n_gate), F32),
        jax.ShapeDtypeStruct((b, n_gate, s), F32),
    )
    return pl.pallas_call(
        functools.partial(_proj_in_kernel, dims=dims),
        grid=(b, s // tm),
        in_specs=[
            pl.BlockSpec((1, tm, d), lambda bi, i: (bi, i, 0)),
            pl.BlockSpec((1, 6, d), mod_map),
            pl.BlockSpec((1, d), const2),
            pl.BlockSpec((tm, 128), lambda bi, i: (i, 0)),
            pl.BlockSpec((tm, 128), lambda bi, i: (i, 0)),
            pl.BlockSpec((d, n_pad), const2),
            pl.BlockSpec((n_gate, d), const2),
            pl.BlockSpec((1, att_w), const2),
            pl.BlockSpec((1, kv_w), const2),
            pl.BlockSpec((1, att_w), const2),
            pl.BlockSpec((att_w, att_w), const2),
            pl.BlockSpec((kv_w, kv_w), const2),
        ],
        out_specs=(
            pl.BlockSpec((1, att_w // HEAD_DIM, tm, 2 * HEAD_DIM), lambda bi, i: (bi, 0, i, 0)),
            pl.BlockSpec((1, kv_w // HEAD_DIM, tm, 2 * HEAD_DIM), lambda bi, i: (bi, 0, i, 0)),
            pl.BlockSpec((1, kv_w // HEAD_DIM, tm, 2 * HEAD_DIM), lambda bi, i: (bi, 0, i, 0)),
            pl.BlockSpec((1, tm, 4 * ret_w), lambda bi, i: (bi, i, 0)),
            pl.BlockSpec((1, tm, 4 * ml_w), lambda bi, i: (bi, i, 0)),
            pl.BlockSpec((1, tm, n_gate), lambda bi, i: (bi, i, 0)),
            pl.BlockSpec((1, n_gate, tm), lambda bi, i: (bi, 0, i)),
        ),
        out_shape=out_shape,
        compiler_params=_cparams("parallel", "parallel"),
        name="proj_in",
    )(h, mod, norm_w, cos_t, sin_t, w_all, w_gate_t, qn, kn, kbound, gq, gk)


def _conv_kernel(x_ref, prev_ref, next_ref, w_ref, o_ref, *, n_ctx, seq, half):
    i = pl.program_id(1)
    x = x_ref[0]
    tm, w = x.shape
    row = lax.broadcasted_iota(jnp.int32, (tm, 1), 0)
    grow = row + i * tm
    xp = jnp.where(row == 0, prev_ref[0, 7:8, :], pltpu.roll(x, 1, 0))
    xn = jnp.where(row == tm - 1, next_ref[0, 0:1, :], pltpu.roll(x, tm - 1, 0))
    xp = jnp.where((grow == 0) | (grow == n_ctx), 0.0, xp)
    xn = jnp.where((grow == n_ctx - 1) | (grow == seq - 1), 0.0, xn)
    y = _silu(xp * w_ref[0:1, :] + x * w_ref[1:2, :] + xn * w_ref[2:3, :])
    lane = lax.broadcasted_iota(jnp.int32, (tm, w), 1)
    o_ref[0] = jnp.where(lane >= half, y * QK_SCALE, y)


def mlstm_conv(m, conv_w, *, n_ctx):
    b, s, _ = m.shape
    w = conv_w.shape[1]
    tm = TOKEN_TILE
    r8 = tm // 8
    last8 = s // 8 - 1
    return pl.pallas_call(
        functools.partial(_conv_kernel, n_ctx=n_ctx, seq=s, half=w // 2),
        grid=(b, s // tm),
        in_specs=[pl.BlockSpec((1, tm, w), lambda bi, i: (bi, i, 0)),
                  pl.BlockSpec((1, 8, w), lambda bi, i: (bi, jnp.maximum(i * r8 - 1, 0), 0)),
                  pl.BlockSpec((1, 8, w), lambda bi, i: (bi, jnp.minimum((i + 1) * r8, last8), 0)),
                  pl.BlockSpec((3, w), lambda bi, i: (0, 0))],
        out_specs=pl.BlockSpec((1, tm, w), lambda bi, i: (bi, i, 0)),
        out_shape=jax.ShapeDtypeStruct((b, s, w), F32),
        compiler_params=_cparams("parallel", "parallel"),
        name="mlstm_conv",
    )(m, m, m, conv_w)


def _attn_kernel(fast_ref, q_ref, k_ref, v_ref, o_ref, m_sc, acc_sc, *, n_ctx, n_lat_tiles, skip):
    i = pl.program_id(2) + skip
    g, tq = q_ref.shape[1], q_ref.shape[2]
    hd = HEAD_DIM

    def fast_step(k, v):
        for h in range(g):
            rows = slice(h * tq, (h + 1) * tq)
            p = jnp.exp2(_dot_nt(q_ref[0, h], k))
            acc_sc[rows] += jnp.dot(p.astype(BF16), v, preferred_element_type=F32)

    def safe_step(k, v):
        for h in range(g):
            rows = slice(h * tq, (h + 1) * tq)
            s = _dot_nt(q_ref[0, h], k)
            m_old = m_sc[rows]
            m_new = jnp.maximum(m_old, jnp.max(s, axis=-1, keepdims=True))
            p = jnp.exp2(s - jnp.concatenate([m_new] * (s.shape[1] // 128), axis=1))
            acc_sc[rows] = jnp.exp2(m_old - m_new) * acc_sc[rows] + jnp.dot(p.astype(BF16), v, preferred_element_type=F32)
            m_sc[rows] = m_new

    def sweep(step, unroll):
        step(k_ref[0, 0, 0:n_ctx, :], v_ref[0, 0, 0:n_ctx, :])

        @pl.when(i * tq >= n_ctx)
        def _():
            def body(j, carry):
                start = pl.multiple_of(n_ctx + j * KV_TILE, math.gcd(n_ctx, KV_TILE))
                step(k_ref[0, 0, pl.ds(start, KV_TILE), :], v_ref[0, 0, pl.ds(start, KV_TILE), :])
                return carry
            lax.fori_loop(0, n_lat_tiles, body, 0, unroll=unroll)

    acc_sc[...] = jnp.zeros(acc_sc.shape, F32)

    @pl.when(fast_ref[0] == 1)
    def _():
        sweep(fast_step, 4)

    @pl.when(fast_ref[0] != 1)
    def _():
        m_sc[...] = jnp.full(m_sc.shape, -jnp.inf, F32)
        sweep(safe_step, 1)

    acc = acc_sc[...]
    out = acc[:, 0:hd] / acc[:, hd:hd + 1]
    o_ref[0] = jnp.concatenate([out[h * tq:(h + 1) * tq] for h in range(g)], axis=1)


def attention(fast, q, k, v, *, n_ctx, skip):
    b, hq, s, lanes = q.shape
    hkv = k.shape[1]
    g = hq // hkv
    tq = TOKEN_TILE
    n_lat = s - n_ctx
    assert n_lat % KV_TILE == 0 and n_ctx % tq == 0 and lanes == 2 * HEAD_DIM
    nq = s // tq - skip
    grid_spec = pltpu.PrefetchScalarGridSpec(
        num_scalar_prefetch=1,
        grid=(b, hkv, nq),
        in_specs=[pl.BlockSpec((1, g, tq, lanes), lambda bi, gi, i, f: (bi, gi, i + skip, 0)),
                  pl.BlockSpec((1, 1, s, lanes), lambda bi, gi, i, f: (bi, gi, 0, 0)),
                  pl.BlockSpec((1, 1, s, lanes), lambda bi, gi, i, f: (bi, gi, 0, 0))],
        out_specs=pl.BlockSpec((1, tq, g * HEAD_DIM), lambda bi, gi, i, f: (bi, i, gi)),
        scratch_shapes=[pltpu.VMEM((g * tq, 128), F32), pltpu.VMEM((g * tq, lanes), F32)],
    )
    return pl.pallas_call(
        functools.partial(_attn_kernel, n_ctx=n_ctx, n_lat_tiles=n_lat // KV_TILE, skip=skip),
        grid_spec=grid_spec,
        out_shape=jax.ShapeDtypeStruct((b, nq * tq, hq * HEAD_DIM), F32),
        compiler_params=_cparams("parallel", "parallel", "arbitrary"),
        name="attention",
    )(fast, q, k, v)


def _rev_chunk(c, nc_ctx, nc):
    return jnp.where(c < nc_ctx, nc_ctx - 1 - c, nc - 1 + nc_ctx - c)


def _ret_state_kernel(kf_ref, vf_ref, kb_ref, vb_ref, dec_ref, sf_ref, sb_ref, s_sc):
    c = pl.program_id(1)

    @pl.when(c == 0)
    def _():
        s_sc[...] = jnp.zeros(s_sc.shape, F32)

    w = kf_ref.shape[2]
    lg = jnp.log1p(-jnp.exp(dec_ref[...]))
    pos = lax.broadcasted_iota(jnp.int32, (CHUNK, 1), 0).astype(F32)
    bd = _block_diag_mask(w)
    for d, (k_ref, v_ref, out_ref) in enumerate(((kf_ref, vf_ref, sf_ref), (kb_ref, vb_ref, sb_ref))):
        lgd = lg[d:d + 1, :]
        e = (CHUNK - 1.0 - pos) if d == 0 else pos
        kscale = jnp.exp(e * lgd)
        chunk_decay = jnp.exp(CHUNK * lgd)
        for sub in (range(SUB_CHUNKS) if d == 0 else reversed(range(SUB_CHUNKS))):
            rows = slice(sub * CHUNK, (sub + 1) * CHUNK)
            kdec = k_ref[0, rows, :] * kscale
            inc = jnp.dot(kdec.T.astype(BF16), v_ref[0, rows, :].astype(BF16), preferred_element_type=F32)
            s_old = s_sc[d]
            out_ref[0, sub] = s_old.astype(BF16)
            s_sc[d] = chunk_decay * s_old + jnp.where(bd, inc, 0.0)


def ret_states(r, dec, *, n_ctx):
    b, s, w4 = r.shape
    w = w4 // 4
    nc, nc_ctx = s // CHUNK, n_ctx // CHUNK
    assert nc % SUB_CHUNKS == 0 and nc_ctx % SUB_CHUNKS == 0
    ng, rows = nc // SUB_CHUNKS, SUB_CHUNKS * CHUNK
    rev = lambda c: _rev_chunk(c, nc_ctx // SUB_CHUNKS, ng)
    st = jax.ShapeDtypeStruct((b, nc, w, w), BF16)
    return pl.pallas_call(
        _ret_state_kernel,
        grid=(b, ng),
        in_specs=[pl.BlockSpec((1, rows, w), lambda bi, c: (bi, c, 1)),
                  pl.BlockSpec((1, rows, w), lambda bi, c: (bi, c, 2)),
                  pl.BlockSpec((1, rows, w), lambda bi, c: (bi, rev(c), 1)),
                  pl.BlockSpec((1, rows, w), lambda bi, c: (bi, rev(c), 2)),
                  pl.BlockSpec((2, w), lambda bi, c: (0, 0))],
        out_specs=(pl.BlockSpec((1, SUB_CHUNKS, w, w), lambda bi, c: (bi, c, 0, 0)),
                   pl.BlockSpec((1, SUB_CHUNKS, w, w), lambda bi, c: (bi, rev(c), 0, 0))),
        out_shape=(st, st),
        scratch_shapes=[pltpu.VMEM((2, w, w), F32)],
        compiler_params=_cparams("parallel", "arbitrary"),
        name="ret_states",
    )(r, r, r, r, dec)


def _ret_out_kernel(r_ref, sf_ref, sb_ref, dec_ref, nw_ref, gmat_ref, o_ref):
    w = o_ref.shape[2]
    n_heads = w // HEAD_DIM
    lg = jnp.log1p(-jnp.exp(dec_ref[...]))
    pos = lax.broadcasted_iota(jnp.int32, (CHUNK, 1), 0).astype(F32)
    row = lax.broadcasted_iota(jnp.int32, (CHUNK, CHUNK), 0)
    col = lax.broadcasted_iota(jnp.int32, (CHUNK, CHUNK), 1)
    lane = lax.broadcasted_iota(jnp.int32, (CHUNK, w), 1)
    head_masks = [(lane >= h * HEAD_DIM) & (lane < (h + 1) * HEAD_DIM) for h in range(n_heads)]
    q_scale = []
    decay = [jnp.zeros((CHUNK, CHUNK), F32)] * n_heads
    for d in range(2):
        lgd = lg[d:d + 1, :]
        q_scale.append(jnp.exp(((pos + 1.0) if d == 0 else (CHUNK - pos)) * lgd))
        lag = ((row - col) if d == 0 else (col - row)).astype(F32)
        decay = [decay[h] + jnp.where(lag >= 0, jnp.exp(jnp.maximum(lag, 0.0) * lgd[:, h * HEAD_DIM:h * HEAD_DIM + 1]), 0.0)
                 for h in range(n_heads)]
    for sub in range(SUB_CHUNKS):
        x = r_ref[0, sub * CHUNK:(sub + 1) * CHUNK, :]
        q, k, v, g = x[:, 0:w], x[:, w:2 * w], x[:, 2 * w:3 * w], x[:, 3 * w:4 * w]
        kb, vb = k.astype(BF16), v.astype(BF16)
        total = jnp.dot((q * q_scale[0]).astype(BF16), sf_ref[0, sub], preferred_element_type=F32)
        total = total + jnp.dot((q * q_scale[1]).astype(BF16), sb_ref[0, sub], preferred_element_type=F32)
        for h in range(n_heads):
            sc = _dot_nt(jnp.where(head_masks[h], q, 0.0).astype(BF16), kb)
            pv = jnp.dot((sc * decay[h]).astype(BF16), vb, preferred_element_type=F32)
            total = total + jnp.where(head_masks[h], pv, 0.0)
        o_ref[0, sub * CHUNK:(sub + 1) * CHUNK, :] = _group_layer_norm(total, gmat_ref[...], nw_ref[...]) * _silu(g)


def ret_out(r, sf, sb, dec, norm_w, gmat, *, skip_chunks):
    b, s, w4 = r.shape
    w = w4 // 4
    assert skip_chunks % SUB_CHUNKS == 0 and (s // CHUNK) % SUB_CHUNKS == 0
    sk = skip_chunks // SUB_CHUNKS
    ng, rows = (s // CHUNK) // SUB_CHUNKS - sk, SUB_CHUNKS * CHUNK
    return pl.pallas_call(
        _ret_out_kernel,
        grid=(b, ng),
        in_specs=[pl.BlockSpec((1, rows, w4), lambda bi, c: (bi, c + sk, 0)),
                  pl.BlockSpec((1, SUB_CHUNKS, w, w), lambda bi, c: (bi, c + sk, 0, 0)),
                  pl.BlockSpec((1, SUB_CHUNKS, w, w), lambda bi, c: (bi, c + sk, 0, 0)),
                  pl.BlockSpec((2, w), lambda bi, c: (0, 0)),
                  pl.BlockSpec((1, w), lambda bi, c: (0, 0)),
                  pl.BlockSpec((w, w), lambda bi, c: (0, 0))],
        out_specs=pl.BlockSpec((1, rows, w), lambda bi, c: (bi, c, 0)),
        out_shape=jax.ShapeDtypeStruct((b, ng * rows, w), F32),
        compiler_params=_cparams("parallel", "parallel"),
        name="ret_out",
    )(r, sf, sb, dec, norm_w, gmat)


def _tri_masks():
    row = lax.broadcasted_iota(jnp.int32, (CHUNK, CHUNK), 0)
    col = lax.broadcasted_iota(jnp.int32, (CHUNK, CHUNK), 1)
    return col <= row, col >= row


def _ones_dot(ones_bf16, x, parts=3):
    acc = None
    r = x
    for _ in range(parts):
        hi = r.astype(BF16)
        d = jnp.dot(ones_bf16, hi, preferred_element_type=F32)
        acc = d if acc is None else acc + d
        r = r - hi.astype(F32)
    return acc


def _mlstm_state_kernel(kf_ref, vf_ref, gf_ref, kb_ref, vb_ref, gb_ref, bias_ref,
                        cf_ref, nmf_ref, cb_ref, nmb_ref, c_sc, nm_sc, *, n_heads):
    c = pl.program_id(1)

    @pl.when(c == 0)
    def _():
        c_sc[...] = jnp.zeros(c_sc.shape, F32)
        nm_sc[...] = jnp.zeros(nm_sc.shape, F32)

    w = kf_ref.shape[2]
    hh = n_heads
    tril, triu = _tri_masks()
    lane_t = lax.broadcasted_iota(jnp.int32, (CHUNK, w), 1)
    lane_1 = lax.broadcasted_iota(jnp.int32, (1, w), 1)
    bd = _block_diag_mask(w)
    dirs = ((kf_ref, vf_ref, gf_ref, cf_ref, nmf_ref, tril), (kb_ref, vb_ref, gb_ref, cb_ref, nmb_ref, triu))
    for d, (k_ref, v_ref, g_ref, c_out, nm_out, tmat) in enumerate(dirs):
        for sub in (range(SUB_CHUNKS) if d == 0 else reversed(range(SUB_CHUNKS))):
            rows = slice(sub * CHUNK, (sub + 1) * CHUNK)
            pre = g_ref[0, rows, :] + bias_ref[...]
            ig = pre[:, d * hh:(d + 1) * hh]
            lf_all = jax.nn.log_sigmoid(pre)
            b_all = _ones_dot(tmat.astype(BF16), lf_all)
            lf = lf_all[:, (2 + d) * hh:(3 + d) * hh]
            bcum = b_all[:, (2 + d) * hh:(3 + d) * hh]
            b_end = jnp.sum(lf, axis=0, keepdims=True)
            a = b_end - bcum + ig
            m_loc = jnp.max(a, axis=0, keepdims=True)
            wgt = jnp.exp(a - m_loc)
            kw = k_ref[0, rows, :] * _expand_heads(wgt, hh, lane_t)
            c_inc = jnp.dot(kw.T.astype(BF16), v_ref[0, rows, :].astype(BF16), preferred_element_type=F32)
            n_inc = jnp.sum(kw, axis=0, keepdims=True)
            be = _expand_heads(b_end, hh, lane_1)
            ml = _expand_heads(m_loc, hh, lane_1)
            c_old = c_sc[d]
            n_old = nm_sc[d, 0:1, :]
            m_old = nm_sc[d, 1:2, :]
            c_out[0, sub] = c_old.astype(BF16)
            nm_out[0, sub] = nm_sc[d]
            m_new = jnp.maximum(be + m_old, ml)
            w_prev = jnp.exp(be + m_old - m_new)
            w_inc = jnp.exp(ml - m_new)
            c_sc[d] = w_prev * c_old + w_inc * jnp.where(bd, c_inc, 0.0)
            nm_sc[d, 0:1, :] = w_prev * n_old + w_inc * n_inc
            nm_sc[d, 1:2, :] = m_new


def mlstm_states(qk, m, gcol, bias_row, *, n_ctx, n_heads):
    b, s, w2 = qk.shape
    w = w2 // 2
    ng = gcol.shape[2]
    nc, nc_ctx = s // CHUNK, n_ctx // CHUNK
    assert nc % SUB_CHUNKS == 0 and nc_ctx % SUB_CHUNKS == 0
    n_groups, rows = nc // SUB_CHUNKS, SUB_CHUNKS * CHUNK
    rev = lambda c: _rev_chunk(c, nc_ctx // SUB_CHUNKS, n_groups)
    cst = jax.ShapeDtypeStruct((b, nc, w, w), BF16)
    nmst = jax.ShapeDtypeStruct((b, nc, 8, w), F32)
    fwd3 = lambda j: (lambda bi, c: (bi, c, j))
    bwd3 = lambda j: (lambda bi, c: (bi, rev(c), j))
    return pl.pallas_call(
        functools.partial(_mlstm_state_kernel, n_heads=n_heads),
        grid=(b, n_groups),
        in_specs=[pl.BlockSpec((1, rows, w), fwd3(1)),
                  pl.BlockSpec((1, rows, w), fwd3(2)),
                  pl.BlockSpec((1, rows, ng), fwd3(0)),
                  pl.BlockSpec((1, rows, w), bwd3(1)),
                  pl.BlockSpec((1, rows, w), bwd3(2)),
                  pl.BlockSpec((1, rows, ng), bwd3(0)),
                  pl.BlockSpec((1, ng), lambda bi, c: (0, 0))],
        out_specs=(pl.BlockSpec((1, SUB_CHUNKS, w, w), lambda bi, c: (bi, c, 0, 0)),
                   pl.BlockSpec((1, SUB_CHUNKS, 8, w), lambda bi, c: (bi, c, 0, 0)),
                   pl.BlockSpec((1, SUB_CHUNKS, w, w), lambda bi, c: (bi, rev(c), 0, 0)),
                   pl.BlockSpec((1, SUB_CHUNKS, 8, w), lambda bi, c: (bi, rev(c), 0, 0))),
        out_shape=(cst, nmst, cst, nmst),
        scratch_shapes=[pltpu.VMEM((2, w, w), F32), pltpu.VMEM((2, 8, w), F32)],
        compiler_params=_cparams("parallel", "arbitrary"),
        name="mlstm_states",
    )(qk, m, gcol, qk, m, gcol, bias_row)


def _mlstm_out_kernel(qk_ref, m_ref, gc_ref, gr_ref, bc_ref, br_ref, cf_ref, nmf_ref, cb_ref, nmb_ref,
                      nw_ref, gmat_ref, o_ref, *, n_heads):
    w = o_ref.shape[2]
    hh = n_heads
    tril, triu = _tri_masks()
    lane = lax.broadcasted_iota(jnp.int32, (CHUNK, w), 1)
    head_masks = [(lane >= h * HEAD_DIM) & (lane < (h + 1) * HEAD_DIM) for h in range(hh)]
    gmat = gmat_ref[...]
    for sub in range(SUB_CHUNKS):
        rows = slice(sub * CHUNK, (sub + 1) * CHUNK)
        qk = qk_ref[0, rows, :]
        q, k = qk[:, 0:w], qk[:, w:2 * w]
        mm = m_ref[0, rows, :]
        v, og = mm[:, 2 * w:3 * w], mm[:, 3 * w:4 * w]
        pre_c = gc_ref[0, rows, :] + bc_ref[...]
        pre_r = gr_ref[0, :, rows] + br_ref[...]
        lf_c = jax.nn.log_sigmoid(pre_c)
        lf_r = jax.nn.log_sigmoid(pre_r)
        kb, vb, qb = k.astype(BF16), v.astype(BF16), q.astype(BF16)
        scores = [_dot_nt(jnp.where(hm, q, 0.0).astype(BF16), kb) for hm in head_masks]
        total = jnp.zeros((CHUNK, w), F32)
        for d, (c_ref, nm_ref) in enumerate(((cf_ref, nmf_ref), (cb_ref, nmb_ref))):
            tmat = tril if d == 0 else triu
            tmat_t = triu if d == 0 else tril
            b_c = _ones_dot(tmat.astype(BF16), lf_c)
            b_r = _split_dot(lf_r, tmat_t.astype(BF16), parts=3)
            nm = nm_ref[0, sub]
            n_row, m_row = nm[0:1, :], nm[1:2, :]
            cross = jnp.dot(qb, c_ref[0, sub], preferred_element_type=F32)
            qn = _split_dot(q * n_row, gmat)
            num = jnp.zeros((CHUNK, w), F32)
            den_e = jnp.zeros((CHUNK, w), F32)
            wp_e = jnp.zeros((CHUNK, w), F32)
            mt_e = jnp.zeros((CHUNK, w), F32)
            for h in range(hh):
                fc, ic = (2 + d) * hh + h, d * hh + h
                hm = head_masks[h]
                bcol = b_c[:, fc:fc + 1]
                logd = jnp.where(tmat, bcol - b_r[fc:fc + 1, :] + pre_r[ic:ic + 1, :], -jnp.inf)
                gg = bcol + m_row[:, h * HEAD_DIM:h * HEAD_DIM + 1]
                m_t = jnp.maximum(gg, jnp.max(logd, axis=-1, keepdims=True))
                p = scores[h] * jnp.exp(logd - m_t)
                pv = jnp.dot(p.astype(BF16), vb, preferred_element_type=F32)
                num = num + jnp.where(hm, pv, 0.0)
                den_e = jnp.where(hm, jnp.sum(p, axis=-1, keepdims=True), den_e)
                wp_e = jnp.where(hm, jnp.exp(gg - m_t), wp_e)
                mt_e = jnp.where(hm, m_t, mt_e)
            num = num + wp_e * cross
            den = den_e + wp_e * qn
            total = total + num / jnp.maximum(jnp.abs(den), jnp.exp(-mt_e))
        o_ref[0, rows, :] = _group_layer_norm(total * jax.nn.sigmoid(og), gmat, nw_ref[...])


def mlstm_out(qk, m, gcol, grow, bias_row, bias_col, cf, nmf, cb, nmb, norm_w, gmat, *, skip_chunks, n_heads):
    b, s, w2 = qk.shape
    w = w2 // 2
    ng = gcol.shape[2]
    assert skip_chunks % SUB_CHUNKS == 0 and (s // CHUNK) % SUB_CHUNKS == 0
    sk = skip_chunks // SUB_CHUNKS
    n_groups, rows = (s // CHUNK) // SUB_CHUNKS - sk, SUB_CHUNKS * CHUNK
    return pl.pallas_call(
        functools.partial(_mlstm_out_kernel, n_heads=n_heads),
        grid=(b, n_groups),
        in_specs=[pl.BlockSpec((1, rows, w2), lambda bi, c: (bi, c + sk, 0)),
                  pl.BlockSpec((1, rows, 4 * w), lambda bi, c: (bi, c + sk, 0)),
                  pl.BlockSpec((1, rows, ng), lambda bi, c: (bi, c + sk, 0)),
                  pl.BlockSpec((1, ng, rows), lambda bi, c: (bi, 0, c + sk)),
                  pl.BlockSpec((1, ng), lambda bi, c: (0, 0)),
                  pl.BlockSpec((ng, 1), lambda bi, c: (0, 0)),
                  pl.BlockSpec((1, SUB_CHUNKS, w, w), lambda bi, c: (bi, c + sk, 0, 0)),
                  pl.BlockSpec((1, SUB_CHUNKS, 8, w), lambda bi, c: (bi, c + sk, 0, 0)),
                  pl.BlockSpec((1, SUB_CHUNKS, w, w), lambda bi, c: (bi, c + sk, 0, 0)),
                  pl.BlockSpec((1, SUB_CHUNKS, 8, w), lambda bi, c: (bi, c + sk, 0, 0)),
                  pl.BlockSpec((1, w), lambda bi, c: (0, 0)),
                  pl.BlockSpec((w, w), lambda bi, c: (0, 0))],
        out_specs=pl.BlockSpec((1, rows, w), lambda bi, c: (bi, c, 0)),
        out_shape=jax.ShapeDtypeStruct((b, n_groups * rows, w), F32),
        compiler_params=_cparams("parallel", "parallel"),
        name="mlstm_out",
    )(qk, m, gcol, grow, bias_row, bias_col, cf, nmf, cb, nmb, norm_w, gmat)


def _proj_out_kernel(att_ref, ret_ref, ml_ref, h_ref, mod_ref, w_ref, o_ref):
    aw, rw = att_ref.shape[2], ret_ref.shape[2]
    y = jnp.dot(att_ref[0].astype(BF16), w_ref[0:aw, :], preferred_element_type=F32)
    y = y + jnp.dot(ret_ref[0].astype(BF16), w_ref[aw:aw + rw, :], preferred_element_type=F32)
    y = y + jnp.dot(ml_ref[0].astype(BF16), w_ref[aw + rw:, :], preferred_element_type=F32)
    o_ref[0] = h_ref[0] + mod_ref[0, 2:3, :] * y


def _mod_map(n_ctx_tiles, ctx_row, skip):
    return lambda bi, i: (jnp.where(i + skip < n_ctx_tiles, ctx_row, bi), 0, 0)


def proj_out(att, ret, ml, h, mod, w_out, *, n_ctx, skip):
    b, s, d = h.shape
    tm = TOKEN_TILE
    nt = s // tm - skip
    tok = lambda bi, i: (bi, i, 0)
    return pl.pallas_call(
        _proj_out_kernel,
        grid=(b, nt),
        in_specs=[pl.BlockSpec((1, tm, att.shape[2]), tok),
                  pl.BlockSpec((1, tm, ret.shape[2]), tok),
                  pl.BlockSpec((1, tm, ml.shape[2]), tok),
                  pl.BlockSpec((1, tm, d), lambda bi, i: (bi, i + skip, 0)),
                  pl.BlockSpec((1, 6, d), _mod_map(n_ctx // tm, mod.shape[0] - 1, skip)),
                  pl.BlockSpec(w_out.shape, lambda bi, i: (0, 0))],
        out_specs=pl.BlockSpec((1, tm, d), tok),
        out_shape=jax.ShapeDtypeStruct((b, nt * tm, d), F32),
        compiler_params=_cparams("parallel", "parallel"),
        name="proj_out",
    )(att, ret, ml, h, mod, w_out)


def _ffn_kernel(h_ref, mod_ref, nw_ref, wg_ref, wu_ref, wd_ref, o_ref):
    x = h_ref[0]
    a = _norm_mod(x, nw_ref[...], mod_ref[0, 3:4, :], mod_ref[0, 4:5, :]).astype(BF16)
    hg = jnp.dot(a, wg_ref[...], preferred_element_type=F32)
    hu = jnp.dot(a, wu_ref[...], preferred_element_type=F32)
    act = (_silu(hg) * hu).astype(BF16)
    y = jnp.dot(act, wd_ref[...], preferred_element_type=F32)
    o_ref[0] = x + mod_ref[0, 5:6, :] * y


def ffn_dense(h, mod, norm_w, wg, wu, wd, *, n_ctx):
    b, s, d = h.shape
    tm = TOKEN_TILE
    tok = lambda bi, i: (bi, i, 0)
    const = lambda bi, i: (0, 0)
    resident = functools.partial(pl.BlockSpec, pipeline_mode=pl.Buffered(1))
    return pl.pallas_call(
        _ffn_kernel,
        grid=(b, s // tm),
        in_specs=[pl.BlockSpec((1, tm, d), tok),
                  pl.BlockSpec((1, 6, d), _mod_map(n_ctx // tm, mod.shape[0] - 1, 0)),
                  pl.BlockSpec((1, d), const),
                  resident(wg.shape, const),
                  resident(wu.shape, const),
                  resident(wd.shape, const)],
        out_specs=pl.BlockSpec((1, tm, d), tok),
        out_shape=jax.ShapeDtypeStruct((b, s, d), F32),
        compiler_params=_cparams("parallel", "parallel"),
        name="ffn_dense",
    )(h, mod, norm_w, wg, wu, wd)


def _moe_pre_kernel(h_ref, mod_ref, nw_ref, rw_ref, rb_ref, xn_ref, route_ref, *, n_experts):
    x = h_ref[0]
    a = _norm_mod(x, nw_ref[...], mod_ref[0, 3:4, :], mod_ref[0, 4:5, :])
    a_hi = a.astype(BF16)
    xn_ref[0] = a_hi
    a_lo = (a - a_hi.astype(F32)).astype(BF16)
    rw = rw_ref[...]
    w_hi = rw.astype(BF16)
    w_lo = (rw - w_hi.astype(F32)).astype(BF16)
    logits = (jnp.dot(a_hi, w_hi, preferred_element_type=F32) + jnp.dot(a_lo, w_hi, preferred_element_type=F32)
              + jnp.dot(a_hi, w_lo, preferred_element_type=F32)) + rb_ref[...]
    lane = lax.broadcasted_iota(jnp.int32, logits.shape, 1)
    l1 = jnp.where(lane < n_experts, logits, -jnp.inf)
    v1 = jnp.max(l1, axis=-1, keepdims=True)
    i1 = jnp.min(jnp.where(l1 == v1, lane, 128), axis=-1, keepdims=True)
    l2 = jnp.where(lane == i1, -jnp.inf, l1)
    v2 = jnp.max(l2, axis=-1, keepdims=True)
    i2 = jnp.min(jnp.where(l2 == v2, lane, 128), axis=-1, keepdims=True)
    e2 = jnp.exp(v2 - v1)
    w1 = 1.0 / (1.0 + e2)
    w2 = e2 * w1
    lane8 = lax.broadcasted_iota(jnp.int32, route_ref.shape[1:], 1)
    route = jnp.where(lane8 == 0, i1.astype(F32),
                      jnp.where(lane8 == 1, i2.astype(F32),
                                jnp.where(lane8 == 2, w1, jnp.where(lane8 == 3, w2, 0.0))))
    route_ref[0] = route


def moe_pre(h, mod, norm_w, router_w_pad, router_b_pad, *, n_experts):
    b, t, d = h.shape
    tm = TOKEN_TILE
    tok = lambda bi, i: (bi, i, 0)
    const = lambda bi, i: (0, 0)
    return pl.pallas_call(
        functools.partial(_moe_pre_kernel, n_experts=n_experts),
        grid=(b, t // tm),
        in_specs=[pl.BlockSpec((1, tm, d), tok),
                  pl.BlockSpec((1, 6, d), lambda bi, i: (bi, 0, 0)),
                  pl.BlockSpec((1, d), const),
                  pl.BlockSpec(router_w_pad.shape, const),
                  pl.BlockSpec((1, 128), const)],
        out_specs=(pl.BlockSpec((1, tm, d), tok), pl.BlockSpec((1, tm, 8), tok)),
        out_shape=(jax.ShapeDtypeStruct((b, t, d), BF16), jax.ShapeDtypeStruct((b, t, 8), F32)),
        compiler_params=_cparams("parallel", "parallel"),
        name="moe_pre",
    )(h, mod, norm_w, router_w_pad, router_b_pad)


def _moe_gmm_kernel(be_ref, nu_ref, x_ref, wg_ref, wu_ref, wd_ref, o_ref, acc_sc):
    i, j = pl.program_id(0), pl.program_id(1)
    nj = pl.num_programs(1)
    used = i < nu_ref[0]

    @pl.when(used)
    def _():
        x = x_ref[...]
        hg = jnp.dot(x, wg_ref[0], preferred_element_type=F32)
        hu = jnp.dot(x, wu_ref[0], preferred_element_type=F32)
        y = jnp.dot((_silu(hg) * hu).astype(BF16), wd_ref[0], preferred_element_type=F32)

        @pl.when(j == 0)
        def _():
            acc_sc[...] = y

        @pl.when(j > 0)
        def _():
            acc_sc[...] += y

    @pl.when(j == nj - 1)
    def _():
        o_ref[...] = jnp.where(used, acc_sc[...], 0.0)


def moe_gmm(block_e, n_used, xb, wg, wu, wd):
    n_slots, d = xb.shape
    tm, tf = MOE_TILE, MOE_FF_TILE
    n_blocks = n_slots // tm
    f = wg.shape[2]
    nj = f // tf
    last = nj - 1

    def jj(i, j, nu):
        return jnp.where(i < nu[0], j, last)

    grid_spec = pltpu.PrefetchScalarGridSpec(
        num_scalar_prefetch=2,
        grid=(n_blocks, nj),
        in_specs=[pl.BlockSpec((tm, d), lambda i, j, be, nu: (i, 0)),
                  pl.BlockSpec((1, d, tf), lambda i, j, be, nu: (be[i], 0, jj(i, j, nu))),
                  pl.BlockSpec((1, d, tf), lambda i, j, be, nu: (be[i], 0, jj(i, j, nu))),
                  pl.BlockSpec((1, tf, d), lambda i, j, be, nu: (be[i], jj(i, j, nu), 0))],
        out_specs=pl.BlockSpec((tm, d), lambda i, j, be, nu: (i, 0)),
        scratch_shapes=[pltpu.VMEM((tm, d), F32)],
    )
    return pl.pallas_call(
        _moe_gmm_kernel,
        grid_spec=grid_spec,
        out_shape=jax.ShapeDtypeStruct((n_slots, d), F32),
        compiler_params=_cparams("arbitrary", "arbitrary"),
        name="moe_gmm",
    )(block_e, n_used, xb, wg, wu, wd)


def _moe_combine_kernel(h_ref, mod_ref, route_ref, y1_ref, y2_ref, o_ref):
    r = route_ref[0]
    y = r[:, 2:3] * y1_ref[0] + r[:, 3:4] * y2_ref[0]
    o_ref[0] = h_ref[0] + mod_ref[0, 5:6, :] * y


def moe_combine(h, mod, route, y1, y2):
    b, t, d = h.shape
    tm = TOKEN_TILE
    tok = lambda bi, i: (bi, i, 0)
    return pl.pallas_call(
        _moe_combine_kernel,
        grid=(b, t // tm),
        in_specs=[pl.BlockSpec((1, tm, d), tok),
                  pl.BlockSpec((1, 6, d), lambda bi, i: (bi, 0, 0)),
                  pl.BlockSpec((1, tm, 8), tok),
                  pl.BlockSpec((1, tm, d), tok),
                  pl.BlockSpec((1, tm, d), tok)],
        out_specs=pl.BlockSpec((1, tm, d), tok),
        out_shape=jax.ShapeDtypeStruct((b, t, d), F32),
        compiler_params=_cparams("parallel", "parallel"),
        name="moe_combine",
    )(h, mod, route, y1, y2)


def _moe_dispatch_plan(expert_idx, n_experts):
    n = expert_idx.shape[0]
    tile = MOE_TILE
    flat_e = expert_idx.reshape(-1)
    onehot = (flat_e[None, :] == jnp.arange(n_experts, dtype=jnp.int32)[:, None]).astype(jnp.int32)
    csum = jnp.cumsum(onehot, axis=1)
    counts = csum[:, -1]
    padded = (counts + tile - 1) // tile * tile
    pad_ends = jnp.cumsum(padded)
    pad_starts = pad_ends - padded
    slot = jnp.sum(onehot * (pad_starts[:, None] + csum - 1), axis=0)
    n_blocks = -(-(n * TOP_K) // tile) + n_experts
    n_used = pad_ends[-1] // tile
    blk = jnp.arange(n_blocks, dtype=jnp.int32)
    block_e = jnp.minimum(jnp.searchsorted(pad_ends, blk * tile, side="right"), n_experts - 1).astype(jnp.int32)
    block_e = jnp.where(blk < n_used, block_e, block_e[jnp.maximum(n_used - 1, 0)])
    slot_tok = jnp.zeros((n_blocks * tile,), jnp.int32).at[slot].set(
        jnp.arange(n * TOP_K, dtype=jnp.int32) // TOP_K, unique_indices=True, mode="promise_in_bounds")
    return slot.reshape(n, TOP_K), slot_tok, block_e, n_used.astype(jnp.int32).reshape(1)


def _rope_tables(n_ctx, n_lat):
    rows = n_lat // GRID_W
    row = jnp.repeat(jnp.arange(rows, dtype=F32), GRID_W)
    col = jnp.tile(jnp.arange(GRID_W, dtype=F32), rows)
    n_freq = HEAD_DIM // 4
    inv_freq = ROPE_THETA ** (-jnp.arange(n_freq, dtype=F32) / n_freq)
    ang = jnp.concatenate([row[:, None] * inv_freq, col[:, None] * inv_freq], -1)
    cos, sin = jnp.cos(ang), jnp.sin(ang)
    cos_t = jnp.concatenate([cos, cos, cos, cos], axis=1)
    sin_t = jnp.concatenate([-sin, sin, -sin, sin], axis=1)
    cos_t = jnp.concatenate([jnp.ones((n_ctx, 128), F32), cos_t], axis=0)
    sin_t = jnp.concatenate([jnp.zeros((n_ctx, 128), F32), sin_t], axis=0)
    return cos_t, sin_t


def _block_ones(n):
    idx = jnp.arange(n) // HEAD_DIM
    return (idx[:, None] == idx[None, :]).astype(BF16)


def kernel(x, c, ctx, c_ctx, mod_w, mod_b, norm1_w, norm2_w, w_in, attn_qn_w, attn_kn_w, ret_decay, ret_norm_w,
           mlstm_conv_w, mlstm_gate_b, mlstm_norm_w, w_out, ffn_w_gate, ffn_w_up, ffn_w_down, router_w, router_b,
           moe_w_gate, moe_w_up, moe_w_down):
    b, n_lat, d = x.shape
    n_ctx = ctx.shape[1]
    depth = mod_w.shape[0]
    n_experts = router_w.shape[2]
    ret_heads = ret_decay.shape[2]
    ml_heads = mlstm_gate_b.shape[2]
    ret_w, ml_w = ret_heads * HEAD_DIM, ml_heads * HEAD_DIM
    n_gate = N_GATES * ml_heads
    kv_w = (w_in.shape[2] - 4 * ret_w - 4 * ml_w - n_gate) // 6
    att_w = 4 * kv_w
    dims = (att_w, kv_w, ret_w, ml_w, n_gate)
    assert n_ctx % TOKEN_TILE == 0 and n_lat % KV_TILE == 0 and b + 1 <= 16
    n_main = w_in.shape[2] - n_gate
    n_pad = -(-w_in.shape[2] // 128) * 128

    cos_t, sin_t = _rope_tables(n_ctx, n_lat)
    gq, gk, g_ret, g_ml = _block_ones(att_w), _block_ones(kv_w), _block_ones(ret_w), _block_ones(ml_w)
    cond = jnp.concatenate([c, c_ctx[None, :], jnp.zeros((16 - b - 1, d), F32)], axis=0)
    ctx_tiles = n_ctx // TOKEN_TILE
    ctx_chunks = n_ctx // CHUNK

    h = jnp.concatenate([ctx, x], axis=1)
    for layer in range(depth):
        last = layer == depth - 1
        skip = ctx_tiles if last else 0
        mod = adaln(cond, mod_w[layer], mod_b[layer]).reshape(16, 6, d)[:b + 1]
        w_all = jnp.pad(w_in[layer].astype(BF16), ((0, 0), (0, n_pad - w_in.shape[2])))
        w_gate_t = w_in[layer][:, n_main:].T.astype(BF16)
        qn = jnp.tile(attn_qn_w[layer], att_w // HEAD_DIM)[None, :]
        kn = jnp.tile(attn_kn_w[layer], kv_w // HEAD_DIM)[None, :]
        k_norm_max = math.sqrt(HEAD_DIM) * jnp.max(jnp.abs(attn_kn_w[layer])) * (1.0 + 2.0 ** -6)
        q_norm_max = math.sqrt(HEAD_DIM) * jnp.max(jnp.abs(attn_qn_w[layer])) * (QK_SCALE * LOG2E)
        fast = (2.0 * q_norm_max * k_norm_max <= MAX_SCORE_SPAN_LOG2).astype(jnp.int32).reshape(1)
        kbound = jnp.full((1, att_w), q_norm_max * k_norm_max, F32)
        q, k, v, r, m, gcol, grow = proj_in(h, mod, norm1_w[layer][None, :], cos_t, sin_t, w_all, w_gate_t, qn, kn,
                                            kbound, gq, gk, n_ctx=n_ctx, dims=dims)
        att = attention(fast, q, k, v, n_ctx=n_ctx, skip=skip)
        dec = jnp.repeat(ret_decay[layer], HEAD_DIM, axis=1)
        sf, sb = ret_states(r, dec, n_ctx=n_ctx)
        ret = ret_out(r, sf, sb, dec, ret_norm_w[layer][None, :], g_ret, skip_chunks=ctx_chunks if last else 0)
        qk = mlstm_conv(m, mlstm_conv_w[layer], n_ctx=n_ctx)
        bias = mlstm_gate_b[layer].reshape(-1)
        cf, nmf, cb, nmb = mlstm_states(qk, m, gcol, bias[None, :], n_ctx=n_ctx, n_heads=ml_heads)
        ml = mlstm_out(qk, m, gcol, grow, bias[None, :], bias[:, None], cf, nmf, cb, nmb, mlstm_norm_w[layer][None, :],
                       g_ml, skip_chunks=ctx_chunks if last else 0, n_heads=ml_heads)
        h = proj_out(att, ret, ml, h, mod, w_out[layer].astype(BF16), n_ctx=n_ctx, skip=skip)
        if layer % 2 == 0:
            li = layer // 2
            assert not last, "dense FFN on a latent-only stream is not wired up"
            h = ffn_dense(h, mod, norm2_w[layer][None, :], ffn_w_gate[li].astype(BF16), ffn_w_up[li].astype(BF16),
                          ffn_w_down[li].astype(BF16), n_ctx=n_ctx)
        else:
            li = layer // 2
            assert last, "expert FFN on the combined stream is not wired up"
            mod_lat = mod[:b]
            rw = jnp.pad(router_w[li], ((0, 0), (0, 128 - n_experts)))
            rb = jnp.pad(router_b[li], (0, 128 - n_experts))[None, :]
            xn, route = moe_pre(h, mod_lat, norm2_w[layer][None, :], rw, rb, n_experts=n_experts)
            t = h.shape[1]
            route_flat = route.reshape(b * t, 8)
            slot, slot_tok, block_e, n_used = _moe_dispatch_plan(route_flat[:, 0:TOP_K].astype(jnp.int32), n_experts)
            xb = xn.reshape(b * t, d).at[slot_tok].get(mode="promise_in_bounds")
            yb = moe_gmm(block_e, n_used, xb, moe_w_gate[li].astype(BF16), moe_w_up[li].astype(BF16),
                         moe_w_down[li].astype(BF16))
            y1 = yb.at[slot[:, 0]].get(mode="promise_in_bounds").reshape(b, t, d)
            y2 = yb.at[slot[:, 1]].get(mode="promise_in_bounds").reshape(b, t, d)
            h = moe_combine(h, mod_lat, route, y1, y2)
    return h if h.shape[1] == n_lat else h[:, n_ctx:, :]
```

```python
import functools
import math

import jax
import jax.numpy as jnp
from jax import lax
from jax.experimental import pallas as pl
from jax.experimental.pallas import tpu as pltpu

F32 = jnp.float32
BF16 = jnp.bfloat16

HEAD_DIM = 64
GRID_W = 64
ROPE_THETA = 10000.0
EPS = 1e-6
CHUNK = 128
SUB_CHUNKS = 2
TOKEN_TILE = 256
KV_TILE = 1024
N_GATES = 4
TOP_K = 2
MOE_TILE = 512
MOE_FF_TILE = 1792
VMEM_LIMIT = 56 * 1024 * 1024
QK_SCALE = HEAD_DIM ** -0.5
LOG2E = math.log2(math.e)
MAX_SCORE_SPAN_LOG2 = 100.0


def _cparams(*sem):
    return pltpu.CompilerParams(dimension_semantics=sem, vmem_limit_bytes=VMEM_LIMIT)


def _silu(x):
    return x * jax.nn.sigmoid(x)


def _split_dot(x, ones_bf16, parts=2):
    acc = None
    r = x
    for _ in range(parts):
        hi = r.astype(BF16)
        d = jnp.dot(hi, ones_bf16, preferred_element_type=F32)
        acc = d if acc is None else acc + d
        r = r - hi.astype(F32)
    return acc


def _dot_nt(a, b):
    return lax.dot_general(a, b, (((1,), (1,)), ((), ())), preferred_element_type=F32)


def _rope(x, cos_t, sin_t):
    w = x.shape[1]
    reps = w // 128
    c = jnp.concatenate([cos_t] * reps, axis=1) if reps > 1 else cos_t
    s = jnp.concatenate([sin_t] * reps, axis=1) if reps > 1 else sin_t
    lane = lax.broadcasted_iota(jnp.int32, x.shape, 1)
    first_half = (lane % HEAD_DIM) < (HEAD_DIM // 2)
    swapped = jnp.where(first_half, pltpu.roll(x, w - HEAD_DIM // 2, 1), pltpu.roll(x, HEAD_DIM // 2, 1))
    return x * c + swapped * s


def _group_rms(x, gmat, gain):
    x2 = x * x
    blk = 256
    if x.shape[1] > blk and x.shape[1] % blk == 0:
        ms = jnp.concatenate([_split_dot(x2[:, c:c + blk], gmat[c:c + blk, c:c + blk])
                              for c in range(0, x.shape[1], blk)], axis=1)
    else:
        ms = _split_dot(x2, gmat)
    return x * lax.rsqrt(ms * (1.0 / HEAD_DIM) + EPS) * gain


def _group_layer_norm(x, gmat, gain):
    mu = _split_dot(x, gmat) * (1.0 / HEAD_DIM)
    xc = x - mu
    var = _split_dot(xc * xc, gmat) * (1.0 / HEAD_DIM)
    return xc * lax.rsqrt(var + EPS) * gain


def _expand_heads(x, n_heads, lane):
    out = x[:, n_heads - 1:n_heads]
    for h in range(n_heads - 2, -1, -1):
        out = jnp.where(lane < (h + 1) * HEAD_DIM, x[:, h:h + 1], out)
    return out


def _block_diag_mask(n):
    r = lax.broadcasted_iota(jnp.int32, (n, n), 0) // HEAD_DIM
    c = lax.broadcasted_iota(jnp.int32, (n, n), 1) // HEAD_DIM
    return r == c


def _adaln_kernel(c_ref, w_ref, b_ref, o_ref):
    a = _silu(c_ref[...]).astype(BF16)
    o_ref[...] = jnp.dot(a, w_ref[...].astype(BF16), preferred_element_type=F32) + b_ref[...]


def adaln(cond, w, b):
    r, d = cond.shape
    n = w.shape[1]
    tn = 1536 if n % 1536 == 0 else n
    return pl.pallas_call(
        _adaln_kernel,
        grid=(n // tn,),
        in_specs=[pl.BlockSpec((r, d), lambda j: (0, 0)),
                  pl.BlockSpec((d, tn), lambda j: (0, j)),
                  pl.BlockSpec((1, tn), lambda j: (0, j))],
        out_specs=pl.BlockSpec((r, tn), lambda j: (0, j)),
        out_shape=jax.ShapeDtypeStruct((r, n), F32),
        compiler_params=_cparams("arbitrary"),
        name="adaln",
    )(cond, w, b.reshape(1, n))


def _norm_mod(x, nw, shift, scale):
    ms = jnp.mean(x * x, axis=-1, keepdims=True)
    y = x * lax.rsqrt(ms + EPS) * nw
    return y * (1.0 + scale) + shift


def _proj_in_kernel(h_ref, mod_ref, nw_ref, cos_ref, sin_ref, w_ref, wgt_ref, qn_ref, kn_ref, kb_ref, gq_ref, gk_ref,
                    q_ref, k_ref, v_ref, r_ref, m_ref, gc_ref, gr_ref, *, dims):
    att_w, kv_w, ret_w, ml_w, n_gate = dims
    x = h_ref[0]
    a = _norm_mod(x, nw_ref[...], mod_ref[0, 0:1, :], mod_ref[0, 1:2, :]).astype(BF16)
    p = jnp.dot(a, w_ref[...], preferred_element_type=F32)
    cos_t, sin_t = cos_ref[...], sin_ref[...]
    o = 0
    aq = p[:, o:o + att_w]; o += att_w
    ak = p[:, o:o + kv_w]; o += kv_w
    av = p[:, o:o + kv_w]; o += kv_w
    aq = _rope(_group_rms(aq, gq_ref[...], qn_ref[...]), cos_t, sin_t) * (QK_SCALE * LOG2E)
    ak = _rope(_group_rms(ak, gk_ref[...], kn_ref[...]), cos_t, sin_t)
    neg_bound = -kb_ref[...]
    tm = x.shape[0]
    lane = lax.broadcasted_iota(jnp.int32, (tm, 2 * HEAD_DIM), 1)

    def widen(a, extra):
        return jnp.where(lane == HEAD_DIM, extra, jnp.concatenate([a, jnp.zeros_like(a)], axis=1)).astype(BF16)

    for hh in range(att_w // HEAD_DIM):
        cols = slice(hh * HEAD_DIM, (hh + 1) * HEAD_DIM)
        q_ref[0, hh] = widen(aq[:, cols], neg_bound[:, hh * HEAD_DIM:hh * HEAD_DIM + 1])
    for hh in range(kv_w // HEAD_DIM):
        cols = slice(hh * HEAD_DIM, (hh + 1) * HEAD_DIM)
        k_ref[0, hh] = widen(ak[:, cols], 1.0)
        v_ref[0, hh] = widen(av[:, cols], 1.0)
    rq = _rope(p[:, o:o + ret_w], cos_t, sin_t)
    rk = _rope(p[:, o + ret_w:o + 2 * ret_w] * QK_SCALE, cos_t, sin_t)
    r_ref[0, :, 0:ret_w] = rq
    r_ref[0, :, ret_w:2 * ret_w] = rk
    r_ref[0, :, 2 * ret_w:4 * ret_w] = p[:, o + 2 * ret_w:o + 4 * ret_w]
    o += 4 * ret_w
    m_ref[0] = p[:, o:o + 4 * ml_w]
    o += 4 * ml_w
    gc_ref[0] = p[:, o:o + n_gate]
    gr_ref[0] = _dot_nt(wgt_ref[...], a)


def proj_in(h, mod, norm_w, cos_t, sin_t, w_all, w_gate_t, qn, kn, kbound, gq, gk, *, n_ctx, dims):
    att_w, kv_w, ret_w, ml_w, n_gate = dims
    b, s, d = h.shape
    tm = TOKEN_TILE
    n_ctx_tiles = n_ctx // tm
    n_pad = w_all.shape[1]
    ctx_row = mod.shape[0] - 1

    def mod_map(bi, i):
        return (jnp.where(i < n_ctx_tiles, ctx_row, bi), 0, 0)

    const2 = lambda bi, i: (0, 0)
    out_shape = (
        jax.ShapeDtypeStruct((b, att_w // HEAD_DIM, s, 2 * HEAD_DIM), BF16),
        jax.ShapeDtypeStruct((b, kv_w // HEAD_DIM, s, 2 * HEAD_DIM), BF16),
        jax.ShapeDtypeStruct((b, kv_w // HEAD_DIM, s, 2 * HEAD_DIM), BF16),
        jax.ShapeDtypeStruct((b, s, 4 * ret_w), F32),
        jax.ShapeDtypeStruct((b, s, 4 * ml_w), F32),
        jax.ShapeDtypeStruct((b, s, n_gate), F32),
        jax.ShapeDtypeStruct((b, n_gate, s), F32),
    )
    return pl.pallas_call(
        functools.partial(_proj_in_kernel, dims=dims),
        grid=(b, s // tm),
        in_specs=[
            pl.BlockSpec((1, tm, d), lambda bi, i: (bi, i, 0)),
            pl.BlockSpec((1, 6, d), mod_map),
            pl.BlockSpec((1, d), const2),
            pl.BlockSpec((tm, 128), lambda bi, i: (i, 0)),
            pl.BlockSpec((tm, 128), lambda bi, i: (i, 0)),
            pl.BlockSpec((d, n_pad), const2),
            pl.BlockSpec((n_gate, d), const2),
            pl.BlockSpec((1, att_w), const2),
            pl.BlockSpec((1, kv_w), const2),
            pl.BlockSpec((1, att_w), const2),
            pl.BlockSpec((att_w, att_w), const2),
            pl.BlockSpec((kv_w, kv_w), const2),
        ],
        out_specs=(
            pl.BlockSpec((1, att_w // HEAD_DIM, tm, 2 * HEAD_DIM), lambda bi, i: (bi, 0, i, 0)),
            pl.BlockSpec((1, kv_w // HEAD_DIM, tm, 2 * HEAD_DIM), lambda bi, i: (bi, 0, i, 0)),
            pl.BlockSpec((1, kv_w // HEAD_DIM, tm, 2 * HEAD_DIM), lambda bi, i: (bi, 0, i, 0)),
            pl.BlockSpec((1, tm, 4 * ret_w), lambda bi, i: (bi, i, 0)),
            pl.BlockSpec((1, tm, 4 * ml_w), lambda bi, i: (bi, i, 0)),
            pl.BlockSpec((1, tm, n_gate), lambda bi, i: (bi, i, 0)),
            pl.BlockSpec((1, n_gate, tm), lambda bi, i: (bi, 0, i)),
        ),
        out_shape=out_shape,
        compiler_params=_cparams("parallel", "parallel"),
        name="proj_in",
    )(h, mod, norm_w, cos_t, sin_t, w_all, w_gate_t, qn, kn, kbound, gq, gk)


def _conv_kernel(x_ref, prev_ref, next_ref, w_ref, o_ref, *, n_ctx, seq, half):
    i = pl.program_id(1)
    x = x_ref[0]
    tm, w = x.shape
    row = lax.broadcasted_iota(jnp.int32, (tm, 1), 0)
    grow = row + i * tm
    xp = jnp.where(row == 0, prev_ref[0, 7:8, :], pltpu.roll(x, 1, 0))
    xn = jnp.where(row == tm - 1, next_ref[0, 0:1, :], pltpu.roll(x, tm - 1, 0))
    xp = jnp.where((grow == 0) | (grow == n_ctx), 0.0, xp)
    xn = jnp.where((grow == n_ctx - 1) | (grow == seq - 1), 0.0, xn)
    y = _silu(xp * w_ref[0:1, :] + x * w_ref[1:2, :] + xn * w_ref[2:3, :])
    lane = lax.broadcasted_iota(jnp.int32, (tm, w), 1)
    o_ref[0] = jnp.where(lane >= half, y * QK_SCALE, y)


def mlstm_conv(m, conv_w, *, n_ctx):
    b, s, _ = m.shape
    w = conv_w.shape[1]
    tm = TOKEN_TILE
    r8 = tm // 8
    last8 = s // 8 - 1
    return pl.pallas_call(
        functools.partial(_conv_kernel, n_ctx=n_ctx, seq=s, half=w // 2),
        grid=(b, s // tm),
        in_specs=[pl.BlockSpec((1, tm, w), lambda bi, i: (bi, i, 0)),
                  pl.BlockSpec((1, 8, w), lambda bi, i: (bi, jnp.maximum(i * r8 - 1, 0), 0)),
                  pl.BlockSpec((1, 8, w), lambda bi, i: (bi, jnp.minimum((i + 1) * r8, last8), 0)),
                  pl.BlockSpec((3, w), lambda bi, i: (0, 0))],
        out_specs=pl.BlockSpec((1, tm, w), lambda bi, i: (bi, i, 0)),
        out_shape=jax.ShapeDtypeStruct((b, s, w), F32),
        compiler_params=_cparams("parallel", "parallel"),
        name="mlstm_conv",
    )(m, m, m, conv_w)


def _attn_kernel(fast_ref, q_ref, k_ref, v_ref, o_ref, m_sc, acc_sc, *, n_ctx, n_lat_tiles, skip):
    i = pl.program_id(2) + skip
    g, tq = q_ref.shape[1], q_ref.shape[2]
    hd = HEAD_DIM

    def fast_step(k, v):
        for h in range(g):
            rows = slice(h * tq, (h + 1) * tq)
            p = jnp.exp2(_dot_nt(q_ref[0, h], k))
            acc_sc[rows] += jnp.dot(p.astype(BF16), v, preferred_element_type=F32)

    def safe_step(k, v):
        for h in range(g):
            rows = slice(h * tq, (h + 1) * tq)
            s = _dot_nt(q_ref[0, h], k)
            m_old = m_sc[rows]
            m_new = jnp.maximum(m_old, jnp.max(s, axis=-1, keepdims=True))
            p = jnp.exp2(s - jnp.concatenate([m_new] * (s.shape[1] // 128), axis=1))
            acc_sc[rows] = jnp.exp2(m_old - m_new) * acc_sc[rows] + jnp.dot(p.astype(BF16), v, preferred_element_type=F32)
            m_sc[rows] = m_new

    def sweep(step, unroll):
        step(k_ref[0, 0, 0:n_ctx, :], v_ref[0, 0, 0:n_ctx, :])

        @pl.when(i * tq >= n_ctx)
        def _():
            def body(j, carry):
                start = pl.multiple_of(n_ctx + j * KV_TILE, math.gcd(n_ctx, KV_TILE))
                step(k_ref[0, 0, pl.ds(start, KV_TILE), :], v_ref[0, 0, pl.ds(start, KV_TILE), :])
                return carry
            lax.fori_loop(0, n_lat_tiles, body, 0, unroll=unroll)

    acc_sc[...] = jnp.zeros(acc_sc.shape, F32)

    @pl.when(fast_ref[0] == 1)
    def _():
        sweep(fast_step, 4)

    @pl.when(fast_ref[0] != 1)
    def _():
        m_sc[...] = jnp.full(m_sc.shape, -jnp.inf, F32)
        sweep(safe_step, 1)

    acc = acc_sc[...]
    out = acc[:, 0:hd] / acc[:, hd:hd + 1]
    o_ref[0] = jnp.concatenate([out[h * tq:(h + 1) * tq] for h in range(g)], axis=1).astype(o_ref.dtype)


def attention(fast, q, k, v, *, n_ctx, skip):
    b, hq, s, lanes = q.shape
    hkv = k.shape[1]
    g = hq // hkv
    tq = TOKEN_TILE
    n_lat = s - n_ctx
    assert n_lat % KV_TILE == 0 and n_ctx % tq == 0 and lanes == 2 * HEAD_DIM
    nq = s // tq - skip
    grid_spec = pltpu.PrefetchScalarGridSpec(
        num_scalar_prefetch=1,
        grid=(b, hkv, nq),
        in_specs=[pl.BlockSpec((1, g, tq, lanes), lambda bi, gi, i, f: (bi, gi, i + skip, 0)),
                  pl.BlockSpec((1, 1, s, lanes), lambda bi, gi, i, f: (bi, gi, 0, 0)),
                  pl.BlockSpec((1, 1, s, lanes), lambda bi, gi, i, f: (bi, gi, 0, 0))],
        out_specs=pl.BlockSpec((1, tq, g * HEAD_DIM), lambda bi, gi, i, f: (bi, i, gi)),
        scratch_shapes=[pltpu.VMEM((g * tq, 128), F32), pltpu.VMEM((g * tq, lanes), F32)],
    )
    return pl.pallas_call(
        functools.partial(_attn_kernel, n_ctx=n_ctx, n_lat_tiles=n_lat // KV_TILE, skip=skip),
        grid_spec=grid_spec,
        out_shape=jax.ShapeDtypeStruct((b, nq * tq, hq * HEAD_DIM), BF16),
        compiler_params=_cparams("parallel", "parallel", "arbitrary"),
        name="attention",
    )(fast, q, k, v)


def _rev_chunk(c, nc_ctx, nc):
    return jnp.where(c < nc_ctx, nc_ctx - 1 - c, nc - 1 + nc_ctx - c)


def _ret_state_kernel(kf_ref, vf_ref, kb_ref, vb_ref, dec_ref, sf_ref, sb_ref, s_sc):
    c = pl.program_id(1)

    @pl.when(c == 0)
    def _():
        s_sc[...] = jnp.zeros(s_sc.shape, F32)

    w = kf_ref.shape[2]
    lg = jnp.log1p(-jnp.exp(dec_ref[...]))
    pos = lax.broadcasted_iota(jnp.int32, (CHUNK, 1), 0).astype(F32)
    bd = _block_diag_mask(w)
    for d, (k_ref, v_ref, out_ref) in enumerate(((kf_ref, vf_ref, sf_ref), (kb_ref, vb_ref, sb_ref))):
        lgd = lg[d:d + 1, :]
        e = (CHUNK - 1.0 - pos) if d == 0 else pos
        kscale = jnp.exp(e * lgd)
        chunk_decay = jnp.exp(CHUNK * lgd)
        for sub in (range(SUB_CHUNKS) if d == 0 else reversed(range(SUB_CHUNKS))):
            rows = slice(sub * CHUNK, (sub + 1) * CHUNK)
            kdec = k_ref[0, rows, :] * kscale
            inc = jnp.dot(kdec.T.astype(BF16), v_ref[0, rows, :].astype(BF16), preferred_element_type=F32)
            s_old = s_sc[d]
            out_ref[0, sub] = s_old.astype(BF16)
            s_sc[d] = chunk_decay * s_old + jnp.where(bd, inc, 0.0)


def ret_states(r, dec, *, n_ctx):
    b, s, w4 = r.shape
    w = w4 // 4
    nc, nc_ctx = s // CHUNK, n_ctx // CHUNK
    assert nc % SUB_CHUNKS == 0 and nc_ctx % SUB_CHUNKS == 0
    ng, rows = nc // SUB_CHUNKS, SUB_CHUNKS * CHUNK
    rev = lambda c: _rev_chunk(c, nc_ctx // SUB_CHUNKS, ng)
    st = jax.ShapeDtypeStruct((b, nc, w, w), BF16)
    return pl.pallas_call(
        _ret_state_kernel,
        grid=(b, ng),
        in_specs=[pl.BlockSpec((1, rows, w), lambda bi, c: (bi, c, 1)),
                  pl.BlockSpec((1, rows, w), lambda bi, c: (bi, c, 2)),
                  pl.BlockSpec((1, rows, w), lambda bi, c: (bi, rev(c), 1)),
                  pl.BlockSpec((1, rows, w), lambda bi, c: (bi, rev(c), 2)),
                  pl.BlockSpec((2, w), lambda bi, c: (0, 0))],
        out_specs=(pl.BlockSpec((1, SUB_CHUNKS, w, w), lambda bi, c: (bi, c, 0, 0)),
                   pl.BlockSpec((1, SUB_CHUNKS, w, w), lambda bi, c: (bi, rev(c), 0, 0))),
        out_shape=(st, st),
        scratch_shapes=[pltpu.VMEM((2, w, w), F32)],
        compiler_params=_cparams("parallel", "arbitrary"),
        name="ret_states",
    )(r, r, r, r, dec)


def _ret_out_kernel(r_ref, sf_ref, sb_ref, dec_ref, nw_ref, gmat_ref, o_ref):
    w = o_ref.shape[2]
    n_heads = w // HEAD_DIM
    lg = jnp.log1p(-jnp.exp(dec_ref[...]))
    pos = lax.broadcasted_iota(jnp.int32, (CHUNK, 1), 0).astype(F32)
    row = lax.broadcasted_iota(jnp.int32, (CHUNK, CHUNK), 0)
    col = lax.broadcasted_iota(jnp.int32, (CHUNK, CHUNK), 1)
    lane = lax.broadcasted_iota(jnp.int32, (CHUNK, w), 1)
    head_masks = [(lane >= h * HEAD_DIM) & (lane < (h + 1) * HEAD_DIM) for h in range(n_heads)]
    q_scale = []
    decay = [jnp.zeros((CHUNK, CHUNK), F32)] * n_heads
    for d in range(2):
        lgd = lg[d:d + 1, :]
        q_scale.append(jnp.exp(((pos + 1.0) if d == 0 else (CHUNK - pos)) * lgd))
        lag = ((row - col) if d == 0 else (col - row)).astype(F32)
        decay = [decay[h] + jnp.where(lag >= 0, jnp.exp(jnp.maximum(lag, 0.0) * lgd[:, h * HEAD_DIM:h * HEAD_DIM + 1]), 0.0)
                 for h in range(n_heads)]
    for sub in range(SUB_CHUNKS):
        x = r_ref[0, sub * CHUNK:(sub + 1) * CHUNK, :]
        q, k, v, g = x[:, 0:w], x[:, w:2 * w], x[:, 2 * w:3 * w], x[:, 3 * w:4 * w]
        kb, vb = k.astype(BF16), v.astype(BF16)
        total = jnp.dot((q * q_scale[0]).astype(BF16), sf_ref[0, sub], preferred_element_type=F32)
        total = total + jnp.dot((q * q_scale[1]).astype(BF16), sb_ref[0, sub], preferred_element_type=F32)
        for h in range(n_heads):
            sc = _dot_nt(jnp.where(head_masks[h], q, 0.0).astype(BF16), kb)
            pv = jnp.dot((sc * decay[h]).astype(BF16), vb, preferred_element_type=F32)
            total = total + jnp.where(head_masks[h], pv, 0.0)
        o_ref[0, sub * CHUNK:(sub + 1) * CHUNK, :] = (
            _group_layer_norm(total, gmat_ref[...], nw_ref[...]) * _silu(g)).astype(o_ref.dtype)


def ret_out(r, sf, sb, dec, norm_w, gmat, *, skip_chunks):
    b, s, w4 = r.shape
    w = w4 // 4
    assert skip_chunks % SUB_CHUNKS == 0 and (s // CHUNK) % SUB_CHUNKS == 0
    sk = skip_chunks // SUB_CHUNKS
    ng, rows = (s // CHUNK) // SUB_CHUNKS - sk, SUB_CHUNKS * CHUNK
    return pl.pallas_call(
        _ret_out_kernel,
        grid=(b, ng),
        in_specs=[pl.BlockSpec((1, rows, w4), lambda bi, c: (bi, c + sk, 0)),
                  pl.BlockSpec((1, SUB_CHUNKS, w, w), lambda bi, c: (bi, c + sk, 0, 0)),
                  pl.BlockSpec((1, SUB_CHUNKS, w, w), lambda bi, c: (bi, c + sk, 0, 0)),
                  pl.BlockSpec((2, w), lambda bi, c: (0, 0)),
                  pl.BlockSpec((1, w), lambda bi, c: (0, 0)),
                  pl.BlockSpec((w, w), lambda bi, c: (0, 0))],
        out_specs=pl.BlockSpec((1, rows, w), lambda bi, c: (bi, c, 0)),
        out_shape=jax.ShapeDtypeStruct((b, ng * rows, w), BF16),
        compiler_params=_cparams("parallel", "parallel"),
        name="ret_out",
    )(r, sf, sb, dec, norm_w, gmat)


def _tri_masks():
    row = lax.broadcasted_iota(jnp.int32, (CHUNK, CHUNK), 0)
    col = lax.broadcasted_iota(jnp.int32, (CHUNK, CHUNK), 1)
    return col <= row, col >= row


def _ones_dot(ones_bf16, x, parts=3):
    acc = None
    r = x
    for _ in range(parts):
        hi = r.astype(BF16)
        d = jnp.dot(ones_bf16, hi, preferred_element_type=F32)
        acc = d if acc is None else acc + d
        r = r - hi.astype(F32)
    return acc


def _mlstm_state_kernel(kf_ref, vf_ref, gf_ref, kb_ref, vb_ref, gb_ref, bias_ref,
                        cf_ref, nmf_ref, cb_ref, nmb_ref, c_sc, nm_sc, *, n_heads):
    c = pl.program_id(1)

    @pl.when(c == 0)
    def _():
        c_sc[...] = jnp.zeros(c_sc.shape, F32)
        nm_sc[...] = jnp.zeros(nm_sc.shape, F32)

    w = kf_ref.shape[2]
    hh = n_heads
    tril, triu = _tri_masks()
    lane_t = lax.broadcasted_iota(jnp.int32, (CHUNK, w), 1)
    lane_1 = lax.broadcasted_iota(jnp.int32, (1, w), 1)
    bd = _block_diag_mask(w)
    dirs = ((kf_ref, vf_ref, gf_ref, cf_ref, nmf_ref, tril), (kb_ref, vb_ref, gb_ref, cb_ref, nmb_ref, triu))
    for d, (k_ref, v_ref, g_ref, c_out, nm_out, tmat) in enumerate(dirs):
        for sub in (range(SUB_CHUNKS) if d == 0 else reversed(range(SUB_CHUNKS))):
            rows = slice(sub * CHUNK, (sub + 1) * CHUNK)
            pre = g_ref[0, rows, :] + bias_ref[...]
            ig = pre[:, d * hh:(d + 1) * hh]
            lf_all = jax.nn.log_sigmoid(pre)
            b_all = _ones_dot(tmat.astype(BF16), lf_all)
            lf = lf_all[:, (2 + d) * hh:(3 + d) * hh]
            bcum = b_all[:, (2 + d) * hh:(3 + d) * hh]
            b_end = jnp.sum(lf, axis=0, keepdims=True)
            a = b_end - bcum + ig
            m_loc = jnp.max(a, axis=0, keepdims=True)
            wgt = jnp.exp(a - m_loc)
            kw = k_ref[0, rows, :] * _expand_heads(wgt, hh, lane_t)
            c_inc = jnp.dot(kw.T.astype(BF16), v_ref[0, rows, :].astype(BF16), preferred_element_type=F32)
            n_inc = jnp.sum(kw, axis=0, keepdims=True)
            be = _expand_heads(b_end, hh, lane_1)
            ml = _expand_heads(m_loc, hh, lane_1)
            c_old = c_sc[d]
            n_old = nm_sc[d, 0:1, :]
            m_old = nm_sc[d, 1:2, :]
            c_out[0, sub] = c_old.astype(BF16)
            nm_out[0, sub] = nm_sc[d]
            m_new = jnp.maximum(be + m_old, ml)
            w_prev = jnp.exp(be + m_old - m_new)
            w_inc = jnp.exp(ml - m_new)
            c_sc[d] = w_prev * c_old + w_inc * jnp.where(bd, c_inc, 0.0)
            nm_sc[d, 0:1, :] = w_prev * n_old + w_inc * n_inc
            nm_sc[d, 1:2, :] = m_new


def mlstm_states(qk, m, gcol, bias_row, *, n_ctx, n_heads):
    b, s, w2 = qk.shape
    w = w2 // 2
    ng = gcol.shape[2]
    nc, nc_ctx = s // CHUNK, n_ctx // CHUNK
    assert nc % SUB_CHUNKS == 0 and nc_ctx % SUB_CHUNKS == 0
    n_groups, rows = nc // SUB_CHUNKS, SUB_CHUNKS * CHUNK
    rev = lambda c: _rev_chunk(c, nc_ctx // SUB_CHUNKS, n_groups)
    cst = jax.ShapeDtypeStruct((b, nc, w, w), BF16)
    nmst = jax.ShapeDtypeStruct((b, nc, 8, w), F32)
    fwd3 = lambda j: (lambda bi, c: (bi, c, j))
    bwd3 = lambda j: (lambda bi, c: (bi, rev(c), j))
    return pl.pallas_call(
        functools.partial(_mlstm_state_kernel, n_heads=n_heads),
        grid=(b, n_groups),
        in_specs=[pl.BlockSpec((1, rows, w), fwd3(1)),
                  pl.BlockSpec((1, rows, w), fwd3(2)),
                  pl.BlockSpec((1, rows, ng), fwd3(0)),
                  pl.BlockSpec((1, rows, w), bwd3(1)),
                  pl.BlockSpec((1, rows, w), bwd3(2)),
                  pl.BlockSpec((1, rows, ng), bwd3(0)),
                  pl.BlockSpec((1, ng), lambda bi, c: (0, 0))],
        out_specs=(pl.BlockSpec((1, SUB_CHUNKS, w, w), lambda bi, c: (bi, c, 0, 0)),
                   pl.BlockSpec((1, SUB_CHUNKS, 8, w), lambda bi, c: (bi, c, 0, 0)),
                   pl.BlockSpec((1, SUB_CHUNKS, w, w), lambda bi, c: (bi, rev(c), 0, 0)),
                   pl.BlockSpec((1, SUB_CHUNKS, 8, w), lambda bi, c: (bi, rev(c), 0, 0))),
        out_shape=(cst, nmst, cst, nmst),
        scratch_shapes=[pltpu.VMEM((2, w, w), F32), pltpu.VMEM((2, 8, w), F32)],
        compiler_params=_cparams("parallel", "arbitrary"),
        name="mlstm_states",
    )(qk, m, gcol, qk, m, gcol, bias_row)


def _mlstm_out_kernel(qk_ref, m_ref, gc_ref, gr_ref, bc_ref, br_ref, cf_ref, nmf_ref, cb_ref, nmb_ref,
                      nw_ref, gmat_ref, o_ref, *, n_heads):
    w = o_ref.shape[2]
    hh = n_heads
    tril, triu = _tri_masks()
    lane = lax.broadcasted_iota(jnp.int32, (CHUNK, w), 1)
    head_masks = [(lane >= h * HEAD_DIM) & (lane < (h + 1) * HEAD_DIM) for h in range(hh)]
    gmat = gmat_ref[...]
    for sub in range(SUB_CHUNKS):
        rows = slice(sub * CHUNK, (sub + 1) * CHUNK)
        qk = qk_ref[0, rows, :]
        q, k = qk[:, 0:w], qk[:, w:2 * w]
        mm = m_ref[0, rows, :]
        v, og = mm[:, 2 * w:3 * w], mm[:, 3 * w:4 * w]
        pre_c = gc_ref[0, rows, :] + bc_ref[...]
        pre_r = gr_ref[0, :, rows] + br_ref[...]
        lf_c = jax.nn.log_sigmoid(pre_c)
        lf_r = jax.nn.log_sigmoid(pre_r)
        kb, vb, qb = k.astype(BF16), v.astype(BF16), q.astype(BF16)
        scores = [_dot_nt(jnp.where(hm, q, 0.0).astype(BF16), kb) for hm in head_masks]
        total = jnp.zeros((CHUNK, w), F32)
        for d, (c_ref, nm_ref) in enumerate(((cf_ref, nmf_ref), (cb_ref, nmb_ref))):
            tmat = tril if d == 0 else triu
            tmat_t = triu if d == 0 else tril
            b_c = _ones_dot(tmat.astype(BF16), lf_c)
            b_r = _split_dot(lf_r, tmat_t.astype(BF16), parts=3)
            nm = nm_ref[0, sub]
            n_row, m_row = nm[0:1, :], nm[1:2, :]
            cross = jnp.dot(qb, c_ref[0, sub], preferred_element_type=F32)
            qn = _split_dot(q * n_row, gmat)
            num = jnp.zeros((CHUNK, w), F32)
            den_e = jnp.zeros((CHUNK, w), F32)
            wp_e = jnp.zeros((CHUNK, w), F32)
            mt_e = jnp.zeros((CHUNK, w), F32)
            for h in range(hh):
                fc, ic = (2 + d) * hh + h, d * hh + h
                hm = head_masks[h]
                bcol = b_c[:, fc:fc + 1]
                logd = jnp.where(tmat, bcol - b_r[fc:fc + 1, :] + pre_r[ic:ic + 1, :], -jnp.inf)
                gg = bcol + m_row[:, h * HEAD_DIM:h * HEAD_DIM + 1]
                m_t = jnp.maximum(gg, jnp.max(logd, axis=-1, keepdims=True))
                p = scores[h] * jnp.exp(logd - m_t)
                pv = jnp.dot(p.astype(BF16), vb, preferred_element_type=F32)
                num = num + jnp.where(hm, pv, 0.0)
                den_e = jnp.where(hm, jnp.sum(p, axis=-1, keepdims=True), den_e)
                wp_e = jnp.where(hm, jnp.exp(gg - m_t), wp_e)
                mt_e = jnp.where(hm, m_t, mt_e)
            num = num + wp_e * cross
            den = den_e + wp_e * qn
            total = total + num / jnp.maximum(jnp.abs(den), jnp.exp(-mt_e))
        o_ref[0, rows, :] = _group_layer_norm(total * jax.nn.sigmoid(og), gmat, nw_ref[...]).astype(o_ref.dtype)


def mlstm_out(qk, m, gcol, grow, bias_row, bias_col, cf, nmf, cb, nmb, norm_w, gmat, *, skip_chunks, n_heads):
    b, s, w2 = qk.shape
    w = w2 // 2
    ng = gcol.shape[2]
    assert skip_chunks % SUB_CHUNKS == 0 and (s // CHUNK) % SUB_CHUNKS == 0
    sk = skip_chunks // SUB_CHUNKS
    n_groups, rows = (s // CHUNK) // SUB_CHUNKS - sk, SUB_CHUNKS * CHUNK
    return pl.pallas_call(
        functools.partial(_mlstm_out_kernel, n_heads=n_heads),
        grid=(b, n_groups),
        in_specs=[pl.BlockSpec((1, rows, w2), lambda bi, c: (bi, c + sk, 0)),
                  pl.BlockSpec((1, rows, 4 * w), lambda bi, c: (bi, c + sk, 0)),
                  pl.BlockSpec((1, rows, ng), lambda bi, c: (bi, c + sk, 0)),
                  pl.BlockSpec((1, ng, rows), lambda bi, c: (bi, 0, c + sk)),
                  pl.BlockSpec((1, ng), lambda bi, c: (0, 0)),
                  pl.BlockSpec((ng, 1), lambda bi, c: (0, 0)),
                  pl.BlockSpec((1, SUB_CHUNKS, w, w), lambda bi, c: (bi, c + sk, 0, 0)),
                  pl.BlockSpec((1, SUB_CHUNKS, 8, w), lambda bi, c: (bi, c + sk, 0, 0)),
                  pl.BlockSpec((1, SUB_CHUNKS, w, w), lambda bi, c: (bi, c + sk, 0, 0)),
                  pl.BlockSpec((1, SUB_CHUNKS, 8, w), lambda bi, c: (bi, c + sk, 0, 0)),
                  pl.BlockSpec((1, w), lambda bi, c: (0, 0)),
                  pl.BlockSpec((w, w), lambda bi, c: (0, 0))],
        out_specs=pl.BlockSpec((1, rows, w), lambda bi, c: (bi, c, 0)),
        out_shape=jax.ShapeDtypeStruct((b, n_groups * rows, w), BF16),
        compiler_params=_cparams("parallel", "parallel"),
        name="mlstm_out",
    )(qk, m, gcol, grow, bias_row, bias_col, cf, nmf, cb, nmb, norm_w, gmat)


def _proj_out_kernel(att_ref, ret_ref, ml_ref, h_ref, mod_ref, w_ref, o_ref):
    aw, rw = att_ref.shape[2], ret_ref.shape[2]
    y = jnp.dot(att_ref[0].astype(BF16), w_ref[0:aw, :], preferred_element_type=F32)
    y = y + jnp.dot(ret_ref[0].astype(BF16), w_ref[aw:aw + rw, :], preferred_element_type=F32)
    y = y + jnp.dot(ml_ref[0].astype(BF16), w_ref[aw + rw:, :], preferred_element_type=F32)
    o_ref[0] = h_ref[0] + mod_ref[0, 2:3, :] * y


def _mod_map(n_ctx_tiles, ctx_row, skip):
    return lambda bi, i: (jnp.where(i + skip < n_ctx_tiles, ctx_row, bi), 0, 0)


def proj_out(att, ret, ml, h, mod, w_out, *, n_ctx, skip):
    b, s, d = h.shape
    tm = TOKEN_TILE
    nt = s // tm - skip
    tok = lambda bi, i: (bi, i, 0)
    return pl.pallas_call(
        _proj_out_kernel,
        grid=(b, nt),
        in_specs=[pl.BlockSpec((1, tm, att.shape[2]), tok),
                  pl.BlockSpec((1, tm, ret.shape[2]), tok),
                  pl.BlockSpec((1, tm, ml.shape[2]), tok),
                  pl.BlockSpec((1, tm, d), lambda bi, i: (bi, i + skip, 0)),
                  pl.BlockSpec((1, 6, d), _mod_map(n_ctx // tm, mod.shape[0] - 1, skip)),
                  pl.BlockSpec(w_out.shape, lambda bi, i: (0, 0))],
        out_specs=pl.BlockSpec((1, tm, d), tok),
        out_shape=jax.ShapeDtypeStruct((b, nt * tm, d), F32),
        compiler_params=_cparams("parallel", "parallel"),
        name="proj_out",
    )(att, ret, ml, h, mod, w_out)


def _ffn_kernel(h_ref, mod_ref, nw_ref, wg_ref, wu_ref, wd_ref, o_ref):
    x = h_ref[0]
    a = _norm_mod(x, nw_ref[...], mod_ref[0, 3:4, :], mod_ref[0, 4:5, :]).astype(BF16)
    hg = jnp.dot(a, wg_ref[...], preferred_element_type=F32)
    hu = jnp.dot(a, wu_ref[...], preferred_element_type=F32)
    act = (_silu(hg) * hu).astype(BF16)
    y = jnp.dot(act, wd_ref[...], preferred_element_type=F32)
    o_ref[0] = x + mod_ref[0, 5:6, :] * y


def ffn_dense(h, mod, norm_w, wg, wu, wd, *, n_ctx):
    b, s, d = h.shape
    tm = TOKEN_TILE
    tok = lambda bi, i: (bi, i, 0)
    const = lambda bi, i: (0, 0)
    resident = functools.partial(pl.BlockSpec, pipeline_mode=pl.Buffered(1))
    return pl.pallas_call(
        _ffn_kernel,
        grid=(b, s // tm),
        in_specs=[pl.BlockSpec((1, tm, d), tok),
                  pl.BlockSpec((1, 6, d), _mod_map(n_ctx // tm, mod.shape[0] - 1, 0)),
                  pl.BlockSpec((1, d), const),
                  resident(wg.shape, const),
                  resident(wu.shape, const),
                  resident(wd.shape, const)],
        out_specs=pl.BlockSpec((1, tm, d), tok),
        out_shape=jax.ShapeDtypeStruct((b, s, d), F32),
        compiler_params=_cparams("parallel", "parallel"),
        name="ffn_dense",
    )(h, mod, norm_w, wg, wu, wd)


def _moe_pre_kernel(h_ref, mod_ref, nw_ref, rw_ref, rb_ref, xn_ref, route_ref, *, n_experts):
    x = h_ref[0]
    a = _norm_mod(x, nw_ref[...], mod_ref[0, 3:4, :], mod_ref[0, 4:5, :])
    a_hi = a.astype(BF16)
    xn_ref[0] = a_hi
    a_lo = (a - a_hi.astype(F32)).astype(BF16)
    rw = rw_ref[...]
    w_hi = rw.astype(BF16)
    w_lo = (rw - w_hi.astype(F32)).astype(BF16)
    logits = (jnp.dot(a_hi, w_hi, preferred_element_type=F32) + jnp.dot(a_lo, w_hi, preferred_element_type=F32)
              + jnp.dot(a_hi, w_lo, preferred_element_type=F32)) + rb_ref[...]
    lane = lax.broadcasted_iota(jnp.int32, logits.shape, 1)
    l1 = jnp.where(lane < n_experts, logits, -jnp.inf)
    v1 = jnp.max(l1, axis=-1, keepdims=True)
    i1 = jnp.min(jnp.where(l1 == v1, lane, 128), axis=-1, keepdims=True)
    l2 = jnp.where(lane == i1, -jnp.inf, l1)
    v2 = jnp.max(l2, axis=-1, keepdims=True)
    i2 = jnp.min(jnp.where(l2 == v2, lane, 128), axis=-1, keepdims=True)
    e2 = jnp.exp(v2 - v1)
    w1 = 1.0 / (1.0 + e2)
    w2 = e2 * w1
    lane8 = lax.broadcasted_iota(jnp.int32, route_ref.shape[1:], 1)
    route = jnp.where(lane8 == 0, i1.astype(F32),
                      jnp.where(lane8 == 1, i2.astype(F32),
                                jnp.where(lane8 == 2, w1, jnp.where(lane8 == 3, w2, 0.0))))
    route_ref[0] = route


def moe_pre(h, mod, norm_w, router_w_pad, router_b_pad, *, n_experts):
    b, t, d = h.shape
    tm = TOKEN_TILE
    tok = lambda bi, i: (bi, i, 0)
    const = lambda bi, i: (0, 0)
    return pl.pallas_call(
        functools.partial(_moe_pre_kernel, n_experts=n_experts),
        grid=(b, t // tm),
        in_specs=[pl.BlockSpec((1, tm, d), tok),
                  pl.BlockSpec((1, 6, d), lambda bi, i: (bi, 0, 0)),
                  pl.BlockSpec((1, d), const),
                  pl.BlockSpec(router_w_pad.shape, const),
                  pl.BlockSpec((1, 128), const)],
        out_specs=(pl.BlockSpec((1, tm, d), tok), pl.BlockSpec((1, tm, 8), tok)),
        out_shape=(jax.ShapeDtypeStruct((b, t, d), BF16), jax.ShapeDtypeStruct((b, t, 8), F32)),
        compiler_params=_cparams("parallel", "parallel"),
        name="moe_pre",
    )(h, mod, norm_w, router_w_pad, router_b_pad)


def _moe_gmm_kernel(be_ref, nu_ref, x_ref, wg_ref, wu_ref, wd_ref, o_ref, acc_sc):
    i, j = pl.program_id(0), pl.program_id(1)
    nj = pl.num_programs(1)
    used = i < nu_ref[0]

    @pl.when(used)
    def _():
        x = x_ref[...]
        hg = jnp.dot(x, wg_ref[0], preferred_element_type=F32)
        hu = jnp.dot(x, wu_ref[0], preferred_element_type=F32)
        y = jnp.dot((_silu(hg) * hu).astype(BF16), wd_ref[0], preferred_element_type=F32)

        @pl.when(j == 0)
        def _():
            acc_sc[...] = y

        @pl.when(j > 0)
        def _():
            acc_sc[...] += y

    @pl.when(j == nj - 1)
    def _():
        o_ref[...] = jnp.where(used, acc_sc[...], 0.0)


def moe_gmm(block_e, n_used, xb, wg, wu, wd):
    n_slots, d = xb.shape
    tm, tf = MOE_TILE, MOE_FF_TILE
    n_blocks = n_slots // tm
    f = wg.shape[2]
    nj = f // tf
    last = nj - 1

    def jj(i, j, nu):
        return jnp.where(i < nu[0], j, last)

    grid_spec = pltpu.PrefetchScalarGridSpec(
        num_scalar_prefetch=2,
        grid=(n_blocks, nj),
        in_specs=[pl.BlockSpec((tm, d), lambda i, j, be, nu: (i, 0)),
                  pl.BlockSpec((1, d, tf), lambda i, j, be, nu: (be[i], 0, jj(i, j, nu))),
                  pl.BlockSpec((1, d, tf), lambda i, j, be, nu: (be[i], 0, jj(i, j, nu))),
                  pl.BlockSpec((1, tf, d), lambda i, j, be, nu: (be[i], jj(i, j, nu), 0))],
        out_specs=pl.BlockSpec((tm, d), lambda i, j, be, nu: (i, 0)),
        scratch_shapes=[pltpu.VMEM((tm, d), F32)],
    )
    return pl.pallas_call(
        _moe_gmm_kernel,
        grid_spec=grid_spec,
        out_shape=jax.ShapeDtypeStruct((n_slots, d), F32),
        compiler_params=_cparams("arbitrary", "arbitrary"),
        name="moe_gmm",
    )(block_e, n_used, xb, wg, wu, wd)


def _moe_combine_kernel(h_ref, mod_ref, route_ref, y1_ref, y2_ref, o_ref):
    r = route_ref[0]
    y = r[:, 2:3] * y1_ref[0] + r[:, 3:4] * y2_ref[0]
    o_ref[0] = h_ref[0] + mod_ref[0, 5:6, :] * y


def moe_combine(h, mod, route, y1, y2):
    b, t, d = h.shape
    tm = TOKEN_TILE
    tok = lambda bi, i: (bi, i, 0)
    return pl.pallas_call(
        _moe_combine_kernel,
        grid=(b, t // tm),
        in_specs=[pl.BlockSpec((1, tm, d), tok),
                  pl.BlockSpec((1, 6, d), lambda bi, i: (bi, 0, 0)),
                  pl.BlockSpec((1, tm, 8), tok),
                  pl.BlockSpec((1, tm, d), tok),
                  pl.BlockSpec((1, tm, d), tok)],
        out_specs=pl.BlockSpec((1, tm, d), tok),
        out_shape=jax.ShapeDtypeStruct((b, t, d), F32),
        compiler_params=_cparams("parallel", "parallel"),
        name="moe_combine",
    )(h, mod, route, y1, y2)


def _moe_dispatch_plan(expert_idx, n_experts):
    n = expert_idx.shape[0]
    tile = MOE_TILE
    flat_e = expert_idx.reshape(-1)
    onehot = (flat_e[None, :] == jnp.arange(n_experts, dtype=jnp.int32)[:, None]).astype(jnp.int32)
    csum = jnp.cumsum(onehot, axis=1)
    counts = csum[:, -1]
    padded = (counts + tile - 1) // tile * tile
    pad_ends = jnp.cumsum(padded)
    pad_starts = pad_ends - padded
    slot = jnp.sum(onehot * (pad_starts[:, None] + csum - 1), axis=0)
    n_blocks = -(-(n * TOP_K) // tile) + n_experts
    n_used = pad_ends[-1] // tile
    blk = jnp.arange(n_blocks, dtype=jnp.int32)
    block_e = jnp.minimum(jnp.searchsorted(pad_ends, blk * tile, side="right"), n_experts - 1).astype(jnp.int32)
    block_e = jnp.where(blk < n_used, block_e, block_e[jnp.maximum(n_used - 1, 0)])
    slot_tok = jnp.zeros((n_blocks * tile,), jnp.int32).at[slot].set(
        jnp.arange(n * TOP_K, dtype=jnp.int32) // TOP_K, unique_indices=True, mode="promise_in_bounds")
    return slot.reshape(n, TOP_K), slot_tok, block_e, n_used.astype(jnp.int32).reshape(1)


def _rope_tables(n_ctx, n_lat):
    rows = n_lat // GRID_W
    row = jnp.repeat(jnp.arange(rows, dtype=F32), GRID_W)
    col = jnp.tile(jnp.arange(GRID_W, dtype=F32), rows)
    n_freq = HEAD_DIM // 4
    inv_freq = ROPE_THETA ** (-jnp.arange(n_freq, dtype=F32) / n_freq)
    ang = jnp.concatenate([row[:, None] * inv_freq, col[:, None] * inv_freq], -1)
    cos, sin = jnp.cos(ang), jnp.sin(ang)
    cos_t = jnp.concatenate([cos, cos, cos, cos], axis=1)
    sin_t = jnp.concatenate([-sin, sin, -sin, sin], axis=1)
    cos_t = jnp.concatenate([jnp.ones((n_ctx, 128), F32), cos_t], axis=0)
    sin_t = jnp.concatenate([jnp.zeros((n_ctx, 128), F32), sin_t], axis=0)
    return cos_t, sin_t


def _block_ones(n):
    idx = jnp.arange(n) // HEAD_DIM
    return (idx[:, None] == idx[None, :]).astype(BF16)


def kernel(x, c, ctx, c_ctx, mod_w, mod_b, norm1_w, norm2_w, w_in, attn_qn_w, attn_kn_w, ret_decay, ret_norm_w,
           mlstm_conv_w, mlstm_gate_b, mlstm_norm_w, w_out, ffn_w_gate, ffn_w_up, ffn_w_down, router_w, router_b,
           moe_w_gate, moe_w_up, moe_w_down):
    b, n_lat, d = x.shape
    n_ctx = ctx.shape[1]
    depth = mod_w.shape[0]
    n_experts = router_w.shape[2]
    ret_heads = ret_decay.shape[2]
    ml_heads = mlstm_gate_b.shape[2]
    ret_w, ml_w = ret_heads * HEAD_DIM, ml_heads * HEAD_DIM
    n_gate = N_GATES * ml_heads
    kv_w = (w_in.shape[2] - 4 * ret_w - 4 * ml_w - n_gate) // 6
    att_w = 4 * kv_w
    dims = (att_w, kv_w, ret_w, ml_w, n_gate)
    assert n_ctx % TOKEN_TILE == 0 and n_lat % KV_TILE == 0 and b + 1 <= 16
    n_main = w_in.shape[2] - n_gate
    n_pad = -(-w_in.shape[2] // 128) * 128

    cos_t, sin_t = _rope_tables(n_ctx, n_lat)
    gq, gk, g_ret, g_ml = _block_ones(att_w), _block_ones(kv_w), _block_ones(ret_w), _block_ones(ml_w)
    cond = jnp.concatenate([c, c_ctx[None, :], jnp.zeros((16 - b - 1, d), F32)], axis=0)
    ctx_tiles = n_ctx // TOKEN_TILE
    ctx_chunks = n_ctx // CHUNK

    h = jnp.concatenate([ctx, x], axis=1)
    for layer in range(depth):
        last = layer == depth - 1
        skip = ctx_tiles if last else 0
        mod = adaln(cond, mod_w[layer], mod_b[layer]).reshape(16, 6, d)[:b + 1]
        w_all = jnp.pad(w_in[layer].astype(BF16), ((0, 0), (0, n_pad - w_in.shape[2])))
        w_gate_t = w_in[layer][:, n_main:].T.astype(BF16)
        qn = jnp.tile(attn_qn_w[layer], att_w // HEAD_DIM)[None, :]
        kn = jnp.tile(attn_kn_w[layer], kv_w // HEAD_DIM)[None, :]
        k_norm_max = math.sqrt(HEAD_DIM) * jnp.max(jnp.abs(attn_kn_w[layer])) * (1.0 + 2.0 ** -6)
        q_norm_max = math.sqrt(HEAD_DIM) * jnp.max(jnp.abs(attn_qn_w[layer])) * (QK_SCALE * LOG2E)
        fast = (2.0 * q_norm_max * k_norm_max <= MAX_SCORE_SPAN_LOG2).astype(jnp.int32).reshape(1)
        kbound = jnp.full((1, att_w), q_norm_max * k_norm_max, F32)
        q, k, v, r, m, gcol, grow = proj_in(h, mod, norm1_w[layer][None, :], cos_t, sin_t, w_all, w_gate_t, qn, kn,
                                            kbound, gq, gk, n_ctx=n_ctx, dims=dims)
        att = attention(fast, q, k, v, n_ctx=n_ctx, skip=skip)
        dec = jnp.repeat(ret_decay[layer], HEAD_DIM, axis=1)
        sf, sb = ret_states(r, dec, n_ctx=n_ctx)
        ret = ret_out(r, sf, sb, dec, ret_norm_w[layer][None, :], g_ret, skip_chunks=ctx_chunks if last else 0)
        qk = mlstm_conv(m, mlstm_conv_w[layer], n_ctx=n_ctx)
        bias = mlstm_gate_b[layer].reshape(-1)
        cf, nmf, cb, nmb = mlstm_states(qk, m, gcol, bias[None, :], n_ctx=n_ctx, n_heads=ml_heads)
        ml = mlstm_out(qk, m, gcol, grow, bias[None, :], bias[:, None], cf, nmf, cb, nmb, mlstm_norm_w[layer][None, :],
                       g_ml, skip_chunks=ctx_chunks if last else 0, n_heads=ml_heads)
        h = proj_out(att, ret, ml, h, mod, w_out[layer].astype(BF16), n_ctx=n_ctx, skip=skip)
        if layer % 2 == 0:
            li = layer // 2
            assert not last, "dense FFN on a latent-only stream is not wired up"
            h = ffn_dense(h, mod, norm2_w[layer][None, :], ffn_w_gate[li].astype(BF16), ffn_w_up[li].astype(BF16),
                          ffn_w_down[li].astype(BF16), n_ctx=n_ctx)
        else:
            li = layer // 2
            assert last, "expert FFN on the combined stream is not wired up"
            mod_lat = mod[:b]
            rw = jnp.pad(router_w[li], ((0, 0), (0, 128 - n_experts)))
            rb = jnp.pad(router_b[li], (0, 128 - n_experts))[None, :]
            xn, route = moe_pre(h, mod_lat, norm2_w[layer][None, :], rw, rb, n_experts=n_experts)
            t = h.shape[1]
            route_flat = route.reshape(b * t, 8)
            slot, slot_tok, block_e, n_used = _moe_dispatch_plan(route_flat[:, 0:TOP_K].astype(jnp.int32), n_experts)
            xb = xn.reshape(b * t, d).at[slot_tok].get(mode="promise_in_bounds")
            yb = moe_gmm(block_e, n_used, xb, moe_w_gate[li].astype(BF16), moe_w_up[li].astype(BF16),
                         moe_w_down[li].astype(BF16))
            y1 = yb.at[slot[:, 0]].get(mode="promise_in_bounds").reshape(b, t, d)
            y2 = yb.at[slot[:, 1]].get(mode="promise_in_bounds").reshape(b, t, d)
            h = moe_combine(h, mod_lat, route, y1, y2)
    return h if h.shape[1] == n_lat else h[:, n_ctx:, :]
```
